```python
import jax, jax.numpy as jnp
from jax import lax
import numpy as np

D_MODEL = 2048
BATCH = 4
SEQ = 2048
DEPTH = 2
DEC_BATCH = 32
DEC_SEQ = 32
PAST_LEN = 1024

CHUNK = 64
N_MIXERS = 2
N_MLA_LAYERS = (DEPTH + N_MIXERS - 1) // N_MIXERS
N_LRU_LAYERS = DEPTH // N_MIXERS
N_HEADS = 16
Q_LORA = 512
KV_LORA = 512
NOPE_DIM = 128
ROPE_DIM = 64
V_DIM = 128
ROPE_THETA = 10000.0
Q_BLOCK = 128
ATTN_SCALE = (NOPE_DIM + ROPE_DIM) ** -0.5
D_RNN = D_MODEL
LRU_BLOCKS = 8
LRU_BLOCK_DIM = D_RNN // LRU_BLOCKS
CONV_W = 4
LRU_C = 8.0
N_EXPERTS = 32
TOP_K = 4
D_EXPERT = D_MODEL
SWIGLU_LIMIT = 7.0
SWIGLU_ALPHA = 1.702
MOE_BLOCK = 128
N_MOD = 6
EPS = 1e-6

kernel_name = 'hybrid_mla_rglru_moe_stream_step'

F32 = jnp.float32


def rmsnorm(x, g):
    xf = x.astype(F32)
    y = xf * lax.rsqrt(jnp.mean(xf * xf, axis=-1, keepdims=True) + EPS)
    return (y * g.astype(F32)).astype(x.dtype)


def adaln(c, w, b):
    mod = jax.nn.silu(c) @ w + b
    return jnp.split(mod[:, None, :], N_MOD, axis=-1)


def modulate(h, shift, scale):
    return h * (1 + scale) + shift


def rope_tables(pos):
    inv = 1.0 / (ROPE_THETA ** (jnp.arange(0, ROPE_DIM, 2, dtype=F32) / ROPE_DIM))
    ang = pos.astype(F32)[:, None] * inv[None, :]
    return jnp.cos(ang), jnp.sin(ang)


def apply_rope(x, cos, sin):
    x1, x2 = jnp.split(x.astype(F32), 2, axis=-1)
    return jnp.concatenate([x1 * cos - x2 * sin, x1 * sin + x2 * cos], axis=-1).astype(x.dtype)


def mla_project(h, pos, w_in, q_norm_g, kv_norm_g, w_uq):
    b, t, _ = h.shape
    lat = h @ w_in
    q_lat, c_kv, k_r = jnp.split(lat, [Q_LORA, Q_LORA + KV_LORA], axis=-1)
    q = (rmsnorm(q_lat, q_norm_g) @ w_uq).reshape(b, t, N_HEADS, NOPE_DIM + ROPE_DIM)
    cos, sin = rope_tables(pos)
    q_nope = q[..., :NOPE_DIM]
    q_rope = apply_rope(q[..., NOPE_DIM:], cos[:, None, :], sin[:, None, :])
    c_kv = rmsnorm(c_kv, kv_norm_g)
    k_rope = apply_rope(k_r, cos, sin)
    return q_nope, q_rope, c_kv, k_rope


def mla_expand(c_kv, w_ukv):
    b, t, _ = c_kv.shape
    kv = (c_kv @ w_ukv).reshape(b, t, N_HEADS, NOPE_DIM + V_DIM)
    return kv[..., :NOPE_DIM], kv[..., NOPE_DIM:]


def mla_attend(q_nope, q_rope, q_pos, k_nope, k_rope, v, k_pos):
    s = (jnp.einsum('bqhd,bkhd->bhqk', q_nope, k_nope)
         + jnp.einsum('bqhr,bkr->bhqk', q_rope, k_rope)).astype(F32) * ATTN_SCALE
    visible = (k_pos[None, :] // CHUNK) <= (q_pos[:, None] // CHUNK)
    s = jnp.where(visible, s, -jnp.inf)
    p = jax.nn.softmax(s, axis=-1).astype(v.dtype)
    return jnp.einsum('bhqk,bkhd->bqhd', p, v)


def mla_prompt(q_nope, q_rope, c_kv, k_rope, pos, w_ukv, w_o):
    b, t = q_nope.shape[:2]
    k_nope, v = mla_expand(c_kv, w_ukv)
    nb = t // Q_BLOCK

    def blk(a):
        return jnp.moveaxis(a.reshape(b, nb, Q_BLOCK, *a.shape[2:]), 1, 0)

    o = lax.map(lambda qs: mla_attend(qs[0], qs[1], qs[2], k_nope, k_rope, v, pos),
                (blk(q_nope), blk(q_rope), pos.reshape(nb, Q_BLOCK)))
    o = jnp.moveaxis(o, 0, 1).reshape(b, t, N_HEADS * V_DIM)
    return o @ w_o


def mla_sample(q_nope, q_rope, c_kv, k_rope, pos, cache_ckv, cache_krope, w_ukv, w_o):
    b, t = q_nope.shape[:2]
    c_all = jnp.concatenate([cache_ckv.astype(c_kv.dtype), c_kv], axis=1)
    kr_all = jnp.concatenate([cache_krope.astype(k_rope.dtype), k_rope], axis=1)
    k_nope, v = mla_expand(c_all, w_ukv)
    k_pos = jnp.arange(c_all.shape[1])
    o = mla_attend(q_nope, q_rope, pos, k_nope, kr_all, v, k_pos)
    return o.reshape(b, t, N_HEADS * V_DIM) @ w_o


def _lin_combine(e1, e2):
    a1, b1 = e1
    a2, b2 = e2
    return a1 * a2, a2 * b1 + b2


def rglru_block(h, pos, conv_buf, h0, w_in, conv_w, conv_b, w_a, b_a, w_x, b_x, lam, w_o):
    b, t, _ = h.shape
    y_br, x_br = jnp.split(h @ w_in, 2, axis=-1)
    y_br = jax.nn.gelu(y_br)
    xpad = jnp.concatenate([conv_buf.astype(x_br.dtype), x_br], axis=1)
    xc = conv_b + sum(xpad[:, k:k + t] * conv_w[k] for k in range(CONV_W))
    new_buf = xpad[:, t:]
    xg = xc.reshape(b, t, LRU_BLOCKS, LRU_BLOCK_DIM)
    r = jax.nn.sigmoid(jnp.einsum('btnc,ncd->btnd', xg, w_a).reshape(b, t, D_RNN) + b_a).astype(F32)
    i = jax.nn.sigmoid(jnp.einsum('btnc,ncd->btnd', xg, w_x).reshape(b, t, D_RNN) + b_x).astype(F32)
    log_a = LRU_C * r * jax.nn.log_sigmoid(lam.astype(F32))
    a = jnp.exp(log_a)
    mult = jnp.where((pos == 0)[None, :, None], 1.0, jnp.sqrt(-jnp.expm1(2.0 * log_a)))
    u = mult * i * xc.astype(F32)
    u = u.at[:, 0].add(a[:, 0] * h0.astype(F32))
    _, hs = lax.associative_scan(_lin_combine, (a, u), axis=1)
    out = (hs.astype(h.dtype) * y_br) @ w_o
    return out, new_buf, hs[:, -1]


def moe_ffn(h, router_w, router_b, w_gu, b_gu, w_down, b_down):
    shp = h.shape
    xt = h.reshape(-1, shp[-1])
    n = xt.shape[0]
    nk = n * TOP_K
    logits = (xt @ router_w + router_b).astype(F32)
    top_v, top_e = lax.top_k(logits, TOP_K)
    gates = jax.nn.softmax(top_v, axis=-1)
    flat_e = top_e.reshape(-1)
    flat_tok = jnp.repeat(jnp.arange(n, dtype=jnp.int32), TOP_K)
    order = jnp.argsort(flat_e)
    se = flat_e[order]
    counts = jnp.bincount(flat_e, length=N_EXPERTS)
    padded = (counts + MOE_BLOCK - 1) // MOE_BLOCK * MOE_BLOCK
    ends_p = jnp.cumsum(padded)
    starts_p = ends_p - padded
    starts = jnp.cumsum(counts) - counts
    dest = starts_p[se] + jnp.arange(nk) - starts[se]
    n_blocks = -(-nk // MOE_BLOCK) + N_EXPERTS
    rows = n_blocks * MOE_BLOCK
    row_tok = jnp.zeros(rows, jnp.int32).at[dest].set(flat_tok[order])
    row_gate = jnp.zeros(rows, F32).at[dest].set(gates.reshape(-1)[order])
    block_e = jnp.minimum(jnp.searchsorted(ends_p, jnp.arange(n_blocks) * MOE_BLOCK, side='right'),
                          N_EXPERTS - 1)
    xb = xt[row_tok].reshape(n_blocks, MOE_BLOCK, shp[-1])

    def expert_block(args):
        xe, e = args
        gu = xe @ w_gu[e] + b_gu[e]
        gate = jnp.minimum(gu[:, 0::2], SWIGLU_LIMIT)
        up = jnp.clip(gu[:, 1::2], -SWIGLU_LIMIT, SWIGLU_LIMIT)
        glu = gate * jax.nn.sigmoid(gate * SWIGLU_ALPHA)
        return ((up + 1) * glu) @ w_down[e] + b_down[e]

    yb = lax.map(expert_block, (xb, block_e)).reshape(rows, shp[-1])
    out = jnp.zeros_like(xt).at[row_tok].add(yb * row_gate[:, None].astype(yb.dtype))
    return out.reshape(shp)


def setup_inputs(seed: int = 0) -> dict:
    key = jax.random.key(seed)
    ks = iter(jax.random.split(key, 40))

    def nrm(shape, scale):
        return jax.random.normal(next(ks), shape, F32) * scale

    NA, NB, D = N_MLA_LAYERS, N_LRU_LAYERS, D_MODEL
    a0 = jax.random.uniform(next(ks), (NB, D_RNN), F32, minval=0.9, maxval=0.999)
    return {
        'x_prompt': nrm((BATCH, SEQ, D), 1.0),
        'x_sample': nrm((DEC_BATCH, DEC_SEQ, D), 1.0),
        'cache_ckv': nrm((NA, DEC_BATCH, PAST_LEN, KV_LORA), 1.0),
        'cache_krope': nrm((NA, DEC_BATCH, PAST_LEN, ROPE_DIM), 1.0),
        'state_conv': nrm((NB, DEC_BATCH, CONV_W - 1, D_RNN), 1.0),
        'state_h': nrm((NB, DEC_BATCH, D_RNN), 0.5),
        'c_prompt': nrm((BATCH, D), 1.0),
        'c_sample': nrm((DEC_BATCH, D), 1.0),
        'mod_w': nrm((DEPTH, D, N_MOD * D), 0.3 * D ** -0.5),
        'mod_b': nrm((DEPTH, N_MOD * D), 0.02),
        'norm1_g': 1.0 + nrm((DEPTH, D), 0.02),
        'norm2_g': 1.0 + nrm((DEPTH, D), 0.02),
        'mla_w_in': nrm((NA, D, Q_LORA + KV_LORA + ROPE_DIM), D ** -0.5),
        'mla_q_norm_g': 1.0 + nrm((NA, Q_LORA), 0.02),
        'mla_kv_norm_g': 1.0 + nrm((NA, KV_LORA), 0.02),
        'mla_w_uq': nrm((NA, Q_LORA, N_HEADS * (NOPE_DIM + ROPE_DIM)), Q_LORA ** -0.5),
        'mla_w_ukv': nrm((NA, KV_LORA, N_HEADS * (NOPE_DIM + V_DIM)), KV_LORA ** -0.5),
        'mla_w_o': nrm((NA, N_HEADS * V_DIM, D), (N_HEADS * V_DIM) ** -0.5),
        'lru_w_in': nrm((NB, D, 2 * D_RNN), D ** -0.5),
        'lru_conv_w': nrm((NB, CONV_W, D_RNN), CONV_W ** -0.5),
        'lru_conv_b': nrm((NB, D_RNN), 0.02),
        'lru_w_a': nrm((NB, LRU_BLOCKS, LRU_BLOCK_DIM, LRU_BLOCK_DIM), LRU_BLOCK_DIM ** -0.5),
        'lru_b_a': nrm((NB, D_RNN), 0.02),
        'lru_w_x': nrm((NB, LRU_BLOCKS, LRU_BLOCK_DIM, LRU_BLOCK_DIM), LRU_BLOCK_DIM ** -0.5),
        'lru_b_x': nrm((NB, D_RNN), 0.02),
        'lru_lambda': jnp.log(a0) - jnp.log1p(-a0),
        'lru_w_o': nrm((NB, D_RNN, D), D_RNN ** -0.5),
        'router_w': nrm((DEPTH, D, N_EXPERTS), D ** -0.5),
        'router_b': nrm((DEPTH, N_EXPERTS), 0.01),
        'moe_w_gu': nrm((DEPTH, N_EXPERTS, D, 2 * D_EXPERT), D ** -0.5),
        'moe_b_gu': nrm((DEPTH, N_EXPERTS, 2 * D_EXPERT), 0.02),
        'moe_w_down': nrm((DEPTH, N_EXPERTS, D_EXPERT, D), D_EXPERT ** -0.5),
        'moe_b_down': nrm((DEPTH, N_EXPERTS, D), 0.02),
        'final_g': 1.0 + nrm((D,), 0.02),
    }


def reference(x_prompt, x_sample, cache_ckv, cache_krope, state_conv, state_h, c_prompt, c_sample,
              mod_w, mod_b, norm1_g, norm2_g,
              mla_w_in, mla_q_norm_g, mla_kv_norm_g, mla_w_uq, mla_w_ukv, mla_w_o,
              lru_w_in, lru_conv_w, lru_conv_b, lru_w_a, lru_b_a, lru_w_x, lru_b_x, lru_lambda, lru_w_o,
              router_w, router_b, moe_w_gu, moe_b_gu, moe_w_down, moe_b_down, final_g):
    bp, tp = x_prompt.shape[:2]
    ts = x_sample.shape[1]
    past = cache_ckv.shape[2]
    pos_p = jnp.arange(tp)
    pos_s = past + jnp.arange(ts)
    xp, xs = x_prompt, x_sample
    ckv_p, kr_p, conv_p, h_p = [], [], [], []
    ckv_s, kr_s, conv_s, h_s = [], [], [], []
    for i in range(DEPTH):
        mp = adaln(c_prompt, mod_w[i], mod_b[i])
        ms = adaln(c_sample, mod_w[i], mod_b[i])
        hp = modulate(rmsnorm(xp, norm1_g[i]), mp[0], mp[1])
        hs = modulate(rmsnorm(xs, norm1_g[i]), ms[0], ms[1])
        j = i // N_MIXERS
        if i % N_MIXERS == 0:
            qn, qr, ckv, kr = mla_project(hp, pos_p, mla_w_in[j], mla_q_norm_g[j], mla_kv_norm_g[j], mla_w_uq[j])
            yp = mla_prompt(qn, qr, ckv, kr, pos_p, mla_w_ukv[j], mla_w_o[j])
            qn2, qr2, ckv2, kr2 = mla_project(hs, pos_s, mla_w_in[j], mla_q_norm_g[j], mla_kv_norm_g[j], mla_w_uq[j])
            ys = mla_sample(qn2, qr2, ckv2, kr2, pos_s, cache_ckv[j], cache_krope[j], mla_w_ukv[j], mla_w_o[j])
            ckv_p.append(ckv)
            kr_p.append(kr)
            ckv_s.append(ckv2)
            kr_s.append(kr2)
        else:
            lru_args = (lru_w_in[j], lru_conv_w[j], lru_conv_b[j], lru_w_a[j], lru_b_a[j],
                        lru_w_x[j], lru_b_x[j], lru_lambda[j], lru_w_o[j])
            zbuf = jnp.zeros((bp, CONV_W - 1, D_RNN), hp.dtype)
            zh = jnp.zeros((bp, D_RNN), F32)
            yp, cb, hl = rglru_block(hp, pos_p, zbuf, zh, *lru_args)
            ys, cb2, hl2 = rglru_block(hs, pos_s, state_conv[j], state_h[j], *lru_args)
            conv_p.append(cb)
            h_p.append(hl)
            conv_s.append(cb2)
            h_s.append(hl2)
        xp = xp + mp[2] * yp
        xs = xs + ms[2] * ys
        moe_w = (router_w[i], router_b[i], moe_w_gu[i], moe_b_gu[i], moe_w_down[i], moe_b_down[i])
        xp = xp + mp[5] * moe_ffn(modulate(rmsnorm(xp, norm2_g[i]), mp[3], mp[4]), *moe_w)
        xs = xs + ms[5] * moe_ffn(modulate(rmsnorm(xs, norm2_g[i]), ms[3], ms[4]), *moe_w)
    y_prompt = rmsnorm(xp, final_g)
    y_sample = rmsnorm(xs, final_g)
    return (y_prompt, y_sample,
            jnp.stack(ckv_p), jnp.stack(kr_p), jnp.stack(conv_p), jnp.stack(h_p),
            jnp.stack(ckv_s), jnp.stack(kr_s), jnp.stack(conv_s), jnp.stack(h_s))
```

```python
import functools
import math

import jax
import jax.numpy as jnp
import numpy as np
from jax import lax
from jax.experimental import pallas as pl
from jax.experimental.pallas import tpu as pltpu

F32 = jnp.float32
BF16 = jnp.bfloat16
I32 = jnp.int32
U32 = jnp.uint32

CHUNK = 64
N_HEADS = 16
Q_LORA = 512
KV_LORA = 512
NOPE_DIM = 128
ROPE_DIM = 64
V_DIM = 128
ROPE_THETA = 10000.0
LRU_BLOCKS = 8
CONV_W = 4
LRU_C = 8.0
N_EXPERTS = 32
TOP_K = 4
SWIGLU_LIMIT = 7.0
SWIGLU_ALPHA = 1.702
N_MOD = 6
EPS = 1e-6

LANES = 128
SUBLANES = 8
HEAD_W = 2 * LANES

ROW_TILE = 256
MOE_TM = 256
GATHER_TILE = 1024
VMEM_LIMIT = 56 * 1024 * 1024


def _cparams(sem):
    return pltpu.CompilerParams(dimension_semantics=sem, vmem_limit_bytes=VMEM_LIMIT)


def _rms(x, g):
    ms = jnp.mean(x * x, axis=-1, keepdims=True)
    return x * lax.rsqrt(ms + EPS) * g


def _norm_mod(x, g, shift, scale):
    tm, d = x.shape
    ng = shift.shape[0]
    y = _rms(x, g).reshape(ng, tm // ng, d)
    return (y * (1.0 + scale[:, None, :]) + shift[:, None, :]).reshape(tm, d)


def _gated_residual(x, gate, y):
    tm, d = x.shape
    ng = gate.shape[0]
    return (x.reshape(ng, tm // ng, d) + gate[:, None, :] * y.reshape(ng, tm // ng, d)).reshape(tm, d)


def _adaln_kernel(c_ref, w_ref, b_ref, o_ref):
    c = c_ref[...]
    a = (c * jax.nn.sigmoid(c)).astype(BF16)
    o_ref[...] = jnp.dot(a, w_ref[...].astype(BF16), preferred_element_type=F32) + b_ref[...]


def _adaln(c_all, w, b):
    bp, d = c_all.shape
    n = w.shape[1]
    tn = 1024
    return pl.pallas_call(
        _adaln_kernel,
        grid=(n // tn,),
        in_specs=[pl.BlockSpec((bp, d), lambda j: (0, 0)),
                  pl.BlockSpec((d, tn), lambda j: (0, j)),
                  pl.BlockSpec((1, tn), lambda j: (0, j))],
        out_specs=pl.BlockSpec((bp, tn), lambda j: (0, j)),
        out_shape=jax.ShapeDtypeStruct((bp, n), F32),
        compiler_params=_cparams(("arbitrary",)),
        name="adaln",
    )(c_all, w, b.reshape(1, n))


def _rope128(v, cos, sin):
    half = ROPE_DIM // 2
    lane = lax.broadcasted_iota(I32, v.shape, 1)
    sw = jnp.where(lane < half, pltpu.roll(v, LANES - half, 1), pltpu.roll(v, half, 1))
    return v * cos + sw * sin


def _mla_proj_kernel(x_ref, g1_ref, sh_ref, sc_ref, win_ref, qg_ref, kvg_ref, wuq_ref, cos_ref, sin_ref,
                     q_ref, ckv_ref, kr_ref, ckvb_ref, krp_ref):
    h = _norm_mod(x_ref[...], g1_ref[...], sh_ref[...], sc_ref[...]).astype(BF16)
    lat = jnp.dot(h, win_ref[...], preferred_element_type=F32)
    q_lat = lat[:, :Q_LORA]
    c_kv = lat[:, Q_LORA:Q_LORA + KV_LORA]
    k_r = lat[:, Q_LORA + KV_LORA:]
    qn = _rms(q_lat, qg_ref[...]).astype(BF16)
    q = jnp.dot(qn, wuq_ref[...], preferred_element_type=F32)
    cos = cos_ref[...]
    sin = sin_ref[...]
    scale = (NOPE_DIM + ROPE_DIM) ** -0.5
    for hh in range(N_HEADS):
        lo = hh * HEAD_W
        q_ref[:, lo:lo + LANES] = (q[:, lo:lo + LANES] * scale).astype(BF16)
        q_ref[:, lo + LANES:lo + HEAD_W] = (_rope128(q[:, lo + LANES:lo + HEAD_W], cos, sin) * scale).astype(BF16)
    ckv = _rms(c_kv, kvg_ref[...])
    ckv_ref[...] = ckv
    ckvb_ref[...] = ckv.astype(BF16)
    kr = _rope128(k_r, cos, sin)
    kr_ref[...] = kr[:, :ROPE_DIM]
    krp_ref[...] = kr.astype(BF16)


def _mla_proj(x, g1, modg, win, qg, kvg, wuq, cos, sin, group):
    n, d = x.shape
    tm = ROW_TILE
    ng = tm // group
    wl = win.shape[1]
    qw = wuq.shape[1]
    row = lambda i: (i, 0)
    const = lambda i: (0, 0)
    return pl.pallas_call(
        _mla_proj_kernel,
        grid=(n // tm,),
        in_specs=[pl.BlockSpec((tm, d), row),
                  pl.BlockSpec((1, d), const),
                  pl.BlockSpec((ng, d), lambda i: (i, 0)),
                  pl.BlockSpec((ng, d), lambda i: (i, 1)),
                  pl.BlockSpec((d, wl), const),
                  pl.BlockSpec((1, Q_LORA), const),
                  pl.BlockSpec((1, KV_LORA), const),
                  pl.BlockSpec((Q_LORA, qw), const),
                  pl.BlockSpec((tm, LANES), row),
                  pl.BlockSpec((tm, LANES), row)],
        out_specs=[pl.BlockSpec((tm, qw), row),
                   pl.BlockSpec((tm, KV_LORA), row),
                   pl.BlockSpec((tm, ROPE_DIM), row),
                   pl.BlockSpec((tm, KV_LORA), row),
                   pl.BlockSpec((tm, LANES), row)],
        out_shape=[jax.ShapeDtypeStruct((n, qw), BF16),
                   jax.ShapeDtypeStruct((n, KV_LORA), F32),
                   jax.ShapeDtypeStruct((n, ROPE_DIM), F32),
                   jax.ShapeDtypeStruct((n, KV_LORA), BF16),
                   jax.ShapeDtypeStruct((n, LANES), BF16)],
        compiler_params=_cparams(("arbitrary",)),
        name="mla_proj",
    )(x, g1.reshape(1, d), modg, modg, win, qg.reshape(1, -1), kvg.reshape(1, -1), wuq, cos, sin)


def _kv_expand_kernel(c_ref, krp_ref, wk_ref, wv_ref, k_ref, v_ref):
    c = c_ref[...]
    kn = jnp.dot(c, wk_ref[...], preferred_element_type=F32).astype(BF16)
    krp = krp_ref[...]
    for hh in range(N_HEADS):
        k_ref[:, hh * HEAD_W:hh * HEAD_W + LANES] = kn[:, hh * NOPE_DIM:(hh + 1) * NOPE_DIM]
        k_ref[:, hh * HEAD_W + LANES:(hh + 1) * HEAD_W] = krp
    v_ref[...] = jnp.dot(c, wv_ref[...], preferred_element_type=F32).astype(BF16)


def _kv_expand(ckvb, krp, wk, wv, rows):
    tm = 512
    row = lambda i: (i, 0)
    const = lambda i: (0, 0)
    return pl.pallas_call(
        _kv_expand_kernel,
        grid=(rows // tm,),
        in_specs=[pl.BlockSpec((tm, KV_LORA), row),
                  pl.BlockSpec((tm, LANES), row),
                  pl.BlockSpec(wk.shape, const),
                  pl.BlockSpec(wv.shape, const)],
        out_specs=[pl.BlockSpec((tm, N_HEADS * HEAD_W), row),
                   pl.BlockSpec((tm, N_HEADS * V_DIM), row)],
        out_shape=[jax.ShapeDtypeStruct((rows, N_HEADS * HEAD_W), BF16),
                   jax.ShapeDtypeStruct((rows, N_HEADS * V_DIM), BF16)],
        compiler_params=_cparams(("arbitrary",)),
        name="kv_expand",
    )(ckvb, krp, wk, wv)


def _attn_prompt_kernel(q_ref, k_ref, v_ref, o_ref, *, tq):
    qi = pl.program_id(2)
    q = q_ref[...]
    dn = (((1,), (1,)), ((), ()))

    def tile(j, carry, masked):
        m, l, acc = carry
        k = k_ref[pl.ds(pl.multiple_of(j * tq, tq), tq), :]
        v = v_ref[pl.ds(pl.multiple_of(j * tq, tq), tq), :]
        s = lax.dot_general(q, k, dn, preferred_element_type=F32)
        if masked:
            r = lax.broadcasted_iota(I32, s.shape, 0) // CHUNK
            c = lax.broadcasted_iota(I32, s.shape, 1) // CHUNK
            s = jnp.where(c <= r, s, -jnp.inf)
        m_new = jnp.maximum(m, jnp.max(s, axis=-1, keepdims=True))
        alpha = jnp.exp(m - m_new)
        p = jnp.exp(s - m_new)
        l = alpha * l + jnp.sum(p, axis=-1, keepdims=True)
        acc = alpha * acc + jnp.dot(p.astype(BF16), v, preferred_element_type=F32)
        return m_new, l, acc

    init = (jnp.full((tq, 1), -jnp.inf, F32), jnp.zeros((tq, 1), F32), jnp.zeros((tq, V_DIM), F32))
    carry = lax.fori_loop(0, qi, lambda j, c: tile(j, c, False), init)
    m, l, acc = tile(qi, carry, True)
    o_ref[...] = (acc / l).astype(BF16)


def _attn_prompt(q, k, v, bp, tp):
    n_rows = bp * tp
    tq = 256
    nq = tp // tq
    return pl.pallas_call(
        functools.partial(_attn_prompt_kernel, tq=tq),
        grid=(bp, N_HEADS, nq),
        in_specs=[pl.BlockSpec((tq, HEAD_W), lambda b, h, i: (b * nq + i, h)),
                  pl.BlockSpec((tp, HEAD_W), lambda b, h, i: (b, h)),
                  pl.BlockSpec((tp, V_DIM), lambda b, h, i: (b, h))],
        out_specs=pl.BlockSpec((tq, V_DIM), lambda b, h, i: (b * nq + i, h)),
        out_shape=jax.ShapeDtypeStruct((n_rows, N_HEADS * V_DIM), BF16),
        compiler_params=_cparams(("arbitrary", "arbitrary", "arbitrary")),
        name="attn_prompt",
    )(q, k, v)


def _absorb_kernel(q_ref, wk_ref, o_ref):
    dn = (((1,), (1,)), ((), ()))
    o_ref[0] = lax.dot_general(q_ref[...], wk_ref[...], dn, preferred_element_type=F32).astype(BF16)


def _absorb(q, wk, row0, rows):
    rb = row0 // rows
    return pl.pallas_call(
        _absorb_kernel,
        grid=(N_HEADS,),
        in_specs=[pl.BlockSpec((rows, LANES), lambda h: (rb, 2 * h)),
                  pl.BlockSpec((KV_LORA, NOPE_DIM), lambda h: (0, h))],
        out_specs=pl.BlockSpec((1, rows, KV_LORA), lambda h: (h, 0, 0)),
        out_shape=jax.ShapeDtypeStruct((N_HEADS, rows, KV_LORA), BF16),
        compiler_params=_cparams(("arbitrary",)),
        name="absorb",
    )(q, wk)


def _attn_sample_kernel(qa_ref, q_ref, cc_ref, ckr_ref, cn_ref, krn_ref, o_ref, *, ts, past):
    hn = N_HEADS
    qa = qa_ref[...].reshape(hn * ts, KV_LORA)
    qfull = q_ref[...]
    qr = jnp.concatenate([qfull[:, h * HEAD_W + LANES:(h + 1) * HEAD_W] for h in range(hn)], axis=0)
    cc = cc_ref[0].astype(BF16)
    ckr = ckr_ref[0].astype(BF16)
    cn = cn_ref[...]
    krn = krn_ref[...]
    dn = (((1,), (1,)), ((), ()))
    s_c = (lax.dot_general(qa, cc, dn, preferred_element_type=F32)
           + lax.dot_general(qr, ckr, dn, preferred_element_type=F32))
    s_n = (lax.dot_general(qa, cn, dn, preferred_element_type=F32)
           + lax.dot_general(qr, krn, dn, preferred_element_type=F32))
    qchunk_c = (past + lax.broadcasted_iota(I32, s_c.shape, 0) % ts) // CHUNK
    s_c = jnp.where(lax.broadcasted_iota(I32, s_c.shape, 1) // CHUNK <= qchunk_c, s_c, -jnp.inf)
    qchunk_n = (past + lax.broadcasted_iota(I32, s_n.shape, 0) % ts) // CHUNK
    s_n = jnp.where((past + lax.broadcasted_iota(I32, s_n.shape, 1)) // CHUNK <= qchunk_n, s_n, -jnp.inf)
    m = jnp.maximum(jnp.max(s_c, axis=-1, keepdims=True), jnp.max(s_n, axis=-1, keepdims=True))
    p_c = jnp.exp(s_c - m)
    p_n = jnp.exp(s_n - m)
    l = jnp.sum(p_c, axis=-1, keepdims=True) + jnp.sum(p_n, axis=-1, keepdims=True)
    o = (jnp.dot(p_c.astype(BF16), cc, preferred_element_type=F32)
         + jnp.dot(p_n.astype(BF16), cn, preferred_element_type=F32)) / l
    o_ref[...] = o.astype(BF16).reshape(hn, ts, KV_LORA)


def _attn_sample(qa, q, cache_c, cache_kr, ckvb, krp, row0, bs, ts, past):
    rb0 = row0 // ts
    return pl.pallas_call(
        functools.partial(_attn_sample_kernel, ts=ts, past=past),
        grid=(bs,),
        in_specs=[pl.BlockSpec((N_HEADS, ts, KV_LORA), lambda b: (0, b, 0)),
                  pl.BlockSpec((ts, N_HEADS * HEAD_W), lambda b: (rb0 + b, 0)),
                  pl.BlockSpec((1, past, KV_LORA), lambda b: (b, 0, 0)),
                  pl.BlockSpec((1, past, LANES), lambda b: (b, 0, 0)),
                  pl.BlockSpec((ts, KV_LORA), lambda b: (rb0 + b, 0)),
                  pl.BlockSpec((ts, LANES), lambda b: (rb0 + b, 0))],
        out_specs=pl.BlockSpec((N_HEADS, ts, KV_LORA), lambda b: (0, b, 0)),
        out_shape=jax.ShapeDtypeStruct((N_HEADS, bs * ts, KV_LORA), BF16),
        compiler_params=_cparams(("arbitrary",)),
        name="attn_sample",
    )(qa, q, cache_c, cache_kr, ckvb, krp)


def _unabsorb_kernel(ol_ref, wv_ref, o_ref):
    o_ref[...] = jnp.dot(ol_ref[0], wv_ref[...], preferred_element_type=F32).astype(BF16)


def _unabsorb(o_lat, wv, rows):
    return pl.pallas_call(
        _unabsorb_kernel,
        grid=(N_HEADS,),
        in_specs=[pl.BlockSpec((1, rows, KV_LORA), lambda h: (h, 0, 0)),
                  pl.BlockSpec((KV_LORA, V_DIM), lambda h: (0, h))],
        out_specs=pl.BlockSpec((rows, V_DIM), lambda h: (0, h)),
        out_shape=jax.ShapeDtypeStruct((rows, N_HEADS * V_DIM), BF16),
        compiler_params=_cparams(("arbitrary",)),
        name="unabsorb",
    )(o_lat, wv)


def _pack_halves(h):
    d = h.shape[1]
    bits = lax.bitcast_convert_type(h.astype(BF16).astype(F32), U32)
    return (bits[:, :d // 2] & jnp.uint32(0xFFFF0000)) | (bits[:, d // 2:] >> 16)


def _unpack_halves(p):
    a = lax.bitcast_convert_type(p & jnp.uint32(0xFFFF0000), F32).astype(BF16)
    b = lax.bitcast_convert_type(p << 16, F32).astype(BF16)
    return a, b


def _post_mixer_kernel(op_ref, os_ref, wo_ref, x_ref, gate_ref, g2_ref, sh_ref, sc_ref, rw_ref, rb_ref,
                       x1_ref, hp_ref, te_ref, tg_ref, *, prompt_tiles):
    o = jnp.where(pl.program_id(0) < prompt_tiles, op_ref[...], os_ref[...])
    y = jnp.dot(o, wo_ref[...], preferred_element_type=F32)
    x1 = _gated_residual(x_ref[...], gate_ref[...], y)
    x1_ref[...] = x1
    h2 = _norm_mod(x1, g2_ref[...], sh_ref[...], sc_ref[...])
    hp_ref[...] = _pack_halves(h2)
    logits = jnp.dot(h2, rw_ref[...], precision=lax.Precision.HIGHEST, preferred_element_type=F32) + rb_ref[...]
    tm, ne = logits.shape
    eid = lax.broadcasted_iota(I32, (tm, ne), 1)
    lane = lax.broadcasted_iota(I32, (tm, LANES), 1)
    te = jnp.zeros((tm, LANES), I32)
    tv = jnp.full((tm, LANES), -jnp.inf, F32)
    work = logits
    for k in range(TOP_K):
        mx = jnp.max(work, axis=-1, keepdims=True)
        idx = jnp.min(jnp.where(work == mx, eid, ne), axis=-1, keepdims=True)
        te = jnp.where(lane == k, idx, te)
        tv = jnp.where(lane == k, mx, tv)
        work = jnp.where(eid == idx, -jnp.inf, work)
    ex = jnp.exp(tv - jnp.max(tv, axis=-1, keepdims=True))
    te_ref[...] = te
    tg_ref[...] = ex / jnp.sum(ex, axis=-1, keepdims=True)


def _post_mixer(o_p, o_s, wo, x, g2, modg, rw, rb, group):
    n, d = x.shape
    tm = ROW_TILE
    ng = tm // group
    npt = o_p.shape[0] // tm
    nst = o_s.shape[0] // tm
    row = lambda i: (i, 0)
    const = lambda i: (0, 0)
    return pl.pallas_call(
        functools.partial(_post_mixer_kernel, prompt_tiles=npt),
        grid=(n // tm,),
        in_specs=[pl.BlockSpec((tm, o_p.shape[1]), lambda i: (jnp.minimum(i, npt - 1), 0)),
                  pl.BlockSpec((tm, o_s.shape[1]), lambda i: (jnp.clip(i - npt, 0, nst - 1), 0)),
                  pl.BlockSpec(wo.shape, const),
                  pl.BlockSpec((tm, d), row),
                  pl.BlockSpec((ng, d), lambda i: (i, 2)),
                  pl.BlockSpec((1, d), const),
                  pl.BlockSpec((ng, d), lambda i: (i, 3)),
                  pl.BlockSpec((ng, d), lambda i: (i, 4)),
                  pl.BlockSpec(rw.shape, const),
                  pl.BlockSpec((1, rw.shape[1]), const)],
        out_specs=[pl.BlockSpec((tm, d), row),
                   pl.BlockSpec((tm, d // 2), row),
                   pl.BlockSpec((tm, LANES), row),
                   pl.BlockSpec((tm, LANES), row)],
        out_shape=[jax.ShapeDtypeStruct((n, d), F32),
                   jax.ShapeDtypeStruct((n, d // 2), U32),
                   jax.ShapeDtypeStruct((n, LANES), I32),
                   jax.ShapeDtypeStruct((n, LANES), F32)],
        compiler_params=_cparams(("arbitrary",)),
        name="post_mixer",
    )(o_p, o_s, wo, x, modg, g2.reshape(1, d), modg, modg, rw, rb.reshape(1, -1))


def _dispatch_kernel(rt_ref, nb_ref, hp_ref, o_ref, sem, *, tg):
    i = pl.program_id(0)

    @pl.when(i < nb_ref[0])
    def _():
        base = i * tg

        def issue(r, c):
            tok = rt_ref[base + r]
            pltpu.make_async_copy(hp_ref.at[pl.ds(tok, 1), :], o_ref.at[pl.ds(r, 1), :], sem).start()
            return c

        lax.fori_loop(0, tg, issue, 0)
        pltpu.make_async_copy(hp_ref.at[pl.ds(0, tg), :], o_ref, sem).wait()

    @pl.when(i >= nb_ref[0])
    def _():
        o_ref[...] = jnp.zeros(o_ref.shape, o_ref.dtype)


def _dispatch(row_tok, n_gather_blocks, hp, rows_total):
    tg = GATHER_TILE
    w = hp.shape[1]
    return pl.pallas_call(
        functools.partial(_dispatch_kernel, tg=tg),
        grid_spec=pltpu.PrefetchScalarGridSpec(
            num_scalar_prefetch=2,
            grid=(rows_total // tg,),
            in_specs=[pl.BlockSpec(memory_space=pl.ANY)],
            out_specs=pl.BlockSpec((tg, w), lambda i, rt, nb: (i, 0)),
            scratch_shapes=[pltpu.SemaphoreType.DMA(())]),
        out_shape=jax.ShapeDtypeStruct((rows_total, w), U32),
        compiler_params=_cparams(("arbitrary",)),
        name="moe_dispatch",
    )(row_tok, n_gather_blocks, hp)


def _moe_gu_kernel(be_ref, nb_ref, xb_ref, wg_ref, wu_ref, bg_ref, bu_ref, o_ref):
    b = pl.program_id(1)

    @pl.when(b < nb_ref[0])
    def _():
        xa, xc = _unpack_halves(xb_ref[...])
        half = xa.shape[1]
        gate = (jnp.dot(xa, wg_ref[0, :half, :], preferred_element_type=F32)
                + jnp.dot(xc, wg_ref[0, half:, :], preferred_element_type=F32) + bg_ref[0])
        up = (jnp.dot(xa, wu_ref[0, :half, :], preferred_element_type=F32)
              + jnp.dot(xc, wu_ref[0, half:, :], preferred_element_type=F32) + bu_ref[0])
        gate = jnp.minimum(gate, SWIGLU_LIMIT)
        up = jnp.clip(up, -SWIGLU_LIMIT, SWIGLU_LIMIT)
        glu = gate * jax.nn.sigmoid(gate * SWIGLU_ALPHA)
        o_ref[...] = ((up + 1.0) * glu).astype(BF16)

    @pl.when(b >= nb_ref[0])
    def _():
        o_ref[...] = jnp.zeros(o_ref.shape, o_ref.dtype)


def _moe_gu(block_e, n_blocks_used, xb, wg, wu, bg, bu):
    rows, half = xb.shape
    ne, d, f = wg.shape
    tm = MOE_TM
    tn = 1024
    return pl.pallas_call(
        _moe_gu_kernel,
        grid_spec=pltpu.PrefetchScalarGridSpec(
            num_scalar_prefetch=2,
            grid=(f // tn, rows // tm),
            in_specs=[pl.BlockSpec((tm, half), lambda j, b, be, nb: (b, 0)),
                      pl.BlockSpec((1, d, tn), lambda j, b, be, nb: (be[b], 0, j)),
                      pl.BlockSpec((1, d, tn), lambda j, b, be, nb: (be[b], 0, j)),
                      pl.BlockSpec((1, 1, tn), lambda j, b, be, nb: (be[b], 0, j)),
                      pl.BlockSpec((1, 1, tn), lambda j, b, be, nb: (be[b], 0, j))],
            out_specs=pl.BlockSpec((tm, tn), lambda j, b, be, nb: (b, j))),
        out_shape=jax.ShapeDtypeStruct((rows, f), BF16),
        compiler_params=_cparams(("arbitrary", "arbitrary")),
        name="moe_gate_up",
    )(block_e, n_blocks_used, xb, wg, wu, bg.reshape(ne, 1, f), bu.reshape(ne, 1, f))


def _moe_down_kernel(be_ref, nb_ref, a_ref, wd_ref, bd_ref, o_ref):
    b = pl.program_id(1)

    @pl.when(b < nb_ref[0])
    def _():
        o_ref[...] = jnp.dot(a_ref[...], wd_ref[0], preferred_element_type=F32) + bd_ref[0]

    @pl.when(b >= nb_ref[0])
    def _():
        o_ref[...] = jnp.zeros(o_ref.shape, o_ref.dtype)


def _moe_down(block_e, n_blocks_used, act, wd, bd):
    rows, f = act.shape
    ne, _, d = wd.shape
    tm = MOE_TM
    tn = 1024
    return pl.pallas_call(
        _moe_down_kernel,
        grid_spec=pltpu.PrefetchScalarGridSpec(
            num_scalar_prefetch=2,
            grid=(d // tn, rows // tm),
            in_specs=[pl.BlockSpec((tm, f), lambda j, b, be, nb: (b, 0)),
                      pl.BlockSpec((1, f, tn), lambda j, b, be, nb: (be[b], 0, j)),
                      pl.BlockSpec((1, 1, tn), lambda j, b, be, nb: (be[b], 0, j))],
            out_specs=pl.BlockSpec((tm, tn), lambda j, b, be, nb: (b, j))),
        out_shape=jax.ShapeDtypeStruct((rows, d), F32),
        compiler_params=_cparams(("arbitrary", "arbitrary")),
        name="moe_down",
    )(block_e, n_blocks_used, act, wd, bd.reshape(ne, 1, d))


def _combine_kernel(dest_ref, y_ref, x1_ref, gm_ref, tg_ref, fg_ref, o_ref, buf, sem, *, tn, final_norm):
    i = pl.program_id(0)

    def issue(t, c):
        for k in range(TOP_K):
            d = dest_ref[(i * tn + t) * TOP_K + k]
            pltpu.make_async_copy(y_ref.at[pl.ds(d, 1), :], buf.at[k, pl.ds(t, 1), :], sem).start()
        return c

    lax.fori_loop(0, tn, issue, 0)
    for k in range(TOP_K):
        pltpu.make_async_copy(y_ref.at[pl.ds(0, tn), :], buf.at[k], sem).wait()
    tg = tg_ref[...]
    moe = tg[:, 0:1] * buf[0]
    for k in range(1, TOP_K):
        moe = moe + tg[:, k:k + 1] * buf[k]
    x2 = _gated_residual(x1_ref[...], gm_ref[...], moe)
    if final_norm:
        x2 = _rms(x2, fg_ref[...])
    o_ref[...] = x2


def _combine(dest, y, x1, modg, tgates, fg, group, final_norm):
    n, d = x1.shape
    tn = ROW_TILE
    ng = tn // group
    return pl.pallas_call(
        functools.partial(_combine_kernel, tn=tn, final_norm=final_norm),
        grid_spec=pltpu.PrefetchScalarGridSpec(
            num_scalar_prefetch=1,
            grid=(n // tn,),
            in_specs=[pl.BlockSpec(memory_space=pl.ANY),
                      pl.BlockSpec((tn, d), lambda i, ds: (i, 0)),
                      pl.BlockSpec((ng, d), lambda i, ds: (i, 5)),
                      pl.BlockSpec((tn, LANES), lambda i, ds: (i, 0)),
                      pl.BlockSpec((1, d), lambda i, ds: (0, 0))],
            out_specs=pl.BlockSpec((tn, d), lambda i, ds: (i, 0)),
            scratch_shapes=[pltpu.VMEM((TOP_K, tn, d), F32), pltpu.SemaphoreType.DMA(())]),
        out_shape=jax.ShapeDtypeStruct((n, d), F32),
        compiler_params=_cparams(("arbitrary",)),
        name="moe_combine",
    )(dest, y, x1, modg, tgates, fg.reshape(1, d))


def _moe_routing(te, n):
    ne = N_EXPERTS
    tm = MOE_TM
    rows_total = -(-(n * TOP_K + ne * tm) // GATHER_TILE) * GATHER_TILE
    onehot = jnp.sum((te[:, :, None] == jnp.arange(ne, dtype=I32)).astype(I32), axis=1)
    counts = jnp.sum(onehot, axis=0)
    rank = jnp.cumsum(onehot, axis=0) - onehot
    padded = (counts + tm - 1) // tm * tm
    ends_p = jnp.cumsum(padded)
    starts_p = ends_p - padded
    dest = starts_p[te] + jnp.take_along_axis(rank, te, axis=1)
    tok = jnp.broadcast_to(jnp.arange(n, dtype=I32)[:, None], (n, TOP_K))
    row_tok = jnp.zeros((rows_total,), I32).at[dest.reshape(-1)].set(tok.reshape(-1), unique_indices=True)
    n_blocks = rows_total // tm
    block_e = jnp.minimum(jnp.searchsorted(ends_p, jnp.arange(n_blocks, dtype=I32) * tm, side='right'),
                          ne - 1).astype(I32)
    used_rows = ends_p[-1]
    n_blocks_used = (used_rows // tm).astype(I32).reshape(1)
    n_gather_used = ((used_rows + GATHER_TILE - 1) // GATHER_TILE).astype(I32).reshape(1)
    return dest.reshape(-1).astype(I32), row_tok, block_e, n_blocks_used, n_gather_used, rows_total


def _moe(hp, te128, tg128, x1, modg, wg, wu, bg, bu, wd, bd, fg, group, final_norm):
    n = x1.shape[0]
    te = te128[:, :TOP_K]
    dest, row_tok, block_e, nbu, ngu, rows_total = _moe_routing(te, n)
    xb = _dispatch(row_tok, ngu, hp, rows_total)
    act = _moe_gu(block_e, nbu, xb, wg, wu, bg, bu)
    y = _moe_down(block_e, nbu, act, wd, bd)
    return _combine(dest, y, x1, modg, tg128, fg, group, final_norm)


def _lru_in_kernel(x_ref, g1_ref, sh_ref, sc_ref, wy_ref, wx_ref, y_ref, xb_ref):
    h = _norm_mod(x_ref[...], g1_ref[...], sh_ref[...], sc_ref[...]).astype(BF16)
    y = jnp.dot(h, wy_ref[...], preferred_element_type=F32)
    y_ref[...] = jax.nn.gelu(y, approximate=True).astype(BF16)
    xb_ref[...] = jnp.dot(h, wx_ref[...], preferred_element_type=F32)


def _lru_in(x, g1, modg, wy, wx, group):
    n, d = x.shape
    dr = wy.shape[1]
    tm = ROW_TILE
    ng = tm // group
    row = lambda i: (i, 0)
    const = lambda i: (0, 0)
    return pl.pallas_call(
        _lru_in_kernel,
        grid=(n // tm,),
        in_specs=[pl.BlockSpec((tm, d), row),
                  pl.BlockSpec((1, d), const),
                  pl.BlockSpec((ng, d), lambda i: (i, 0)),
                  pl.BlockSpec((ng, d), lambda i: (i, 1)),
                  pl.BlockSpec((d, dr), const),
                  pl.BlockSpec((d, dr), const)],
        out_specs=[pl.BlockSpec((tm, dr), row), pl.BlockSpec((tm, dr), row)],
        out_shape=[jax.ShapeDtypeStruct((n, dr), BF16), jax.ShapeDtypeStruct((n, dr), F32)],
        compiler_params=_cparams(("arbitrary",)),
        name="lru_in",
    )(x, g1.reshape(1, d), modg, modg, wy, wx)


def _lru_scan_kernel(y_ref, xb_ref, cb_ref, h0_ref, cw_ref, cbias_ref, wa_ref, ba_ref, wx_ref, bx_ref, lam_ref,
                     hy_ref, cbo_ref, ho_ref, xe, a_s, u_s, hc, *, tc, starts_at_pos0):
    c = pl.program_id(1)
    dr = xb_ref.shape[1]
    nb = wa_ref.shape[0]
    bd = dr // nb
    pre = SUBLANES

    @pl.when(c == 0)
    def _():
        xe[0:pre, :] = jnp.zeros((pre, dr), F32)
        xe[pre - (CONV_W - 1):pre, :] = cb_ref[0]
        hc[...] = jnp.broadcast_to(h0_ref[0], (SUBLANES, dr))

    xe[pre:pre + tc, :] = xb_ref[...]
    cw = cw_ref[...]
    xc = cbias_ref[...] + xe[pre:pre + tc, :] * cw[CONV_W - 1:CONV_W, :]
    for k in range(1, CONV_W):
        xc = xc + xe[pre - k:pre - k + tc, :] * cw[CONV_W - 1 - k:CONV_W - k, :]
    cbo_ref[0] = xe[pre + tc - (CONV_W - 1):pre + tc, :]
    xe[0:pre, :] = xe[tc:tc + pre, :]

    xcb = xc.astype(BF16)
    ra = jnp.concatenate([jnp.dot(xcb[:, n * bd:(n + 1) * bd], wa_ref[n], preferred_element_type=F32)
                          for n in range(nb)], axis=1)
    rx = jnp.concatenate([jnp.dot(xcb[:, n * bd:(n + 1) * bd], wx_ref[n], preferred_element_type=F32)
                          for n in range(nb)], axis=1)
    r = jax.nn.sigmoid(ra + ba_ref[...])
    ig = jax.nn.sigmoid(rx + bx_ref[...])
    lam = lam_ref[...]
    log_sig = jnp.minimum(lam, 0.0) - jnp.log1p(jnp.exp(-jnp.abs(lam)))
    log_a = LRU_C * r * log_sig
    a = jnp.exp(log_a)
    th = jnp.tanh(log_a)
    mult = jnp.sqrt(-2.0 * th / (1.0 - th))
    if starts_at_pos0:
        first = (lax.broadcasted_iota(I32, (tc, 1), 0) == 0) & (c == 0)
        mult = jnp.where(first, 1.0, mult)
    a_s[...] = a
    u_s[...] = mult * ig * xc

    row8 = lax.broadcasted_iota(I32, (SUBLANES, dr), 0)

    def group_step(g, hprev):
        off = pl.multiple_of(g * SUBLANES, SUBLANES)
        aa = a_s[pl.ds(off, SUBLANES), :]
        uu = u_s[pl.ds(off, SUBLANES), :]
        s = 1
        while s < SUBLANES:
            m = row8 >= s
            uu = jnp.where(m, aa * pltpu.roll(uu, s, 0) + uu, uu)
            aa = jnp.where(m, aa * pltpu.roll(aa, s, 0), aa)
            s *= 2
        hh = aa * hprev + uu
        u_s[pl.ds(off, SUBLANES), :] = hh
        return jnp.broadcast_to(hh[SUBLANES - 1:SUBLANES, :], (SUBLANES, dr))

    hlast = lax.fori_loop(0, tc // SUBLANES, group_step, hc[...])
    hc[...] = hlast
    ho_ref[0] = hlast[0:1, :]
    hy_ref[...] = (u_s[...] * y_ref[...].astype(F32)).astype(BF16)


def _lru_scan(yb, xb, conv_buf, h0, cw, cbias, wa, ba, wx, bx, lam, row0, n_seq, t, tc, starts_at_pos0):
    dr = xb.shape[1]
    nc = t // tc
    rb0 = row0 // tc
    inmap = lambda s, c: (rb0 + s * nc + c, 0)
    outmap = lambda s, c: (s * nc + c, 0)
    const2 = lambda s, c: (0, 0)
    const3 = lambda s, c: (0, 0, 0)
    seq3 = lambda s, c: (s, 0, 0)
    return pl.pallas_call(
        functools.partial(_lru_scan_kernel, tc=tc, starts_at_pos0=starts_at_pos0),
        grid=(n_seq, nc),
        in_specs=[pl.BlockSpec((tc, dr), inmap),
                  pl.BlockSpec((tc, dr), inmap),
                  pl.BlockSpec((1, CONV_W - 1, dr), seq3),
                  pl.BlockSpec((1, 1, dr), seq3),
                  pl.BlockSpec((CONV_W, dr), const2),
                  pl.BlockSpec((1, dr), const2),
                  pl.BlockSpec(wa.shape, const3),
                  pl.BlockSpec((1, dr), const2),
                  pl.BlockSpec(wx.shape, const3),
                  pl.BlockSpec((1, dr), const2),
                  pl.BlockSpec((1, dr), const2)],
        out_specs=[pl.BlockSpec((tc, dr), outmap),
                   pl.BlockSpec((1, CONV_W - 1, dr), seq3),
                   pl.BlockSpec((1, 1, dr), seq3)],
        out_shape=[jax.ShapeDtypeStruct((n_seq * t, dr), BF16),
                   jax.ShapeDtypeStruct((n_seq, CONV_W - 1, dr), F32),
                   jax.ShapeDtypeStruct((n_seq, 1, dr), F32)],
        scratch_shapes=[pltpu.VMEM((SUBLANES + tc, dr), F32),
                        pltpu.VMEM((tc, dr), F32),
                        pltpu.VMEM((tc, dr), F32),
                        pltpu.VMEM((SUBLANES, dr), F32)],
        compiler_params=_cparams(("arbitrary", "arbitrary")),
        name="lru_scan",
    )(yb, xb, conv_buf, h0.reshape(n_seq, 1, dr), cw, cbias.reshape(1, dr), wa, ba.reshape(1, dr),
      wx, bx.reshape(1, dr), lam.reshape(1, dr))


def _rope_tables(pos):
    half = ROPE_DIM // 2
    inv = 1.0 / (ROPE_THETA ** (jnp.arange(0, ROPE_DIM, 2, dtype=F32) / ROPE_DIM))
    ang = pos.astype(F32)[:, None] * inv[None, :]
    cos, sin = jnp.cos(ang), jnp.sin(ang)
    z = jnp.zeros((pos.shape[0], LANES - ROPE_DIM), F32)
    return jnp.concatenate([cos, cos, z], axis=1), jnp.concatenate([-sin, sin, z], axis=1)


def _head_slab_weights(w_uq):
    ql = w_uq.shape[0]
    w = w_uq.reshape(ql, N_HEADS, NOPE_DIM + ROPE_DIM)
    z = jnp.zeros((ql, N_HEADS, HEAD_W - NOPE_DIM - ROPE_DIM), w.dtype)
    return jnp.concatenate([w, z], axis=2).reshape(ql, N_HEADS * HEAD_W).astype(BF16)


def kernel(x_prompt, x_sample, cache_ckv, cache_krope, state_conv, state_h, c_prompt, c_sample,
           mod_w, mod_b, norm1_g, norm2_g,
           mla_w_in, mla_q_norm_g, mla_kv_norm_g, mla_w_uq, mla_w_ukv, mla_w_o,
           lru_w_in, lru_conv_w, lru_conv_b, lru_w_a, lru_b_a, lru_w_x, lru_b_x, lru_lambda, lru_w_o,
           router_w, router_b, moe_w_gu, moe_b_gu, moe_w_down, moe_b_down, final_g):
    bp, tp, d = x_prompt.shape
    bs, ts, _ = x_sample.shape
    past = cache_ckv.shape[2]
    depth = mod_w.shape[0]
    n_p, n_s = bp * tp, bs * ts
    n = n_p + n_s
    group = math.gcd(tp, ts)
    assert group % SUBLANES == 0 and ROW_TILE % group == 0 and n_p % ROW_TILE == 0 and n_s % ROW_TILE == 0
    assert NOPE_DIM == LANES and V_DIM == LANES and ROPE_DIM <= LANES and n_p % n_s == 0

    x = jnp.concatenate([x_prompt.reshape(n_p, d), x_sample.reshape(n_s, d)], axis=0)
    nb = bp + bs
    nb_pad = -(-nb // SUBLANES) * SUBLANES
    c_all = jnp.concatenate([c_prompt, c_sample, jnp.zeros((nb_pad - nb, d), F32)], axis=0)
    grp_batch = np.concatenate([np.repeat(np.arange(bp), tp // group), bp + np.repeat(np.arange(bs), ts // group)])
    pos = jnp.concatenate([jnp.tile(jnp.arange(tp), bp), jnp.tile(past + jnp.arange(ts), bs)])
    cos, sin = _rope_tables(pos)

    outs = {k: [] for k in ("ckv_p", "kr_p", "conv_p", "h_p", "ckv_s", "kr_s", "conv_s", "h_s")}
    for i in range(depth):
        mod = _adaln(c_all, mod_w[i], mod_b[i])
        modg = jnp.take(mod, jnp.asarray(grp_batch), axis=0)
        j = i // 2
        if i % 2 == 0:
            w_in = mla_w_in[j]
            zpad = jnp.zeros((d, LANES - ROPE_DIM), F32)
            win = jnp.concatenate([w_in, zpad], axis=1).astype(BF16)
            wuq = _head_slab_weights(mla_w_uq[j])
            wukv = mla_w_ukv[j].reshape(KV_LORA, N_HEADS, NOPE_DIM + V_DIM)
            wk = wukv[:, :, :NOPE_DIM].reshape(KV_LORA, N_HEADS * NOPE_DIM).astype(BF16)
            wv = wukv[:, :, NOPE_DIM:].reshape(KV_LORA, N_HEADS * V_DIM).astype(BF16)
            q, ckv, kr, ckvb, krp = _mla_proj(x, norm1_g[i], modg, win, mla_q_norm_g[j], mla_kv_norm_g[j],
                                              wuq, cos, sin, group)
            kk, vv = _kv_expand(ckvb, krp, wk, wv, n_p)
            o_p = _attn_prompt(q, kk, vv, bp, tp)
            qa = _absorb(q, wk, n_p, n_s)
            ckr_pad = jnp.pad(cache_krope[j], ((0, 0), (0, 0), (0, LANES - ROPE_DIM)))
            o_lat = _attn_sample(qa, q, cache_ckv[j], ckr_pad, ckvb, krp, n_p, bs, ts, past)
            o_s = _unabsorb(o_lat, wv, n_s)
            wo = mla_w_o[j].astype(BF16)
            outs["ckv_p"].append(ckv[:n_p].reshape(bp, tp, KV_LORA))
            outs["kr_p"].append(kr[:n_p].reshape(bp, tp, ROPE_DIM))
            outs["ckv_s"].append(ckv[n_p:].reshape(bs, ts, KV_LORA))
            outs["kr_s"].append(kr[n_p:].reshape(bs, ts, ROPE_DIM))
        else:
            dr = lru_w_in.shape[2] // 2
            wy = lru_w_in[j][:, :dr].astype(BF16)
            wx = lru_w_in[j][:, dr:].astype(BF16)
            yb, xb = _lru_in(x, norm1_g[i], modg, wy, wx, group)
            wa = lru_w_a[j].astype(BF16)
            wxg = lru_w_x[j].astype(BF16)
            lru_args = (lru_conv_w[j], lru_conv_b[j], wa, lru_b_a[j], wxg, lru_b_x[j], lru_lambda[j])
            zbuf = jnp.zeros((bp, CONV_W - 1, dr), F32)
            zh = jnp.zeros((bp, dr), F32)
            o_p, cb_p, h_p = _lru_scan(yb, xb, zbuf, zh, *lru_args, 0, bp, tp, ROW_TILE, True)
            o_s, cb_s, h_s = _lru_scan(yb, xb, state_conv[j], state_h[j], *lru_args, n_p, bs, ts, ts, False)
            wo = lru_w_o[j].astype(BF16)
            outs["conv_p"].append(cb_p)
            outs["h_p"].append(h_p.reshape(bp, dr))
            outs["conv_s"].append(cb_s)
            outs["h_s"].append(h_s.reshape(bs, dr))
        x1, hp, te128, tg128 = _post_mixer(o_p, o_s, wo, x, norm2_g[i], modg, router_w[i], router_b[i], group)
        wgu = moe_w_gu[i]
        wg = wgu[:, :, 0::2].astype(BF16)
        wu = wgu[:, :, 1::2].astype(BF16)
        bg = moe_b_gu[i][:, 0::2]
        bu = moe_b_gu[i][:, 1::2]
        wd = moe_w_down[i].astype(BF16)
        x = _moe(hp, te128, tg128, x1, modg, wg, wu, bg, bu, wd, moe_b_down[i], final_g, group,
                 final_norm=(i == depth - 1))
    y_prompt = x[:n_p].reshape(bp, tp, d)
    y_sample = x[n_p:].reshape(bs, ts, d)
    return (y_prompt, y_sample,
            jnp.stack(outs["ckv_p"]), jnp.stack(outs["kr_p"]), jnp.stack(outs["conv_p"]), jnp.stack(outs["h_p"]),
            jnp.stack(outs["ckv_s"]), jnp.stack(outs["kr_s"]), jnp.stack(outs["conv_s"]), jnp.stack(outs["h_s"]))
```

```python
import functools
import math

import jax
import jax.numpy as jnp
import numpy as np
from jax import lax
from jax.experimental import pallas as pl
from jax.experimental.pallas import tpu as pltpu

F32 = jnp.float32
BF16 = jnp.bfloat16
I32 = jnp.int32

CHUNK = 64
N_HEADS = 16
Q_LORA = 512
KV_LORA = 512
NOPE_DIM = 128
ROPE_DIM = 64
V_DIM = 128
ROPE_THETA = 10000.0
LRU_BLOCKS = 8
CONV_W = 4
LRU_C = 8.0
N_EXPERTS = 32
TOP_K = 4
SWIGLU_LIMIT = 7.0
SWIGLU_ALPHA = 1.702
N_MOD = 6
EPS = 1e-6

LANES = 128
SUBLANES = 8
HEAD_W = 2 * LANES

ROW_TILE = 256
MOE_TM = 256
VMEM_LIMIT = 56 * 1024 * 1024


def _cparams(sem):
    return pltpu.CompilerParams(dimension_semantics=sem, vmem_limit_bytes=VMEM_LIMIT)


def _rms(x, g):
    ms = jnp.mean(x * x, axis=-1, keepdims=True)
    return x * lax.rsqrt(ms + EPS) * g


def _norm_mod(x, g, shift, scale):
    tm, d = x.shape
    ng = shift.shape[0]
    y = _rms(x, g).reshape(ng, tm // ng, d)
    return (y * (1.0 + scale[:, None, :]) + shift[:, None, :]).reshape(tm, d)


def _gated_residual(x, gate, y):
    tm, d = x.shape
    ng = gate.shape[0]
    return (x.reshape(ng, tm // ng, d) + gate[:, None, :] * y.reshape(ng, tm // ng, d)).reshape(tm, d)


def _adaln_kernel(c_ref, w_ref, b_ref, o_ref):
    c = c_ref[...]
    a = (c * jax.nn.sigmoid(c)).astype(BF16)
    o_ref[...] = jnp.dot(a, w_ref[...].astype(BF16), preferred_element_type=F32) + b_ref[...]


def _adaln(c_all, w, b):
    bp, d = c_all.shape
    n = w.shape[1]
    tn = 1024
    return pl.pallas_call(
        _adaln_kernel,
        grid=(n // tn,),
        in_specs=[pl.BlockSpec((bp, d), lambda j: (0, 0)),
                  pl.BlockSpec((d, tn), lambda j: (0, j)),
                  pl.BlockSpec((1, tn), lambda j: (0, j))],
        out_specs=pl.BlockSpec((bp, tn), lambda j: (0, j)),
        out_shape=jax.ShapeDtypeStruct((bp, n), F32),
        compiler_params=_cparams(("arbitrary",)),
        name="adaln",
    )(c_all, w, b.reshape(1, n))


def _rope128(v, cos, sin):
    half = ROPE_DIM // 2
    lane = lax.broadcasted_iota(I32, v.shape, 1)
    sw = jnp.where(lane < half, pltpu.roll(v, LANES - half, 1), pltpu.roll(v, half, 1))
    return v * cos + sw * sin


def _mla_proj_kernel(x_ref, g1_ref, sh_ref, sc_ref, win_ref, qg_ref, kvg_ref, wuq_ref, cos_ref, sin_ref,
                     q_ref, ckv_ref, kr_ref, ckvb_ref, krp_ref):
    h = _norm_mod(x_ref[...], g1_ref[...], sh_ref[...], sc_ref[...]).astype(BF16)
    lat = jnp.dot(h, win_ref[...], preferred_element_type=F32)
    q_lat = lat[:, :Q_LORA]
    c_kv = lat[:, Q_LORA:Q_LORA + KV_LORA]
    k_r = lat[:, Q_LORA + KV_LORA:]
    qn = _rms(q_lat, qg_ref[...]).astype(BF16)
    q = jnp.dot(qn, wuq_ref[...], preferred_element_type=F32)
    cos = cos_ref[...]
    sin = sin_ref[...]
    scale = (NOPE_DIM + ROPE_DIM) ** -0.5
    for hh in range(N_HEADS):
        lo = hh * HEAD_W
        q_ref[:, lo:lo + LANES] = (q[:, lo:lo + LANES] * scale).astype(BF16)
        q_ref[:, lo + LANES:lo + HEAD_W] = (_rope128(q[:, lo + LANES:lo + HEAD_W], cos, sin) * scale).astype(BF16)
    ckv = _rms(c_kv, kvg_ref[...])
    ckv_ref[...] = ckv
    ckvb_ref[...] = ckv.astype(BF16)
    kr = _rope128(k_r, cos, sin)
    kr_ref[...] = kr[:, :ROPE_DIM]
    krp_ref[...] = kr.astype(BF16)


def _mla_proj(x, g1, modg, win, qg, kvg, wuq, cos, sin, group):
    n, d = x.shape
    tm = ROW_TILE
    ng = tm // group
    wl = win.shape[1]
    qw = wuq.shape[1]
    row = lambda i: (i, 0)
    const = lambda i: (0, 0)
    return pl.pallas_call(
        _mla_proj_kernel,
        grid=(n // tm,),
        in_specs=[pl.BlockSpec((tm, d), row),
                  pl.BlockSpec((1, d), const),
                  pl.BlockSpec((ng, d), lambda i: (i, 0)),
                  pl.BlockSpec((ng, d), lambda i: (i, 1)),
                  pl.BlockSpec((d, wl), const),
                  pl.BlockSpec((1, Q_LORA), const),
                  pl.BlockSpec((1, KV_LORA), const),
                  pl.BlockSpec((Q_LORA, qw), const),
                  pl.BlockSpec((tm, LANES), row),
                  pl.BlockSpec((tm, LANES), row)],
        out_specs=[pl.BlockSpec((tm, qw), row),
                   pl.BlockSpec((tm, KV_LORA), row),
                   pl.BlockSpec((tm, ROPE_DIM), row),
                   pl.BlockSpec((tm, KV_LORA), row),
                   pl.BlockSpec((tm, LANES), row)],
        out_shape=[jax.ShapeDtypeStruct((n, qw), BF16),
                   jax.ShapeDtypeStruct((n, KV_LORA), F32),
                   jax.ShapeDtypeStruct((n, ROPE_DIM), F32),
                   jax.ShapeDtypeStruct((n, KV_LORA), BF16),
                   jax.ShapeDtypeStruct((n, LANES), BF16)],
        compiler_params=_cparams(("arbitrary",)),
        name="mla_proj",
    )(x, g1.reshape(1, d), modg, modg, win, qg.reshape(1, -1), kvg.reshape(1, -1), wuq, cos, sin)


def _kv_expand_kernel(c_ref, krp_ref, wk_ref, wv_ref, k_ref, v_ref):
    c = c_ref[...]
    kn = jnp.dot(c, wk_ref[...], preferred_element_type=F32).astype(BF16)
    krp = krp_ref[...]
    for hh in range(N_HEADS):
        k_ref[:, hh * HEAD_W:hh * HEAD_W + LANES] = kn[:, hh * NOPE_DIM:(hh + 1) * NOPE_DIM]
        k_ref[:, hh * HEAD_W + LANES:(hh + 1) * HEAD_W] = krp
    v_ref[...] = jnp.dot(c, wv_ref[...], preferred_element_type=F32).astype(BF16)


def _kv_expand(ckvb, krp, wk, wv, rows):
    tm = 512
    row = lambda i: (i, 0)
    const = lambda i: (0, 0)
    return pl.pallas_call(
        _kv_expand_kernel,
        grid=(rows // tm,),
        in_specs=[pl.BlockSpec((tm, KV_LORA), row),
                  pl.BlockSpec((tm, LANES), row),
                  pl.BlockSpec(wk.shape, const),
                  pl.BlockSpec(wv.shape, const)],
        out_specs=[pl.BlockSpec((tm, N_HEADS * HEAD_W), row),
                   pl.BlockSpec((tm, N_HEADS * V_DIM), row)],
        out_shape=[jax.ShapeDtypeStruct((rows, N_HEADS * HEAD_W), BF16),
                   jax.ShapeDtypeStruct((rows, N_HEADS * V_DIM), BF16)],
        compiler_params=_cparams(("arbitrary",)),
        name="kv_expand",
    )(ckvb, krp, wk, wv)


ATTN_HEADS_PER_STEP = 2


def _attn_prompt_kernel(q_ref, k_ref, v_ref, o_ref, *, tq):
    qi = pl.program_id(2)
    dn = (((1,), (1,)), ((), ()))
    hs = ATTN_HEADS_PER_STEP
    qs = [q_ref[:, h * HEAD_W:(h + 1) * HEAD_W] for h in range(hs)]

    def tile(j, carries, masked):
        rows = pl.ds(pl.multiple_of(j * tq, tq), tq)
        out = []
        for h in range(hs):
            m, l, acc = carries[h]
            k = k_ref[rows, h * HEAD_W:(h + 1) * HEAD_W]
            v = v_ref[rows, h * V_DIM:(h + 1) * V_DIM]
            s = lax.dot_general(qs[h], k, dn, preferred_element_type=F32)
            if masked:
                r = lax.broadcasted_iota(I32, s.shape, 0) // CHUNK
                c = lax.broadcasted_iota(I32, s.shape, 1) // CHUNK
                s = jnp.where(c <= r, s, -jnp.inf)
            m_new = jnp.maximum(m, jnp.max(s, axis=-1, keepdims=True))
            alpha = jnp.exp(m - m_new)
            p = jnp.exp(s - m_new)
            l = alpha * l + jnp.sum(p, axis=-1, keepdims=True)
            acc = alpha * acc + jnp.dot(p.astype(BF16), v, preferred_element_type=F32)
            out.append((m_new, l, acc))
        return tuple(out)

    init = tuple((jnp.full((tq, 1), -jnp.inf, F32), jnp.zeros((tq, 1), F32), jnp.zeros((tq, V_DIM), F32))
                 for _ in range(hs))
    carries = lax.fori_loop(0, qi, lambda j, c: tile(j, c, False), init)
    carries = tile(qi, carries, True)
    for h in range(hs):
        _, l, acc = carries[h]
        o_ref[:, h * V_DIM:(h + 1) * V_DIM] = (acc / l).astype(BF16)


def _attn_prompt(q, k, v, bp, tp):
    n_rows = bp * tp
    tq = 256
    nq = tp // tq
    hs = ATTN_HEADS_PER_STEP
    return pl.pallas_call(
        functools.partial(_attn_prompt_kernel, tq=tq),
        grid=(bp, N_HEADS // hs, nq),
        in_specs=[pl.BlockSpec((tq, hs * HEAD_W), lambda b, h, i: (b * nq + i, h)),
                  pl.BlockSpec((tp, hs * HEAD_W), lambda b, h, i: (b, h)),
                  pl.BlockSpec((tp, hs * V_DIM), lambda b, h, i: (b, h))],
        out_specs=pl.BlockSpec((tq, hs * V_DIM), lambda b, h, i: (b * nq + i, h)),
        out_shape=jax.ShapeDtypeStruct((n_rows, N_HEADS * V_DIM), BF16),
        compiler_params=_cparams(("arbitrary", "arbitrary", "arbitrary")),
        name="attn_prompt",
    )(q, k, v)


def _absorb_kernel(q_ref, wk_ref, o_ref):
    dn = (((1,), (1,)), ((), ()))
    o_ref[0] = lax.dot_general(q_ref[...], wk_ref[...], dn, preferred_element_type=F32).astype(BF16)


def _absorb(q, wk, row0, rows):
    rb = row0 // rows
    return pl.pallas_call(
        _absorb_kernel,
        grid=(N_HEADS,),
        in_specs=[pl.BlockSpec((rows, LANES), lambda h: (rb, 2 * h)),
                  pl.BlockSpec((KV_LORA, NOPE_DIM), lambda h: (0, h))],
        out_specs=pl.BlockSpec((1, rows, KV_LORA), lambda h: (h, 0, 0)),
        out_shape=jax.ShapeDtypeStruct((N_HEADS, rows, KV_LORA), BF16),
        compiler_params=_cparams(("arbitrary",)),
        name="absorb",
    )(q, wk)


def _attn_sample_kernel(qa_ref, q_ref, cc_ref, ckr_ref, cn_ref, krn_ref, o_ref, *, ts, past):
    hn = N_HEADS
    qa = qa_ref[...].reshape(hn * ts, KV_LORA)
    qfull = q_ref[...]
    qr = jnp.concatenate([qfull[:, h * HEAD_W + LANES:(h + 1) * HEAD_W] for h in range(hn)], axis=0)
    cc = cc_ref[0].astype(BF16)
    ckr = ckr_ref[0].astype(BF16)
    cn = cn_ref[...]
    krn = krn_ref[...]
    dn = (((1,), (1,)), ((), ()))
    s_c = (lax.dot_general(qa, cc, dn, preferred_element_type=F32)
           + lax.dot_general(qr, ckr, dn, preferred_element_type=F32))
    s_n = (lax.dot_general(qa, cn, dn, preferred_element_type=F32)
           + lax.dot_general(qr, krn, dn, preferred_element_type=F32))
    qchunk_c = (past + lax.broadcasted_iota(I32, s_c.shape, 0) % ts) // CHUNK
    s_c = jnp.where(lax.broadcasted_iota(I32, s_c.shape, 1) // CHUNK <= qchunk_c, s_c, -jnp.inf)
    qchunk_n = (past + lax.broadcasted_iota(I32, s_n.shape, 0) % ts) // CHUNK
    s_n = jnp.where((past + lax.broadcasted_iota(I32, s_n.shape, 1)) // CHUNK <= qchunk_n, s_n, -jnp.inf)
    m = jnp.maximum(jnp.max(s_c, axis=-1, keepdims=True), jnp.max(s_n, axis=-1, keepdims=True))
    p_c = jnp.exp(s_c - m)
    p_n = jnp.exp(s_n - m)
    l = jnp.sum(p_c, axis=-1, keepdims=True) + jnp.sum(p_n, axis=-1, keepdims=True)
    o = (jnp.dot(p_c.astype(BF16), cc, preferred_element_type=F32)
         + jnp.dot(p_n.astype(BF16), cn, preferred_element_type=F32)) / l
    o_ref[...] = o.astype(BF16).reshape(hn, ts, KV_LORA)


def _attn_sample(qa, q, cache_c, cache_kr, ckvb, krp, row0, bs, ts, past):
    rb0 = row0 // ts
    return pl.pallas_call(
        functools.partial(_attn_sample_kernel, ts=ts, past=past),
        grid=(bs,),
        in_specs=[pl.BlockSpec((N_HEADS, ts, KV_LORA), lambda b: (0, b, 0)),
                  pl.BlockSpec((ts, N_HEADS * HEAD_W), lambda b: (rb0 + b, 0)),
                  pl.BlockSpec((1, past, KV_LORA), lambda b: (b, 0, 0)),
                  pl.BlockSpec((1, past, LANES), lambda b: (b, 0, 0)),
                  pl.BlockSpec((ts, KV_LORA), lambda b: (rb0 + b, 0)),
                  pl.BlockSpec((ts, LANES), lambda b: (rb0 + b, 0))],
        out_specs=pl.BlockSpec((N_HEADS, ts, KV_LORA), lambda b: (0, b, 0)),
        out_shape=jax.ShapeDtypeStruct((N_HEADS, bs * ts, KV_LORA), BF16),
        compiler_params=_cparams(("arbitrary",)),
        name="attn_sample",
    )(qa, q, cache_c, cache_kr, ckvb, krp)


def _unabsorb_kernel(ol_ref, wv_ref, o_ref):
    o_ref[...] = jnp.dot(ol_ref[0], wv_ref[...], preferred_element_type=F32).astype(BF16)


def _unabsorb(o_lat, wv, rows):
    return pl.pallas_call(
        _unabsorb_kernel,
        grid=(N_HEADS,),
        in_specs=[pl.BlockSpec((1, rows, KV_LORA), lambda h: (h, 0, 0)),
                  pl.BlockSpec((KV_LORA, V_DIM), lambda h: (0, h))],
        out_specs=pl.BlockSpec((rows, V_DIM), lambda h: (0, h)),
        out_shape=jax.ShapeDtypeStruct((rows, N_HEADS * V_DIM), BF16),
        compiler_params=_cparams(("arbitrary",)),
        name="unabsorb",
    )(o_lat, wv)


def _post_mixer_kernel(op_ref, os_ref, wo_ref, x_ref, gate_ref, g2_ref, sh_ref, sc_ref, rw_ref, rb_ref,
                       x1_ref, hp_ref, te_ref, tg_ref, pos_ref, cnt_ref, carry, *, prompt_tiles):
    @pl.when(pl.program_id(0) == 0)
    def _():
        carry[...] = jnp.zeros(carry.shape, F32)

    o = jnp.where(pl.program_id(0) < prompt_tiles, op_ref[...], os_ref[...])
    y = jnp.dot(o, wo_ref[...], preferred_element_type=F32)
    x1 = _gated_residual(x_ref[...], gate_ref[...], y)
    x1_ref[...] = x1
    h2 = _norm_mod(x1, g2_ref[...], sh_ref[...], sc_ref[...])
    hp_ref[...] = h2
    rw = rw_ref[...]
    h_hi = h2.astype(BF16)
    h_lo = (h2 - h_hi.astype(F32)).astype(BF16)
    w_hi = rw.astype(BF16)
    w_lo = (rw - w_hi.astype(F32)).astype(BF16)
    logits = (jnp.dot(h_hi, w_hi, preferred_element_type=F32) + jnp.dot(h_lo, w_hi, preferred_element_type=F32)
              + jnp.dot(h_hi, w_lo, preferred_element_type=F32) + rb_ref[...])
    tm, ne = logits.shape
    eid = lax.broadcasted_iota(I32, (tm, ne), 1)
    lane = lax.broadcasted_iota(I32, (tm, LANES), 1)
    te = jnp.zeros((tm, LANES), I32)
    tv = jnp.full((tm, LANES), -jnp.inf, F32)
    work = logits
    picks = []
    for k in range(TOP_K):
        mx = jnp.max(work, axis=-1, keepdims=True)
        idx = jnp.min(jnp.where(work == mx, eid, ne), axis=-1, keepdims=True)
        picks.append(idx)
        te = jnp.where(lane == k, idx, te)
        tv = jnp.where(lane == k, mx, tv)
        work = jnp.where(eid == idx, -jnp.inf, work)
    ex = jnp.exp(tv - jnp.max(tv, axis=-1, keepdims=True))
    te_ref[...] = te
    tg_ref[...] = ex / jnp.sum(ex, axis=-1, keepdims=True)
    onehot = jnp.zeros((tm, LANES), F32)
    for idx in picks:
        onehot = onehot + (lane == idx).astype(F32)
    tri = (lax.broadcasted_iota(I32, (tm, tm), 1) < lax.broadcasted_iota(I32, (tm, tm), 0)).astype(BF16)
    rank = jnp.dot(tri, onehot.astype(BF16), preferred_element_type=F32) + carry[0:1, :]
    pos = jnp.zeros((tm, LANES), I32)
    for k, idx in enumerate(picks):
        pk = jnp.sum(jnp.where(lane == idx, rank, 0.0), axis=-1, keepdims=True)
        pos = jnp.where(lane == k, pk.astype(I32), pos)
    pos_ref[...] = pos
    total = carry[0:1, :] + jnp.sum(onehot, axis=0, keepdims=True)
    carry[...] = jnp.broadcast_to(total, carry.shape)
    cnt_ref[...] = carry[...]


def _post_mixer(o_p, o_s, wo, x, g2, modg, rw, rb, group):
    n, d = x.shape
    tm = ROW_TILE
    ng = tm // group
    npt = o_p.shape[0] // tm
    nst = o_s.shape[0] // tm
    row = lambda i: (i, 0)
    const = lambda i: (0, 0)
    return pl.pallas_call(
        functools.partial(_post_mixer_kernel, prompt_tiles=npt),
        grid=(n // tm,),
        in_specs=[pl.BlockSpec((tm, o_p.shape[1]), lambda i: (jnp.minimum(i, npt - 1), 0)),
                  pl.BlockSpec((tm, o_s.shape[1]), lambda i: (jnp.clip(i - npt, 0, nst - 1), 0)),
                  pl.BlockSpec(wo.shape, const),
                  pl.BlockSpec((tm, d), row),
                  pl.BlockSpec((ng, d), lambda i: (i, 2)),
                  pl.BlockSpec((1, d), const),
                  pl.BlockSpec((ng, d), lambda i: (i, 3)),
                  pl.BlockSpec((ng, d), lambda i: (i, 4)),
                  pl.BlockSpec(rw.shape, const),
                  pl.BlockSpec((1, rw.shape[1]), const)],
        out_specs=[pl.BlockSpec((tm, d), row),
                   pl.BlockSpec((tm, d), row),
                   pl.BlockSpec((tm, LANES), row),
                   pl.BlockSpec((tm, LANES), row),
                   pl.BlockSpec((tm, LANES), row),
                   pl.BlockSpec((SUBLANES, LANES), const)],
        out_shape=[jax.ShapeDtypeStruct((n, d), F32),
                   jax.ShapeDtypeStruct((n, d), F32),
                   jax.ShapeDtypeStruct((n, LANES), I32),
                   jax.ShapeDtypeStruct((n, LANES), F32),
                   jax.ShapeDtypeStruct((n, LANES), I32),
                   jax.ShapeDtypeStruct((SUBLANES, LANES), F32)],
        scratch_shapes=[pltpu.VMEM((SUBLANES, LANES), F32)],
        compiler_params=_cparams(("arbitrary",)),
        name="post_mixer",
    )(o_p, o_s, wo, x, modg, g2.reshape(1, d), modg, modg, rw, rb.reshape(1, -1))


DEST_GROUP = LANES // TOP_K


def _moe_dest_kernel(te_ref, pos_ref, cnt_ref, dest_ref, meta_ref, *, n_blocks):
    shift = MOE_TM.bit_length() - 1
    lane8 = lax.broadcasted_iota(I32, (SUBLANES, LANES), 1)
    cnt = cnt_ref[...].astype(I32)
    padded = ((cnt + (MOE_TM - 1)) >> shift) << shift
    ends = padded.astype(F32)
    s = 1
    while s < N_EXPERTS:
        ends = ends + jnp.where(lane8 >= s, pltpu.roll(ends, s, 1), 0.0)
        s *= 2
    ends_row = ends[0:1, :]
    starts_row = (ends - padded.astype(F32))[0:1, :]
    te = te_ref[...]
    pos = pos_ref[...]
    tm = te.shape[0]
    lane = lax.broadcasted_iota(I32, (tm, LANES), 1)
    dest = jnp.zeros((tm, LANES), F32)
    for k in range(TOP_K):
        sk = jnp.sum(jnp.where(lane == te[:, k:k + 1], starts_row, 0.0), axis=-1, keepdims=True)
        dest = jnp.where(lane == k, sk + pos[:, k:k + 1].astype(F32), dest)
    hi = jnp.floor(dest * (1.0 / 256.0))
    lo = dest - 256.0 * hi
    sel = (lax.broadcasted_iota(I32, (LANES, LANES), 0)
           == lax.broadcasted_iota(I32, (LANES, LANES), 1) % TOP_K).astype(BF16)
    spread = (256.0 * jnp.dot(hi.astype(BF16), sel, preferred_element_type=F32)
              + jnp.dot(lo.astype(BF16), sel, preferred_element_type=F32))
    row = lax.broadcasted_iota(I32, (tm, LANES), 0)
    keep = lane // TOP_K == row % DEST_GROUP
    dense = jnp.sum(jnp.where(keep, spread, 0.0).reshape(tm // DEST_GROUP, DEST_GROUP, LANES), axis=1)
    dest_ref[...] = dense.astype(I32)

    @pl.when(pl.program_id(0) == 0)
    def _():
        nl = meta_ref.shape[1]
        r_i = lax.broadcasted_iota(I32, (LANES, LANES), 0)
        l_i = lax.broadcasted_iota(I32, (LANES, LANES), 1)
        ends_col = jnp.sum(jnp.where(l_i == r_i, ends_row, 0.0), axis=-1, keepdims=True)
        e_i = lax.broadcasted_iota(I32, (LANES, nl), 0)
        b_i = lax.broadcasted_iota(I32, (LANES, nl), 1)
        closed = (e_i < N_EXPERTS) & (ends_col <= (b_i * MOE_TM).astype(F32))
        be = jnp.minimum(jnp.sum(jnp.where(closed, 1.0, 0.0), axis=0, keepdims=True), N_EXPERTS - 1.0)
        total = jnp.sum(jnp.where(lane8[0:1, :] == N_EXPERTS - 1, ends_row, 0.0), axis=-1, keepdims=True)
        n_used = (total.astype(I32) >> shift).astype(F32)
        meta = jnp.where(b_i[0:1, :] < n_blocks, be, n_used).astype(I32)
        meta_ref[...] = jnp.broadcast_to(meta, meta_ref.shape)


def _moe_dest(te128, pos128, cnt, n_blocks):
    n = te128.shape[0]
    tm = ROW_TILE
    nl = -(-(n_blocks + 1) // LANES) * LANES
    row = lambda i: (i, 0)
    const = lambda i: (0, 0)
    return pl.pallas_call(
        functools.partial(_moe_dest_kernel, n_blocks=n_blocks),
        grid=(n // tm,),
        in_specs=[pl.BlockSpec((tm, LANES), row),
                  pl.BlockSpec((tm, LANES), row),
                  pl.BlockSpec((SUBLANES, LANES), const)],
        out_specs=[pl.BlockSpec((tm // DEST_GROUP, LANES), row),
                   pl.BlockSpec((SUBLANES, nl), const)],
        out_shape=[jax.ShapeDtypeStruct((n // DEST_GROUP, LANES), I32),
                   jax.ShapeDtypeStruct((SUBLANES, nl), I32)],
        compiler_params=_cparams(("arbitrary",)),
        name="moe_dest",
    )(te128, pos128, cnt)


def _dispatch_kernel(rt_ref, nb_ref, h_ref, o_ref, buf, sem, *, tg):
    i = pl.program_id(0)
    n_used = (nb_ref[0] * MOE_TM + tg - 1) // tg
    slot = i % 2

    def issue(blk, s):
        def body(r, c):
            tok = rt_ref[blk * tg + r]
            pltpu.make_async_copy(h_ref.at[pl.ds(tok, 1), :], buf.at[s, pl.ds(r, 1), :], sem.at[s]).start()
            return c
        lax.fori_loop(0, tg, body, 0, unroll=4)

    @pl.when(i == 0)
    def _():
        issue(0, 0)

    @pl.when(i + 1 < n_used)
    def _():
        issue(i + 1, 1 - slot)

    @pl.when(i < n_used)
    def _():
        pltpu.make_async_copy(h_ref.at[pl.ds(0, tg), :], buf.at[slot], sem.at[slot]).wait()
        o_ref[...] = buf[slot].astype(BF16)

    @pl.when(i >= n_used)
    def _():
        o_ref[...] = jnp.zeros(o_ref.shape, o_ref.dtype)


def _dispatch(row_tok, n_blocks_used, h, rows_total):
    tg = 2 * MOE_TM
    d = h.shape[1]
    return pl.pallas_call(
        functools.partial(_dispatch_kernel, tg=tg),
        grid_spec=pltpu.PrefetchScalarGridSpec(
            num_scalar_prefetch=2,
            grid=(rows_total // tg,),
            in_specs=[pl.BlockSpec(memory_space=pl.ANY)],
            out_specs=pl.BlockSpec((tg, d), lambda i, rt, nb: (i, 0)),
            scratch_shapes=[pltpu.VMEM((2, tg, d), F32), pltpu.SemaphoreType.DMA((2,))]),
        out_shape=jax.ShapeDtypeStruct((rows_total, d), BF16),
        compiler_params=_cparams(("arbitrary",)),
        name="moe_dispatch",
    )(row_tok, n_blocks_used, h)


def _new_expert(be_ref, b):
    return (b == 0) | (be_ref[b] != be_ref[jnp.maximum(b - 1, 0)])


def _swiglu_pairs(v):
    g = jnp.minimum(v, SWIGLU_LIMIT)
    glu = g * jax.nn.sigmoid(g * SWIGLU_ALPHA)
    up1 = jnp.clip(v, -SWIGLU_LIMIT, SWIGLU_LIMIT) + 1.0
    return glu, up1


def _moe_gu_kernel(be_ref, nb_ref, xb_ref, wa_ref, wb_ref, ba_ref, bb_ref, o_ref, wa_s, wb_s):
    b = pl.program_id(1)
    active = b < nb_ref[0]

    @pl.when(active & _new_expert(be_ref, b))
    def _():
        wa_s[...] = wa_ref[0].astype(BF16)
        wb_s[...] = wb_ref[0].astype(BF16)

    @pl.when(active)
    def _():
        x = xb_ref[...]
        ga = jnp.dot(x, wa_s[...], preferred_element_type=F32) + ba_ref[0]
        gb = jnp.dot(x, wb_s[...], preferred_element_type=F32) + bb_ref[0]
        tm, tn = ga.shape
        even = lax.broadcasted_iota(I32, (tm, LANES), 1) % 2 == 0
        for c in range(tn // LANES):
            glu_a, up_a = _swiglu_pairs(ga[:, c * LANES:(c + 1) * LANES])
            glu_b, up_b = _swiglu_pairs(gb[:, c * LANES:(c + 1) * LANES])
            ra = glu_a * pltpu.roll(up_a, LANES - 1, 1)
            rb = pltpu.roll(glu_b, 1, 1) * up_b
            o_ref[:, c * LANES:(c + 1) * LANES] = jnp.where(even, ra, rb).astype(BF16)

    @pl.when(b >= nb_ref[0])
    def _():
        o_ref[...] = jnp.zeros(o_ref.shape, o_ref.dtype)


def _moe_gu(block_e, n_blocks_used, xb, wgu, bgu):
    rows, half = xb.shape
    ne, d, f2 = wgu.shape
    tm = MOE_TM
    tn = 1024
    nj = f2 // 2 // tn
    return pl.pallas_call(
        _moe_gu_kernel,
        grid_spec=pltpu.PrefetchScalarGridSpec(
            num_scalar_prefetch=2,
            grid=(nj, rows // tm),
            in_specs=[pl.BlockSpec((tm, half), lambda j, b, be, nb: (b, 0)),
                      pl.BlockSpec((1, d, tn), lambda j, b, be, nb: (be[b], 0, j)),
                      pl.BlockSpec((1, d, tn), lambda j, b, be, nb: (be[b], 0, j + nj)),
                      pl.BlockSpec((1, 1, tn), lambda j, b, be, nb: (be[b], 0, j)),
                      pl.BlockSpec((1, 1, tn), lambda j, b, be, nb: (be[b], 0, j + nj))],
            out_specs=pl.BlockSpec((tm, tn), lambda j, b, be, nb: (b, j)),
            scratch_shapes=[pltpu.VMEM((d, tn), BF16), pltpu.VMEM((d, tn), BF16)]),
        out_shape=jax.ShapeDtypeStruct((rows, f2 // 2), BF16),
        compiler_params=_cparams(("arbitrary", "arbitrary")),
        name="moe_gate_up",
    )(block_e, n_blocks_used, xb, wgu, wgu, bgu.reshape(ne, 1, f2), bgu.reshape(ne, 1, f2))


def _moe_down_kernel(be_ref, nb_ref, a_ref, wd_ref, bd_ref, o_ref, wp_s, stage):
    b = pl.program_id(1)
    active = b < nb_ref[0]

    @pl.when(active & _new_expert(be_ref, b))
    def _():
        f = wd_ref.shape[1]
        for c in range(wd_ref.shape[2] // LANES):
            cols = slice(c * LANES, (c + 1) * LANES)
            stage[c, pl.ds(0, f // 2, stride=2), :] = wd_ref[0, :f // 2, cols]
            stage[c, pl.ds(1, f // 2, stride=2), :] = wd_ref[0, f // 2:, cols]
            wp_s[:, cols] = stage[c].astype(BF16)

    @pl.when(active)
    def _():
        o_ref[...] = jnp.dot(a_ref[...], wp_s[...], preferred_element_type=F32) + bd_ref[0]

    @pl.when(b >= nb_ref[0])
    def _():
        o_ref[...] = jnp.zeros(o_ref.shape, o_ref.dtype)


def _moe_down(block_e, n_blocks_used, act, wd, bd):
    rows, f = act.shape
    ne, _, d = wd.shape
    tm = MOE_TM
    tn = 1024
    return pl.pallas_call(
        _moe_down_kernel,
        grid_spec=pltpu.PrefetchScalarGridSpec(
            num_scalar_prefetch=2,
            grid=(d // tn, rows // tm),
            in_specs=[pl.BlockSpec((tm, f), lambda j, b, be, nb: (b, 0)),
                      pl.BlockSpec((1, f, tn), lambda j, b, be, nb: (be[b], 0, j)),
                      pl.BlockSpec((1, 1, tn), lambda j, b, be, nb: (be[b], 0, j))],
            out_specs=pl.BlockSpec((tm, tn), lambda j, b, be, nb: (b, j)),
            scratch_shapes=[pltpu.VMEM((f, tn), BF16), pltpu.VMEM((tn // LANES, f, LANES), F32)]),
        out_shape=jax.ShapeDtypeStruct((rows, d), F32),
        compiler_params=_cparams(("arbitrary", "arbitrary")),
        name="moe_down",
    )(block_e, n_blocks_used, act, wd, bd.reshape(ne, 1, d))


def _combine_kernel(dest_ref, y_ref, x1_ref, gm_ref, tg_ref, fg_ref, o_ref, buf, sem, *, tn, tile0, final_norm):
    i = pl.program_id(0) + tile0

    def issue(t, c):
        for k in range(TOP_K):
            d = dest_ref[(i * tn + t) * TOP_K + k]
            pltpu.make_async_copy(y_ref.at[pl.ds(d, 1), :], buf.at[k, pl.ds(t, 1), :], sem).start()
        return c

    lax.fori_loop(0, tn, issue, 0, unroll=2)
    for k in range(TOP_K):
        pltpu.make_async_copy(y_ref.at[pl.ds(0, tn), :], buf.at[k], sem).wait()
    tg = tg_ref[...]
    moe = tg[:, 0:1] * buf[0]
    for k in range(1, TOP_K):
        moe = moe + tg[:, k:k + 1] * buf[k]
    x2 = _gated_residual(x1_ref[...], gm_ref[...], moe)
    if final_norm:
        x2 = _rms(x2, fg_ref[...])
    o_ref[...] = x2


def _combine(dest, y, x1, modg, tgates, fg, group, final_norm, row0, rows):
    d = x1.shape[1]
    tn = ROW_TILE
    ng = tn // group
    t0 = row0 // tn
    return pl.pallas_call(
        functools.partial(_combine_kernel, tn=tn, tile0=t0, final_norm=final_norm),
        grid_spec=pltpu.PrefetchScalarGridSpec(
            num_scalar_prefetch=1,
            grid=(rows // tn,),
            in_specs=[pl.BlockSpec(memory_space=pl.ANY),
                      pl.BlockSpec((tn, d), lambda i, ds: (i + t0, 0)),
                      pl.BlockSpec((ng, d), lambda i, ds: (i + t0, 5)),
                      pl.BlockSpec((tn, LANES), lambda i, ds: (i + t0, 0)),
                      pl.BlockSpec((1, d), lambda i, ds: (0, 0))],
            out_specs=pl.BlockSpec((tn, d), lambda i, ds: (i, 0)),
            scratch_shapes=[pltpu.VMEM((TOP_K, tn, d), F32), pltpu.SemaphoreType.DMA(())]),
        out_shape=jax.ShapeDtypeStruct((rows, d), F32),
        compiler_params=_cparams(("arbitrary",)),
        name="moe_combine",
    )(dest, y, x1, modg, tgates, fg.reshape(1, d))


def _moe(hp, te128, tg128, pos128, cnt, x1, modg, wgu, bgu, wd, bd, fg, group, splits):
    n = x1.shape[0]
    rows_total = n * TOP_K + N_EXPERTS * MOE_TM
    n_blocks = rows_total // MOE_TM
    dest2d, meta = _moe_dest(te128, pos128, cnt, n_blocks)
    dest = dest2d.reshape(-1)
    block_e = meta[0, :n_blocks]
    nbu = meta[0, n_blocks:n_blocks + 1]
    tok = jnp.arange(n * TOP_K, dtype=I32) // TOP_K
    row_tok = jnp.zeros((rows_total,), I32).at[dest].set(tok, unique_indices=True)
    xb = _dispatch(row_tok, nbu, hp, rows_total)
    act = _moe_gu(block_e, nbu, xb, wgu, bgu)
    y = _moe_down(block_e, nbu, act, wd, bd)
    return [_combine(dest, y, x1, modg, tg128, fg, group, fn, r0, rows) for r0, rows, fn in splits]


def _lru_in_kernel(x_ref, g1_ref, sh_ref, sc_ref, wy_ref, wx_ref, y_ref, xb_ref):
    h = _norm_mod(x_ref[...], g1_ref[...], sh_ref[...], sc_ref[...]).astype(BF16)
    y = jnp.dot(h, wy_ref[...], preferred_element_type=F32)
    y_ref[...] = jax.nn.gelu(y, approximate=True).astype(BF16)
    xb_ref[...] = jnp.dot(h, wx_ref[...], preferred_element_type=F32)


def _lru_in(x, g1, modg, wy, wx, group):
    n, d = x.shape
    dr = wy.shape[1]
    tm = ROW_TILE
    ng = tm // group
    row = lambda i: (i, 0)
    const = lambda i: (0, 0)
    return pl.pallas_call(
        _lru_in_kernel,
        grid=(n // tm,),
        in_specs=[pl.BlockSpec((tm, d), row),
                  pl.BlockSpec((1, d), const),
                  pl.BlockSpec((ng, d), lambda i: (i, 0)),
                  pl.BlockSpec((ng, d), lambda i: (i, 1)),
                  pl.BlockSpec((d, dr), const),
                  pl.BlockSpec((d, dr), const)],
        out_specs=[pl.BlockSpec((tm, dr), row), pl.BlockSpec((tm, dr), row)],
        out_shape=[jax.ShapeDtypeStruct((n, dr), BF16), jax.ShapeDtypeStruct((n, dr), F32)],
        compiler_params=_cparams(("arbitrary",)),
        name="lru_in",
    )(x, g1.reshape(1, d), modg, modg, wy, wx)


def _lru_scan_kernel(y_ref, xb_ref, cb_ref, h0_ref, cw_ref, cbias_ref, wa_ref, ba_ref, wx_ref, bx_ref, lam_ref,
                     hy_ref, cbo_ref, ho_ref, xe, a_s, u_s, hc, *, tc, starts_at_pos0):
    c = pl.program_id(1)
    dr = xb_ref.shape[1]
    nb = wa_ref.shape[0]
    bd = dr // nb
    pre = SUBLANES

    @pl.when(c == 0)
    def _():
        xe[0:pre, :] = jnp.zeros((pre, dr), F32)
        xe[pre - (CONV_W - 1):pre, :] = cb_ref[0]
        hc[...] = jnp.broadcast_to(h0_ref[0], (SUBLANES, dr))

    xe[pre:pre + tc, :] = xb_ref[...]
    cw = cw_ref[...]
    xc = cbias_ref[...] + xe[pre:pre + tc, :] * cw[CONV_W - 1:CONV_W, :]
    for k in range(1, CONV_W):
        xc = xc + xe[pre - k:pre - k + tc, :] * cw[CONV_W - 1 - k:CONV_W - k, :]
    cbo_ref[0] = xe[pre + tc - (CONV_W - 1):pre + tc, :]
    xe[0:pre, :] = xe[tc:tc + pre, :]

    xcb = xc.astype(BF16)
    ra = jnp.concatenate([jnp.dot(xcb[:, n * bd:(n + 1) * bd], wa_ref[n], preferred_element_type=F32)
                          for n in range(nb)], axis=1)
    rx = jnp.concatenate([jnp.dot(xcb[:, n * bd:(n + 1) * bd], wx_ref[n], preferred_element_type=F32)
                          for n in range(nb)], axis=1)
    r = jax.nn.sigmoid(ra + ba_ref[...])
    ig = jax.nn.sigmoid(rx + bx_ref[...])
    lam = lam_ref[...]
    log_sig = jnp.minimum(lam, 0.0) - jnp.log1p(jnp.exp(-jnp.abs(lam)))
    log_a = LRU_C * r * log_sig
    a = jnp.exp(log_a)
    th = jnp.tanh(log_a)
    mult = jnp.sqrt(-2.0 * th / (1.0 - th))
    if starts_at_pos0:
        first = (lax.broadcasted_iota(I32, (tc, 1), 0) == 0) & (c == 0)
        mult = jnp.where(first, 1.0, mult)
    a_s[...] = a
    u_s[...] = mult * ig * xc

    row8 = lax.broadcasted_iota(I32, (SUBLANES, dr), 0)

    def group_step(g, hprev):
        off = pl.multiple_of(g * SUBLANES, SUBLANES)
        aa = a_s[pl.ds(off, SUBLANES), :]
        uu = u_s[pl.ds(off, SUBLANES), :]
        s = 1
        while s < SUBLANES:
            m = row8 >= s
            uu = jnp.where(m, aa * pltpu.roll(uu, s, 0) + uu, uu)
            aa = jnp.where(m, aa * pltpu.roll(aa, s, 0), aa)
            s *= 2
        hh = aa * hprev + uu
        u_s[pl.ds(off, SUBLANES), :] = hh
        return jnp.broadcast_to(hh[SUBLANES - 1:SUBLANES, :], (SUBLANES, dr))

    hlast = lax.fori_loop(0, tc // SUBLANES, group_step, hc[...])
    hc[...] = hlast
    ho_ref[0] = hlast[0:1, :]
    hy_ref[...] = (u_s[...] * y_ref[...].astype(F32)).astype(BF16)


def _lru_scan(yb, xb, conv_buf, h0, cw, cbias, wa, ba, wx, bx, lam, row0, n_seq, t, tc, starts_at_pos0):
    dr = xb.shape[1]
    nc = t // tc
    rb0 = row0 // tc
    inmap = lambda s, c: (rb0 + s * nc + c, 0)
    outmap = lambda s, c: (s * nc + c, 0)
    const2 = lambda s, c: (0, 0)
    const3 = lambda s, c: (0, 0, 0)
    seq3 = lambda s, c: (s, 0, 0)
    return pl.pallas_call(
        functools.partial(_lru_scan_kernel, tc=tc, starts_at_pos0=starts_at_pos0),
        grid=(n_seq, nc),
        in_specs=[pl.BlockSpec((tc, dr), inmap),
                  pl.BlockSpec((tc, dr), inmap),
                  pl.BlockSpec((1, CONV_W - 1, dr), seq3),
                  pl.BlockSpec((1, 1, dr), seq3),
                  pl.BlockSpec((CONV_W, dr), const2),
                  pl.BlockSpec((1, dr), const2),
                  pl.BlockSpec(wa.shape, const3),
                  pl.BlockSpec((1, dr), const2),
                  pl.BlockSpec(wx.shape, const3),
                  pl.BlockSpec((1, dr), const2),
                  pl.BlockSpec((1, dr), const2)],
        out_specs=[pl.BlockSpec((tc, dr), outmap),
                   pl.BlockSpec((1, CONV_W - 1, dr), seq3),
                   pl.BlockSpec((1, 1, dr), seq3)],
        out_shape=[jax.ShapeDtypeStruct((n_seq * t, dr), BF16),
                   jax.ShapeDtypeStruct((n_seq, CONV_W - 1, dr), F32),
                   jax.ShapeDtypeStruct((n_seq, 1, dr), F32)],
        scratch_shapes=[pltpu.VMEM((SUBLANES + tc, dr), F32),
                        pltpu.VMEM((tc, dr), F32),
                        pltpu.VMEM((tc, dr), F32),
                        pltpu.VMEM((SUBLANES, dr), F32)],
        compiler_params=_cparams(("arbitrary", "arbitrary")),
        name="lru_scan",
    )(yb, xb, conv_buf, h0.reshape(n_seq, 1, dr), cw, cbias.reshape(1, dr), wa, ba.reshape(1, dr),
      wx, bx.reshape(1, dr), lam.reshape(1, dr))


def _rope_tables(pos):
    half = ROPE_DIM // 2
    inv = 1.0 / (ROPE_THETA ** (jnp.arange(0, ROPE_DIM, 2, dtype=F32) / ROPE_DIM))
    ang = pos.astype(F32)[:, None] * inv[None, :]
    cos, sin = jnp.cos(ang), jnp.sin(ang)
    z = jnp.zeros((pos.shape[0], LANES - ROPE_DIM), F32)
    return jnp.concatenate([cos, cos, z], axis=1), jnp.concatenate([-sin, sin, z], axis=1)


def _head_slab_weights(w_uq):
    ql = w_uq.shape[0]
    w = w_uq.reshape(ql, N_HEADS, NOPE_DIM + ROPE_DIM)
    z = jnp.zeros((ql, N_HEADS, HEAD_W - NOPE_DIM - ROPE_DIM), w.dtype)
    return jnp.concatenate([w, z], axis=2).reshape(ql, N_HEADS * HEAD_W).astype(BF16)


def kernel(x_prompt, x_sample, cache_ckv, cache_krope, state_conv, state_h, c_prompt, c_sample,
           mod_w, mod_b, norm1_g, norm2_g,
           mla_w_in, mla_q_norm_g, mla_kv_norm_g, mla_w_uq, mla_w_ukv, mla_w_o,
           lru_w_in, lru_conv_w, lru_conv_b, lru_w_a, lru_b_a, lru_w_x, lru_b_x, lru_lambda, lru_w_o,
           router_w, router_b, moe_w_gu, moe_b_gu, moe_w_down, moe_b_down, final_g):
    bp, tp, d = x_prompt.shape
    bs, ts, _ = x_sample.shape
    past = cache_ckv.shape[2]
    depth = mod_w.shape[0]
    n_p, n_s = bp * tp, bs * ts
    n = n_p + n_s
    group = math.gcd(tp, ts)
    assert group % SUBLANES == 0 and ROW_TILE % group == 0 and n_p % ROW_TILE == 0 and n_s % ROW_TILE == 0
    assert NOPE_DIM == LANES and V_DIM == LANES and ROPE_DIM <= LANES and n_p % n_s == 0

    x = jnp.concatenate([x_prompt.reshape(n_p, d), x_sample.reshape(n_s, d)], axis=0)
    nb = bp + bs
    nb_pad = -(-nb // SUBLANES) * SUBLANES
    c_all = jnp.concatenate([c_prompt, c_sample, jnp.zeros((nb_pad - nb, d), F32)], axis=0)
    grp_batch = np.concatenate([np.repeat(np.arange(bp), tp // group), bp + np.repeat(np.arange(bs), ts // group)])
    pos = jnp.concatenate([jnp.tile(jnp.arange(tp), bp), jnp.tile(past + jnp.arange(ts), bs)])
    cos, sin = _rope_tables(pos)

    outs = {k: [] for k in ("ckv_p", "kr_p", "conv_p", "h_p", "ckv_s", "kr_s", "conv_s", "h_s")}
    for i in range(depth):
        mod = _adaln(c_all, mod_w[i], mod_b[i])
        modg = jnp.take(mod, jnp.asarray(grp_batch), axis=0)
        j = i // 2
        if i % 2 == 0:
            w_in = mla_w_in[j]
            zpad = jnp.zeros((d, LANES - ROPE_DIM), F32)
            win = jnp.concatenate([w_in, zpad], axis=1).astype(BF16)
            wuq = _head_slab_weights(mla_w_uq[j])
            wukv = mla_w_ukv[j].reshape(KV_LORA, N_HEADS, NOPE_DIM + V_DIM)
            wk = wukv[:, :, :NOPE_DIM].reshape(KV_LORA, N_HEADS * NOPE_DIM).astype(BF16)
            wv = wukv[:, :, NOPE_DIM:].reshape(KV_LORA, N_HEADS * V_DIM).astype(BF16)
            q, ckv, kr, ckvb, krp = _mla_proj(x, norm1_g[i], modg, win, mla_q_norm_g[j], mla_kv_norm_g[j],
                                              wuq, cos, sin, group)
            kk, vv = _kv_expand(ckvb, krp, wk, wv, n_p)
            o_p = _attn_prompt(q, kk, vv, bp, tp)
            qa = _absorb(q, wk, n_p, n_s)
            ckr_pad = jnp.pad(cache_krope[j], ((0, 0), (0, 0), (0, LANES - ROPE_DIM)))
            o_lat = _attn_sample(qa, q, cache_ckv[j], ckr_pad, ckvb, krp, n_p, bs, ts, past)
            o_s = _unabsorb(o_lat, wv, n_s)
            wo = mla_w_o[j].astype(BF16)
            outs["ckv_p"].append(ckv[:n_p].reshape(bp, tp, KV_LORA))
            outs["kr_p"].append(kr[:n_p].reshape(bp, tp, ROPE_DIM))
            outs["ckv_s"].append(ckv[n_p:].reshape(bs, ts, KV_LORA))
            outs["kr_s"].append(kr[n_p:].reshape(bs, ts, ROPE_DIM))
        else:
            dr = lru_w_in.shape[2] // 2
            wy = lru_w_in[j][:, :dr].astype(BF16)
            wx = lru_w_in[j][:, dr:].astype(BF16)
            yb, xb = _lru_in(x, norm1_g[i], modg, wy, wx, group)
            wa = lru_w_a[j].astype(BF16)
            wxg = lru_w_x[j].astype(BF16)
            lru_args = (lru_conv_w[j], lru_conv_b[j], wa, lru_b_a[j], wxg, lru_b_x[j], lru_lambda[j])
            zbuf = jnp.zeros((bp, CONV_W - 1, dr), F32)
            zh = jnp.zeros((bp, dr), F32)
            o_p, cb_p, h_p = _lru_scan(yb, xb, zbuf, zh, *lru_args, 0, bp, tp, ROW_TILE, True)
            o_s, cb_s, h_s = _lru_scan(yb, xb, state_conv[j], state_h[j], *lru_args, n_p, bs, ts, ts, False)
            wo = lru_w_o[j].astype(BF16)
            outs["conv_p"].append(cb_p)
            outs["h_p"].append(h_p.reshape(bp, dr))
            outs["conv_s"].append(cb_s)
            outs["h_s"].append(h_s.reshape(bs, dr))
        x1, hp, te128, tg128, pos128, cnt = _post_mixer(o_p, o_s, wo, x, norm2_g[i], modg, router_w[i], router_b[i],
                                                        group)
        last = i == depth - 1
        splits = [(0, n_p, True), (n_p, n_s, True)] if last else [(0, n, False)]
        res = _moe(hp, te128, tg128, pos128, cnt, x1, modg, moe_w_gu[i], moe_b_gu[i], moe_w_down[i], moe_b_down[i],
                   final_g, group, splits)
        x = res[0]
    y_prompt = res[0].reshape(bp, tp, d)
    y_sample = res[1].reshape(bs, ts, d)
    return (y_prompt, y_sample,
            jnp.stack(outs["ckv_p"]), jnp.stack(outs["kr_p"]), jnp.stack(outs["conv_p"]), jnp.stack(outs["h_p"]),
            jnp.stack(outs["ckv_s"]), jnp.stack(outs["kr_s"]), jnp.stack(outs["conv_s"]), jnp.stack(outs["h_s"]))
```

```python
import functools
import math

import jax
import jax.numpy as jnp
import numpy as np
from jax import lax
from jax.experimental import pallas as pl
from jax.experimental.pallas import tpu as pltpu

F32 = jnp.float32
BF16 = jnp.bfloat16
I32 = jnp.int32

CHUNK = 64
N_HEADS = 16
Q_LORA = 512
KV_LORA = 512
NOPE_DIM = 128
ROPE_DIM = 64
V_DIM = 128
ROPE_THETA = 10000.0
LRU_BLOCKS = 8
CONV_W = 4
LRU_C = 8.0
N_EXPERTS = 32
TOP_K = 4
SWIGLU_LIMIT = 7.0
SWIGLU_ALPHA = 1.702
N_MOD = 6
EPS = 1e-6

LANES = 128
SUBLANES = 8
HEAD_W = 2 * LANES

ROW_TILE = 256
MOE_TM = 256
VMEM_LIMIT = 56 * 1024 * 1024


def _cparams(sem):
    return pltpu.CompilerParams(dimension_semantics=sem, vmem_limit_bytes=VMEM_LIMIT)


def _rms(x, g):
    ms = jnp.mean(x * x, axis=-1, keepdims=True)
    return x * lax.rsqrt(ms + EPS) * g


def _norm_mod(x, g, shift, scale):
    tm, d = x.shape
    ng = shift.shape[0]
    y = _rms(x, g).reshape(ng, tm // ng, d)
    return (y * (1.0 + scale[:, None, :]) + shift[:, None, :]).reshape(tm, d)


def _gated_residual(x, gate, y):
    tm, d = x.shape
    ng = gate.shape[0]
    return (x.reshape(ng, tm // ng, d) + gate[:, None, :] * y.reshape(ng, tm // ng, d)).reshape(tm, d)


def _adaln_kernel(c_ref, w_ref, b_ref, o_ref):
    c = c_ref[...]
    a = (c * jax.nn.sigmoid(c)).astype(BF16)
    o_ref[...] = jnp.dot(a, w_ref[0].astype(BF16), preferred_element_type=F32) + b_ref[...]


def _adaln(c_all, w_all, layer, b):
    bp, d = c_all.shape
    n = w_all.shape[2]
    tn = 1024
    return pl.pallas_call(
        _adaln_kernel,
        grid=(n // tn,),
        in_specs=[pl.BlockSpec((bp, d), lambda j: (0, 0)),
                  pl.BlockSpec((1, d, tn), lambda j: (layer, 0, j)),
                  pl.BlockSpec((1, tn), lambda j: (0, j))],
        out_specs=pl.BlockSpec((bp, tn), lambda j: (0, j)),
        out_shape=jax.ShapeDtypeStruct((bp, n), F32),
        compiler_params=_cparams(("arbitrary",)),
        name="adaln",
    )(c_all, w_all, b.reshape(1, n))


def _rope128(v, cos, sin):
    half = ROPE_DIM // 2
    lane = lax.broadcasted_iota(I32, v.shape, 1)
    sw = jnp.where(lane < half, pltpu.roll(v, LANES - half, 1), pltpu.roll(v, half, 1))
    return v * cos + sw * sin


def _mla_proj_kernel(x_ref, g1_ref, sh_ref, sc_ref, win_ref, qg_ref, kvg_ref, wuq_ref, cos_ref, sin_ref,
                     q_ref, ckv_ref, kr_ref, ckvb_ref, krp_ref):
    h = _norm_mod(x_ref[...], g1_ref[...], sh_ref[...], sc_ref[...]).astype(BF16)
    lat = jnp.dot(h, win_ref[...], preferred_element_type=F32)
    q_lat = lat[:, :Q_LORA]
    c_kv = lat[:, Q_LORA:Q_LORA + KV_LORA]
    k_r = lat[:, Q_LORA + KV_LORA:]
    qn = _rms(q_lat, qg_ref[...]).astype(BF16)
    q = jnp.dot(qn, wuq_ref[...], preferred_element_type=F32)
    cos = cos_ref[...]
    sin = sin_ref[...]
    scale = (NOPE_DIM + ROPE_DIM) ** -0.5
    for hh in range(N_HEADS):
        lo = hh * HEAD_W
        q_ref[:, lo:lo + LANES] = (q[:, lo:lo + LANES] * scale).astype(BF16)
        q_ref[:, lo + LANES:lo + HEAD_W] = (_rope128(q[:, lo + LANES:lo + HEAD_W], cos, sin) * scale).astype(BF16)
    ckv = _rms(c_kv, kvg_ref[...])
    ckv_ref[...] = ckv
    ckvb_ref[...] = ckv.astype(BF16)
    kr = _rope128(k_r, cos, sin)
    kr_ref[...] = kr[:, :ROPE_DIM]
    krp_ref[...] = kr.astype(BF16)


def _mla_proj(x, g1, modg, win, qg, kvg, wuq, cos, sin, group):
    n, d = x.shape
    tm = ROW_TILE
    ng = tm // group
    wl = win.shape[1]
    qw = wuq.shape[1]
    row = lambda i: (i, 0)
    const = lambda i: (0, 0)
    return pl.pallas_call(
        _mla_proj_kernel,
        grid=(n // tm,),
        in_specs=[pl.BlockSpec((tm, d), row),
                  pl.BlockSpec((1, d), const),
                  pl.BlockSpec((ng, d), lambda i: (i, 0)),
                  pl.BlockSpec((ng, d), lambda i: (i, 1)),
                  pl.BlockSpec((d, wl), const),
                  pl.BlockSpec((1, Q_LORA), const),
                  pl.BlockSpec((1, KV_LORA), const),
                  pl.BlockSpec((Q_LORA, qw), const),
                  pl.BlockSpec((tm, LANES), row),
                  pl.BlockSpec((tm, LANES), row)],
        out_specs=[pl.BlockSpec((tm, qw), row),
                   pl.BlockSpec((tm, KV_LORA), row),
                   pl.BlockSpec((tm, ROPE_DIM), row),
                   pl.BlockSpec((tm, KV_LORA), row),
                   pl.BlockSpec((tm, LANES), row)],
        out_shape=[jax.ShapeDtypeStruct((n, qw), BF16),
                   jax.ShapeDtypeStruct((n, KV_LORA), F32),
                   jax.ShapeDtypeStruct((n, ROPE_DIM), F32),
                   jax.ShapeDtypeStruct((n, KV_LORA), BF16),
                   jax.ShapeDtypeStruct((n, LANES), BF16)],
        compiler_params=_cparams(("arbitrary",)),
        name="mla_proj",
    )(x, g1.reshape(1, d), modg, modg, win, qg.reshape(1, -1), kvg.reshape(1, -1), wuq, cos, sin)


def _kv_expand_kernel(c_ref, krp_ref, wk_ref, wv_ref, k_ref, v_ref):
    c = c_ref[...]
    kn = jnp.dot(c, wk_ref[...], preferred_element_type=F32).astype(BF16)
    krp = krp_ref[...]
    for hh in range(N_HEADS):
        k_ref[:, hh * HEAD_W:hh * HEAD_W + LANES] = kn[:, hh * NOPE_DIM:(hh + 1) * NOPE_DIM]
        k_ref[:, hh * HEAD_W + LANES:(hh + 1) * HEAD_W] = krp
    v_ref[...] = jnp.dot(c, wv_ref[...], preferred_element_type=F32).astype(BF16)


def _kv_expand(ckvb, krp, wk, wv, rows):
    tm = 512
    row = lambda i: (i, 0)
    const = lambda i: (0, 0)
    return pl.pallas_call(
        _kv_expand_kernel,
        grid=(rows // tm,),
        in_specs=[pl.BlockSpec((tm, KV_LORA), row),
                  pl.BlockSpec((tm, LANES), row),
                  pl.BlockSpec(wk.shape, const),
                  pl.BlockSpec(wv.shape, const)],
        out_specs=[pl.BlockSpec((tm, N_HEADS * HEAD_W), row),
                   pl.BlockSpec((tm, N_HEADS * V_DIM), row)],
        out_shape=[jax.ShapeDtypeStruct((rows, N_HEADS * HEAD_W), BF16),
                   jax.ShapeDtypeStruct((rows, N_HEADS * V_DIM), BF16)],
        compiler_params=_cparams(("arbitrary",)),
        name="kv_expand",
    )(ckvb, krp, wk, wv)


ATTN_HEADS_PER_STEP = 2


def _attn_prompt_kernel(q_ref, k_ref, v_ref, o_ref, *, tq, nq):
    qi = pl.program_id(2)
    dn = (((1,), (1,)), ((), ()))
    hs = ATTN_HEADS_PER_STEP
    r = lax.broadcasted_iota(I32, (tq, tq), 0) // CHUNK
    c = lax.broadcasted_iota(I32, (tq, tq), 1) // CHUNK
    diag_visible = c <= r
    for qs in range(nq):
        @pl.when(qi == qs)
        def _(qs=qs):
            past = qs * tq
            for h in range(hs):
                q = q_ref[:, h * HEAD_W:(h + 1) * HEAD_W]
                kcols = slice(h * HEAD_W, (h + 1) * HEAD_W)
                vcols = slice(h * V_DIM, (h + 1) * V_DIM)
                s_d = lax.dot_general(q, k_ref[past:past + tq, kcols], dn, preferred_element_type=F32)
                s_d = jnp.where(diag_visible, s_d, -jnp.inf)
                m = jnp.max(s_d, axis=-1, keepdims=True)
                if past:
                    s_f = lax.dot_general(q, k_ref[0:past, kcols], dn, preferred_element_type=F32)
                    m = jnp.maximum(m, jnp.max(s_f, axis=-1, keepdims=True))
                p_d = jnp.exp(s_d - m)
                l = jnp.sum(p_d, axis=-1, keepdims=True)
                acc = jnp.dot(p_d.astype(BF16), v_ref[past:past + tq, vcols], preferred_element_type=F32)
                if past:
                    p_f = jnp.exp(s_f - m)
                    l = l + jnp.sum(p_f, axis=-1, keepdims=True)
                    acc = acc + jnp.dot(p_f.astype(BF16), v_ref[0:past, vcols], preferred_element_type=F32)
                o_ref[:, vcols] = (acc / l).astype(BF16)


def _attn_prompt(q, k, v, bp, tp):
    n_rows = bp * tp
    tq = 256
    nq = tp // tq
    hs = ATTN_HEADS_PER_STEP
    return pl.pallas_call(
        functools.partial(_attn_prompt_kernel, tq=tq, nq=nq),
        grid=(bp, N_HEADS // hs, nq),
        in_specs=[pl.BlockSpec((tq, hs * HEAD_W), lambda b, h, i: (b * nq + i, h)),
                  pl.BlockSpec((tp, hs * HEAD_W), lambda b, h, i: (b, h)),
                  pl.BlockSpec((tp, hs * V_DIM), lambda b, h, i: (b, h))],
        out_specs=pl.BlockSpec((tq, hs * V_DIM), lambda b, h, i: (b * nq + i, h)),
        out_shape=jax.ShapeDtypeStruct((n_rows, N_HEADS * V_DIM), BF16),
        compiler_params=_cparams(("arbitrary", "arbitrary", "arbitrary")),
        name="attn_prompt",
    )(q, k, v)


def _absorb_kernel(q_ref, wk_ref, o_ref):
    dn = (((1,), (1,)), ((), ()))
    o_ref[0] = lax.dot_general(q_ref[...], wk_ref[...], dn, preferred_element_type=F32).astype(BF16)


def _absorb(q, wk, row0, rows):
    rb = row0 // rows
    return pl.pallas_call(
        _absorb_kernel,
        grid=(N_HEADS,),
        in_specs=[pl.BlockSpec((rows, LANES), lambda h: (rb, 2 * h)),
                  pl.BlockSpec((KV_LORA, NOPE_DIM), lambda h: (0, h))],
        out_specs=pl.BlockSpec((1, rows, KV_LORA), lambda h: (h, 0, 0)),
        out_shape=jax.ShapeDtypeStruct((N_HEADS, rows, KV_LORA), BF16),
        compiler_params=_cparams(("arbitrary",)),
        name="absorb",
    )(q, wk)


def _attn_sample_kernel(qa_ref, q_ref, cc_ref, ckr_ref, cn_ref, krn_ref, o_ref, *, ts, past):
    hn = N_HEADS
    qa = qa_ref[...].reshape(hn * ts, KV_LORA)
    qfull = q_ref[...]
    qr = jnp.concatenate([qfull[:, h * HEAD_W + LANES:(h + 1) * HEAD_W] for h in range(hn)], axis=0)
    cc = cc_ref[0].astype(BF16)
    ckr = ckr_ref[0].astype(BF16)
    cn = cn_ref[...]
    krn = krn_ref[...]
    dn = (((1,), (1,)), ((), ()))
    s_c = (lax.dot_general(qa, cc, dn, preferred_element_type=F32)
           + lax.dot_general(qr, ckr, dn, preferred_element_type=F32))
    s_n = (lax.dot_general(qa, cn, dn, preferred_element_type=F32)
           + lax.dot_general(qr, krn, dn, preferred_element_type=F32))
    qchunk_c = (past + lax.broadcasted_iota(I32, s_c.shape, 0) % ts) // CHUNK
    s_c = jnp.where(lax.broadcasted_iota(I32, s_c.shape, 1) // CHUNK <= qchunk_c, s_c, -jnp.inf)
    qchunk_n = (past + lax.broadcasted_iota(I32, s_n.shape, 0) % ts) // CHUNK
    s_n = jnp.where((past + lax.broadcasted_iota(I32, s_n.shape, 1)) // CHUNK <= qchunk_n, s_n, -jnp.inf)
    m = jnp.maximum(jnp.max(s_c, axis=-1, keepdims=True), jnp.max(s_n, axis=-1, keepdims=True))
    p_c = jnp.exp(s_c - m)
    p_n = jnp.exp(s_n - m)
    l = jnp.sum(p_c, axis=-1, keepdims=True) + jnp.sum(p_n, axis=-1, keepdims=True)
    o = (jnp.dot(p_c.astype(BF16), cc, preferred_element_type=F32)
         + jnp.dot(p_n.astype(BF16), cn, preferred_element_type=F32)) / l
    o_ref[...] = o.astype(BF16).reshape(hn, ts, KV_LORA)


def _attn_sample(qa, q, cache_c, cache_kr, ckvb, krp, row0, bs, ts, past):
    rb0 = row0 // ts
    return pl.pallas_call(
        functools.partial(_attn_sample_kernel, ts=ts, past=past),
        grid=(bs,),
        in_specs=[pl.BlockSpec((N_HEADS, ts, KV_LORA), lambda b: (0, b, 0)),
                  pl.BlockSpec((ts, N_HEADS * HEAD_W), lambda b: (rb0 + b, 0)),
                  pl.BlockSpec((1, past, KV_LORA), lambda b: (b, 0, 0)),
                  pl.BlockSpec((1, past, LANES), lambda b: (b, 0, 0)),
                  pl.BlockSpec((ts, KV_LORA), lambda b: (rb0 + b, 0)),
                  pl.BlockSpec((ts, LANES), lambda b: (rb0 + b, 0))],
        out_specs=pl.BlockSpec((N_HEADS, ts, KV_LORA), lambda b: (0, b, 0)),
        out_shape=jax.ShapeDtypeStruct((N_HEADS, bs * ts, KV_LORA), BF16),
        compiler_params=_cparams(("arbitrary",)),
        name="attn_sample",
    )(qa, q, cache_c, cache_kr, ckvb, krp)


def _unabsorb_kernel(ol_ref, wv_ref, o_ref):
    o_ref[...] = jnp.dot(ol_ref[0], wv_ref[...], preferred_element_type=F32).astype(BF16)


def _unabsorb(o_lat, wv, rows):
    return pl.pallas_call(
        _unabsorb_kernel,
        grid=(N_HEADS,),
        in_specs=[pl.BlockSpec((1, rows, KV_LORA), lambda h: (h, 0, 0)),
                  pl.BlockSpec((KV_LORA, V_DIM), lambda h: (0, h))],
        out_specs=pl.BlockSpec((rows, V_DIM), lambda h: (0, h)),
        out_shape=jax.ShapeDtypeStruct((rows, N_HEADS * V_DIM), BF16),
        compiler_params=_cparams(("arbitrary",)),
        name="unabsorb",
    )(o_lat, wv)


def _post_mixer_kernel(op_ref, os_ref, wo_ref, x_ref, gate_ref, g2_ref, sh_ref, sc_ref, rw_ref, rb_ref,
                       x1_ref, hp_ref, te_ref, tg_ref, pos_ref, cnt_ref, carry, *, prompt_tiles):
    @pl.when(pl.program_id(0) == 0)
    def _():
        carry[...] = jnp.zeros(carry.shape, F32)

    o = jnp.where(pl.program_id(0) < prompt_tiles, op_ref[...], os_ref[...])
    y = jnp.dot(o, wo_ref[...], preferred_element_type=F32)
    x1 = _gated_residual(x_ref[...], gate_ref[...], y)
    x1_ref[...] = x1
    h2 = _norm_mod(x1, g2_ref[...], sh_ref[...], sc_ref[...])
    hp_ref[...] = h2
    rw = rw_ref[...]
    h_hi = h2.astype(BF16)
    h_lo = (h2 - h_hi.astype(F32)).astype(BF16)
    w_hi = rw.astype(BF16)
    w_lo = (rw - w_hi.astype(F32)).astype(BF16)
    logits = (jnp.dot(h_hi, w_hi, preferred_element_type=F32) + jnp.dot(h_lo, w_hi, preferred_element_type=F32)
              + jnp.dot(h_hi, w_lo, preferred_element_type=F32) + rb_ref[...])
    tm, ne = logits.shape
    eid = lax.broadcasted_iota(I32, (tm, ne), 1)
    lane = lax.broadcasted_iota(I32, (tm, LANES), 1)
    te = jnp.zeros((tm, LANES), I32)
    tv = jnp.full((tm, LANES), -jnp.inf, F32)
    work = logits
    picks = []
    for k in range(TOP_K):
        mx = jnp.max(work, axis=-1, keepdims=True)
        idx = jnp.min(jnp.where(work == mx, eid, ne), axis=-1, keepdims=True)
        picks.append(idx)
        te = jnp.where(lane == k, idx, te)
        tv = jnp.where(lane == k, mx, tv)
        work = jnp.where(eid == idx, -jnp.inf, work)
    ex = jnp.exp(tv - jnp.max(tv, axis=-1, keepdims=True))
    te_ref[...] = te
    tg_ref[...] = ex / jnp.sum(ex, axis=-1, keepdims=True)
    onehot = jnp.zeros((tm, LANES), F32)
    for idx in picks:
        onehot = onehot + (lane == idx).astype(F32)
    tri = (lax.broadcasted_iota(I32, (tm, tm), 1) < lax.broadcasted_iota(I32, (tm, tm), 0)).astype(BF16)
    rank = jnp.dot(tri, onehot.astype(BF16), preferred_element_type=F32) + carry[0:1, :]
    pos = jnp.zeros((tm, LANES), I32)
    for k, idx in enumerate(picks):
        pk = jnp.sum(jnp.where(lane == idx, rank, 0.0), axis=-1, keepdims=True)
        pos = jnp.where(lane == k, pk.astype(I32), pos)
    pos_ref[...] = pos
    total = carry[0:1, :] + jnp.sum(onehot, axis=0, keepdims=True)
    carry[...] = jnp.broadcast_to(total, carry.shape)
    cnt_ref[...] = carry[...]


def _post_mixer(o_p, o_s, wo, x, g2, modg, rw, rb, group):
    n, d = x.shape
    tm = ROW_TILE
    ng = tm // group
    npt = o_p.shape[0] // tm
    nst = o_s.shape[0] // tm
    row = lambda i: (i, 0)
    const = lambda i: (0, 0)
    return pl.pallas_call(
        functools.partial(_post_mixer_kernel, prompt_tiles=npt),
        grid=(n // tm,),
        in_specs=[pl.BlockSpec((tm, o_p.shape[1]), lambda i: (jnp.minimum(i, npt - 1), 0)),
                  pl.BlockSpec((tm, o_s.shape[1]), lambda i: (jnp.clip(i - npt, 0, nst - 1), 0)),
                  pl.BlockSpec(wo.shape, const),
                  pl.BlockSpec((tm, d), row),
                  pl.BlockSpec((ng, d), lambda i: (i, 2)),
                  pl.BlockSpec((1, d), const),
                  pl.BlockSpec((ng, d), lambda i: (i, 3)),
                  pl.BlockSpec((ng, d), lambda i: (i, 4)),
                  pl.BlockSpec(rw.shape, const),
                  pl.BlockSpec((1, rw.shape[1]), const)],
        out_specs=[pl.BlockSpec((tm, d), row),
                   pl.BlockSpec((tm, d), row),
                   pl.BlockSpec((tm, LANES), row),
                   pl.BlockSpec((tm, LANES), row),
                   pl.BlockSpec((tm, LANES), row),
                   pl.BlockSpec((SUBLANES, LANES), const)],
        out_shape=[jax.ShapeDtypeStruct((n, d), F32),
                   jax.ShapeDtypeStruct((n, d), F32),
                   jax.ShapeDtypeStruct((n, LANES), I32),
                   jax.ShapeDtypeStruct((n, LANES), F32),
                   jax.ShapeDtypeStruct((n, LANES), I32),
                   jax.ShapeDtypeStruct((SUBLANES, LANES), F32)],
        scratch_shapes=[pltpu.VMEM((SUBLANES, LANES), F32)],
        compiler_params=_cparams(("arbitrary",)),
        name="post_mixer",
    )(o_p, o_s, wo, x, modg, g2.reshape(1, d), modg, modg, rw, rb.reshape(1, -1))


DEST_GROUP = LANES // TOP_K


def _moe_dest_kernel(te_ref, pos_ref, cnt_ref, dest_ref, meta_ref, *, n_blocks):
    shift = MOE_TM.bit_length() - 1
    lane8 = lax.broadcasted_iota(I32, (SUBLANES, LANES), 1)
    cnt = cnt_ref[...].astype(I32)
    padded = ((cnt + (MOE_TM - 1)) >> shift) << shift
    ends = padded.astype(F32)
    s = 1
    while s < N_EXPERTS:
        ends = ends + jnp.where(lane8 >= s, pltpu.roll(ends, s, 1), 0.0)
        s *= 2
    ends_row = ends[0:1, :]
    starts_row = (ends - padded.astype(F32))[0:1, :]
    te = te_ref[...]
    pos = pos_ref[...]
    tm = te.shape[0]
    lane = lax.broadcasted_iota(I32, (tm, LANES), 1)
    dest = jnp.zeros((tm, LANES), F32)
    for k in range(TOP_K):
        sk = jnp.sum(jnp.where(lane == te[:, k:k + 1], starts_row, 0.0), axis=-1, keepdims=True)
        dest = jnp.where(lane == k, sk + pos[:, k:k + 1].astype(F32), dest)
    hi = jnp.floor(dest * (1.0 / 256.0))
    lo = dest - 256.0 * hi
    sel = (lax.broadcasted_iota(I32, (LANES, LANES), 0)
           == lax.broadcasted_iota(I32, (LANES, LANES), 1) % TOP_K).astype(BF16)
    spread = (256.0 * jnp.dot(hi.astype(BF16), sel, preferred_element_type=F32)
              + jnp.dot(lo.astype(BF16), sel, preferred_element_type=F32))
    row = lax.broadcasted_iota(I32, (tm, LANES), 0)
    keep = lane // TOP_K == row % DEST_GROUP
    dense = jnp.sum(jnp.where(keep, spread, 0.0).reshape(tm // DEST_GROUP, DEST_GROUP, LANES), axis=1)
    dest_ref[...] = dense.astype(I32)

    @pl.when(pl.program_id(0) == 0)
    def _():
        nl = meta_ref.shape[1]
        r_i = lax.broadcasted_iota(I32, (LANES, LANES), 0)
        l_i = lax.broadcasted_iota(I32, (LANES, LANES), 1)
        ends_col = jnp.sum(jnp.where(l_i == r_i, ends_row, 0.0), axis=-1, keepdims=True)
        e_i = lax.broadcasted_iota(I32, (LANES, nl), 0)
        b_i = lax.broadcasted_iota(I32, (LANES, nl), 1)
        closed = (e_i < N_EXPERTS) & (ends_col <= (b_i * MOE_TM).astype(F32))
        be = jnp.minimum(jnp.sum(jnp.where(closed, 1.0, 0.0), axis=0, keepdims=True), N_EXPERTS - 1.0)
        total = jnp.sum(jnp.where(lane8[0:1, :] == N_EXPERTS - 1, ends_row, 0.0), axis=-1, keepdims=True)
        n_used = (total.astype(I32) >> shift).astype(F32)
        meta = jnp.where(b_i[0:1, :] < n_blocks, be, n_used).astype(I32)
        meta_ref[...] = jnp.broadcast_to(meta, meta_ref.shape)


def _moe_dest(te128, pos128, cnt, n_blocks):
    n = te128.shape[0]
    tm = ROW_TILE
    nl = -(-(n_blocks + 1) // LANES) * LANES
    row = lambda i: (i, 0)
    const = lambda i: (0, 0)
    return pl.pallas_call(
        functools.partial(_moe_dest_kernel, n_blocks=n_blocks),
        grid=(n // tm,),
        in_specs=[pl.BlockSpec((tm, LANES), row),
                  pl.BlockSpec((tm, LANES), row),
                  pl.BlockSpec((SUBLANES, LANES), const)],
        out_specs=[pl.BlockSpec((tm // DEST_GROUP, LANES), row),
                   pl.BlockSpec((SUBLANES, nl), const)],
        out_shape=[jax.ShapeDtypeStruct((n // DEST_GROUP, LANES), I32),
                   jax.ShapeDtypeStruct((SUBLANES, nl), I32)],
        compiler_params=_cparams(("arbitrary",)),
        name="moe_dest",
    )(te128, pos128, cnt)


def _dispatch_kernel(rt_ref, nb_ref, h_ref, o_ref, buf, sem, *, tg):
    i = pl.program_id(0)
    n_used = (nb_ref[0] * MOE_TM + tg - 1) // tg
    slot = i % 2

    def issue(blk, s):
        def body(r, c):
            tok = rt_ref[blk * tg + r]
            pltpu.make_async_copy(h_ref.at[pl.ds(tok, 1), :], buf.at[s, pl.ds(r, 1), :], sem.at[s]).start()
            return c
        lax.fori_loop(0, tg, body, 0, unroll=4)

    @pl.when(i == 0)
    def _():
        issue(0, 0)

    @pl.when(i + 1 < n_used)
    def _():
        issue(i + 1, 1 - slot)

    @pl.when(i < n_used)
    def _():
        pltpu.make_async_copy(h_ref.at[pl.ds(0, tg), :], buf.at[slot], sem.at[slot]).wait()
        o_ref[...] = buf[slot].astype(BF16)

    @pl.when(i >= n_used)
    def _():
        o_ref[...] = jnp.zeros(o_ref.shape, o_ref.dtype)


def _dispatch(row_tok, n_blocks_used, h, rows_total):
    tg = 2 * MOE_TM
    d = h.shape[1]
    return pl.pallas_call(
        functools.partial(_dispatch_kernel, tg=tg),
        grid_spec=pltpu.PrefetchScalarGridSpec(
            num_scalar_prefetch=2,
            grid=(rows_total // tg,),
            in_specs=[pl.BlockSpec(memory_space=pl.ANY)],
            out_specs=pl.BlockSpec((tg, d), lambda i, rt, nb: (i, 0)),
            scratch_shapes=[pltpu.VMEM((2, tg, d), F32), pltpu.SemaphoreType.DMA((2,))]),
        out_shape=jax.ShapeDtypeStruct((rows_total, d), BF16),
        compiler_params=_cparams(("arbitrary",)),
        name="moe_dispatch",
    )(row_tok, n_blocks_used, h)


def _new_expert(be_ref, b):
    return (b == 0) | (be_ref[b] != be_ref[jnp.maximum(b - 1, 0)])


def _swiglu_pairs(v):
    g = jnp.minimum(v, SWIGLU_LIMIT)
    glu = g * jax.nn.sigmoid(g * SWIGLU_ALPHA)
    up1 = jnp.clip(v, -SWIGLU_LIMIT, SWIGLU_LIMIT) + 1.0
    return glu, up1


def _stream_expert_weights(be_ref, nb_ref, run_ctr, copies, consume):
    j = pl.program_id(0)
    b = pl.program_id(1)
    nj = pl.num_programs(0)
    nb = nb_ref[0]
    last_blk = be_ref.shape[0] - 1
    e = be_ref[b]

    @pl.when((j == 0) & (b == 0))
    def _():
        run_ctr[0] = 0
        for c in copies(0, e, 0):
            c.start()

    @pl.when((b < nb) & _new_expert(be_ref, b))
    def _():
        k = run_ctr[0]
        slot = k % 2
        for c in copies(j, e, slot):
            c.wait()
        consume(slot)
        run_end = lax.while_loop(lambda bb: (bb < nb) & (be_ref[jnp.minimum(bb, last_blk)] == e),
                                 lambda bb: bb + 1, b + 1)
        more_runs = run_end < nb
        j_next = jnp.where(more_runs, j, j + 1)
        e_next = jnp.where(more_runs, be_ref[jnp.minimum(run_end, last_blk)], be_ref[0])

        @pl.when(more_runs | (j + 1 < nj))
        def _():
            for c in copies(j_next, e_next, 1 - slot):
                c.start()
        run_ctr[0] = k + 1


def _moe_gu_kernel(be_ref, nb_ref, xb_ref, w_ref, ba_ref, bb_ref, o_ref, wa_s, wb_s, wbuf, sem, run_ctr,
                   *, layer, tn, nj):
    b = pl.program_id(1)
    active = b < nb_ref[0]

    def copies(j, e, slot):
        return [pltpu.make_async_copy(
            w_ref.at[layer, e, :, pl.ds(pl.multiple_of((j + h * nj) * tn, tn), tn)], wbuf.at[slot, h], sem.at[slot])
            for h in range(2)]

    def consume(slot):
        wa_s[...] = wbuf[slot, 0].astype(BF16)
        wb_s[...] = wbuf[slot, 1].astype(BF16)

    _stream_expert_weights(be_ref, nb_ref, run_ctr, copies, consume)

    @pl.when(active)
    def _():
        x = xb_ref[...]
        ga = jnp.dot(x, wa_s[...], preferred_element_type=F32) + ba_ref[0]
        gb = jnp.dot(x, wb_s[...], preferred_element_type=F32) + bb_ref[0]
        tm, tn = ga.shape
        even = lax.broadcasted_iota(I32, (tm, LANES), 1) % 2 == 0
        for c in range(tn // LANES):
            glu_a, up_a = _swiglu_pairs(ga[:, c * LANES:(c + 1) * LANES])
            glu_b, up_b = _swiglu_pairs(gb[:, c * LANES:(c + 1) * LANES])
            ra = glu_a * pltpu.roll(up_a, LANES - 1, 1)
            rb = pltpu.roll(glu_b, 1, 1) * up_b
            o_ref[:, c * LANES:(c + 1) * LANES] = jnp.where(even, ra, rb).astype(BF16)

    @pl.when(b >= nb_ref[0])
    def _():
        o_ref[...] = jnp.zeros(o_ref.shape, o_ref.dtype)


def _moe_gu(block_e, n_blocks_used, xb, wgu_all, layer, bgu):
    rows, d = xb.shape
    ne, f2 = bgu.shape
    tm = MOE_TM
    tn = 1024
    nj = f2 // 2 // tn
    return pl.pallas_call(
        functools.partial(_moe_gu_kernel, layer=layer, tn=tn, nj=nj),
        grid_spec=pltpu.PrefetchScalarGridSpec(
            num_scalar_prefetch=2,
            grid=(nj, rows // tm),
            in_specs=[pl.BlockSpec((tm, d), lambda j, b, be, nb: (b, 0)),
                      pl.BlockSpec(memory_space=pl.ANY),
                      pl.BlockSpec((1, 1, tn), lambda j, b, be, nb: (be[b], 0, j)),
                      pl.BlockSpec((1, 1, tn), lambda j, b, be, nb: (be[b], 0, j + nj))],
            out_specs=pl.BlockSpec((tm, tn), lambda j, b, be, nb: (b, j)),
            scratch_shapes=[pltpu.VMEM((d, tn), BF16), pltpu.VMEM((d, tn), BF16),
                            pltpu.VMEM((2, 2, d, tn), F32), pltpu.SemaphoreType.DMA((2,)),
                            pltpu.SMEM((1,), I32)]),
        out_shape=jax.ShapeDtypeStruct((rows, f2 // 2), BF16),
        compiler_params=_cparams(("arbitrary", "arbitrary")),
        name="moe_gate_up",
    )(block_e, n_blocks_used, xb, wgu_all, bgu.reshape(ne, 1, f2), bgu.reshape(ne, 1, f2))


def _moe_down_kernel(be_ref, nb_ref, a_ref, w_ref, bd_ref, o_ref, wp_s, stage, wbuf, sem, run_ctr, *, layer, tn):
    b = pl.program_id(1)
    active = b < nb_ref[0]

    def copies(j, e, slot):
        return [pltpu.make_async_copy(w_ref.at[layer, e, :, pl.ds(pl.multiple_of(j * tn, tn), tn)], wbuf.at[slot],
                                      sem.at[slot])]

    def consume(slot):
        f = wbuf.shape[1]
        for c in range(tn // LANES):
            cols = slice(c * LANES, (c + 1) * LANES)
            stage[c, pl.ds(0, f // 2, stride=2), :] = wbuf[slot, :f // 2, cols]
            stage[c, pl.ds(1, f // 2, stride=2), :] = wbuf[slot, f // 2:, cols]
            wp_s[:, cols] = stage[c].astype(BF16)

    _stream_expert_weights(be_ref, nb_ref, run_ctr, copies, consume)

    @pl.when(active)
    def _():
        o_ref[...] = jnp.dot(a_ref[...], wp_s[...], preferred_element_type=F32) + bd_ref[0]

    @pl.when(b >= nb_ref[0])
    def _():
        o_ref[...] = jnp.zeros(o_ref.shape, o_ref.dtype)


def _moe_down(block_e, n_blocks_used, act, wd_all, layer, bd):
    rows, f = act.shape
    ne, d = bd.shape
    tm = MOE_TM
    tn = 1024
    return pl.pallas_call(
        functools.partial(_moe_down_kernel, layer=layer, tn=tn),
        grid_spec=pltpu.PrefetchScalarGridSpec(
            num_scalar_prefetch=2,
            grid=(d // tn, rows // tm),
            in_specs=[pl.BlockSpec((tm, f), lambda j, b, be, nb: (b, 0)),
                      pl.BlockSpec(memory_space=pl.ANY),
                      pl.BlockSpec((1, 1, tn), lambda j, b, be, nb: (be[b], 0, j))],
            out_specs=pl.BlockSpec((tm, tn), lambda j, b, be, nb: (b, j)),
            scratch_shapes=[pltpu.VMEM((f, tn), BF16), pltpu.VMEM((tn // LANES, f, LANES), F32),
                            pltpu.VMEM((2, f, tn), F32), pltpu.SemaphoreType.DMA((2,)), pltpu.SMEM((1,), I32)]),
        out_shape=jax.ShapeDtypeStruct((rows, d), F32),
        compiler_params=_cparams(("arbitrary", "arbitrary")),
        name="moe_down",
    )(block_e, n_blocks_used, act, wd_all, bd.reshape(ne, 1, d))


def _combine_kernel(dest_ref, y_ref, x1_ref, gm_ref, tg_ref, fg_ref, o_ref, buf, sem, *, tn, tile0, final_norm):
    i = pl.program_id(0) + tile0

    def issue(t, c):
        for k in range(TOP_K):
            d = dest_ref[(i * tn + t) * TOP_K + k]
            pltpu.make_async_copy(y_ref.at[pl.ds(d, 1), :], buf.at[k, pl.ds(t, 1), :], sem).start()
        return c

    lax.fori_loop(0, tn, issue, 0, unroll=2)
    for k in range(TOP_K):
        pltpu.make_async_copy(y_ref.at[pl.ds(0, tn), :], buf.at[k], sem).wait()
    tg = tg_ref[...]
    moe = tg[:, 0:1] * buf[0]
    for k in range(1, TOP_K):
        moe = moe + tg[:, k:k + 1] * buf[k]
    x2 = _gated_residual(x1_ref[...], gm_ref[...], moe)
    if final_norm:
        x2 = _rms(x2, fg_ref[...])
    o_ref[...] = x2


def _combine(dest, y, x1, modg, tgates, fg, group, final_norm, row0, rows):
    d = x1.shape[1]
    tn = ROW_TILE
    ng = tn // group
    t0 = row0 // tn
    return pl.pallas_call(
        functools.partial(_combine_kernel, tn=tn, tile0=t0, final_norm=final_norm),
        grid_spec=pltpu.PrefetchScalarGridSpec(
            num_scalar_prefetch=1,
            grid=(rows // tn,),
            in_specs=[pl.BlockSpec(memory_space=pl.ANY),
                      pl.BlockSpec((tn, d), lambda i, ds: (i + t0, 0)),
                      pl.BlockSpec((ng, d), lambda i, ds: (i + t0, 5)),
                      pl.BlockSpec((tn, LANES), lambda i, ds: (i + t0, 0)),
                      pl.BlockSpec((1, d), lambda i, ds: (0, 0))],
            out_specs=pl.BlockSpec((tn, d), lambda i, ds: (i, 0)),
            scratch_shapes=[pltpu.VMEM((TOP_K, tn, d), F32), pltpu.SemaphoreType.DMA(())]),
        out_shape=jax.ShapeDtypeStruct((rows, d), F32),
        compiler_params=_cparams(("arbitrary",)),
        name="moe_combine",
    )(dest, y, x1, modg, tgates, fg.reshape(1, d))


def _moe(hp, te128, tg128, pos128, cnt, x1, modg, wgu_all, bgu, wd_all, bd, layer, fg, group, splits):
    n = x1.shape[0]
    rows_total = n * TOP_K + N_EXPERTS * MOE_TM
    n_blocks = rows_total // MOE_TM
    dest2d, meta = _moe_dest(te128, pos128, cnt, n_blocks)
    dest = dest2d.reshape(-1)
    block_e = meta[0, :n_blocks]
    nbu = meta[0, n_blocks:n_blocks + 1]
    tok = jnp.arange(n * TOP_K, dtype=I32) // TOP_K
    row_tok = jnp.zeros((rows_total,), I32).at[dest].set(tok, unique_indices=True)
    xb = _dispatch(row_tok, nbu, hp, rows_total)
    act = _moe_gu(block_e, nbu, xb, wgu_all, layer, bgu)
    y = _moe_down(block_e, nbu, act, wd_all, layer, bd)
    return [_combine(dest, y, x1, modg, tg128, fg, group, fn, r0, rows) for r0, rows, fn in splits]


def _lru_in_kernel(x_ref, g1_ref, sh_ref, sc_ref, wy_ref, wx_ref, y_ref, xb_ref):
    h = _norm_mod(x_ref[...], g1_ref[...], sh_ref[...], sc_ref[...]).astype(BF16)
    y = jnp.dot(h, wy_ref[...], preferred_element_type=F32)
    y_ref[...] = jax.nn.gelu(y, approximate=True).astype(BF16)
    xb_ref[...] = jnp.dot(h, wx_ref[...], preferred_element_type=F32)


def _lru_in(x, g1, modg, wy, wx, group):
    n, d = x.shape
    dr = wy.shape[1]
    tm = ROW_TILE
    ng = tm // group
    row = lambda i: (i, 0)
    const = lambda i: (0, 0)
    return pl.pallas_call(
        _lru_in_kernel,
        grid=(n // tm,),
        in_specs=[pl.BlockSpec((tm, d), row),
                  pl.BlockSpec((1, d), const),
                  pl.BlockSpec((ng, d), lambda i: (i, 0)),
                  pl.BlockSpec((ng, d), lambda i: (i, 1)),
                  pl.BlockSpec((d, dr), const),
                  pl.BlockSpec((d, dr), const)],
        out_specs=[pl.BlockSpec((tm, dr), row), pl.BlockSpec((tm, dr), row)],
        out_shape=[jax.ShapeDtypeStruct((n, dr), BF16), jax.ShapeDtypeStruct((n, dr), F32)],
        compiler_params=_cparams(("arbitrary",)),
        name="lru_in",
    )(x, g1.reshape(1, d), modg, modg, wy, wx)


def _lru_scan_kernel(y_ref, xb_ref, cb_ref, h0_ref, cw_ref, cbias_ref, wa_ref, ba_ref, wx_ref, bx_ref, lam_ref,
                     hy_ref, cbo_ref, ho_ref, xe, a_s, u_s, hc, *, tc, starts_at_pos0):
    c = pl.program_id(1)
    dr = xb_ref.shape[1]
    nb = wa_ref.shape[0]
    bd = dr // nb
    pre = SUBLANES

    @pl.when(c == 0)
    def _():
        xe[0:pre, :] = jnp.zeros((pre, dr), F32)
        xe[pre - (CONV_W - 1):pre, :] = cb_ref[0]
        hc[...] = jnp.broadcast_to(h0_ref[0], (SUBLANES, dr))

    xe[pre:pre + tc, :] = xb_ref[...]
    cw = cw_ref[...]
    xc = cbias_ref[...] + xe[pre:pre + tc, :] * cw[CONV_W - 1:CONV_W, :]
    for k in range(1, CONV_W):
        xc = xc + xe[pre - k:pre - k + tc, :] * cw[CONV_W - 1 - k:CONV_W - k, :]
    cbo_ref[0] = xe[pre + tc - (CONV_W - 1):pre + tc, :]
    xe[0:pre, :] = xe[tc:tc + pre, :]

    xcb = xc.astype(BF16)
    ra = jnp.concatenate([jnp.dot(xcb[:, n * bd:(n + 1) * bd], wa_ref[n], preferred_element_type=F32)
                          for n in range(nb)], axis=1)
    rx = jnp.concatenate([jnp.dot(xcb[:, n * bd:(n + 1) * bd], wx_ref[n], preferred_element_type=F32)
                          for n in range(nb)], axis=1)
    r = jax.nn.sigmoid(ra + ba_ref[...])
    ig = jax.nn.sigmoid(rx + bx_ref[...])
    lam = lam_ref[...]
    log_sig = jnp.minimum(lam, 0.0) - jnp.log1p(jnp.exp(-jnp.abs(lam)))
    log_a = LRU_C * r * log_sig
    a = jnp.exp(log_a)
    th = jnp.tanh(log_a)
    mult = jnp.sqrt(-2.0 * th / (1.0 - th))
    if starts_at_pos0:
        first = (lax.broadcasted_iota(I32, (tc, 1), 0) == 0) & (c == 0)
        mult = jnp.where(first, 1.0, mult)
    a_s[...] = a
    u_s[...] = mult * ig * xc

    row8 = lax.broadcasted_iota(I32, (SUBLANES, dr), 0)

    def group_step(g, hprev):
        off = pl.multiple_of(g * SUBLANES, SUBLANES)
        aa = a_s[pl.ds(off, SUBLANES), :]
        uu = u_s[pl.ds(off, SUBLANES), :]
        s = 1
        while s < SUBLANES:
            m = row8 >= s
            uu = jnp.where(m, aa * pltpu.roll(uu, s, 0) + uu, uu)
            aa = jnp.where(m, aa * pltpu.roll(aa, s, 0), aa)
            s *= 2
        hh = aa * hprev + uu
        u_s[pl.ds(off, SUBLANES), :] = hh
        return jnp.broadcast_to(hh[SUBLANES - 1:SUBLANES, :], (SUBLANES, dr))

    hlast = lax.fori_loop(0, tc // SUBLANES, group_step, hc[...])
    hc[...] = hlast
    ho_ref[0] = hlast[0:1, :]
    hy_ref[...] = (u_s[...] * y_ref[...].astype(F32)).astype(BF16)


def _lru_scan(yb, xb, conv_buf, h0, cw, cbias, wa, ba, wx, bx, lam, row0, n_seq, t, tc, starts_at_pos0):
    dr = xb.shape[1]
    nc = t // tc
    rb0 = row0 // tc
    inmap = lambda s, c: (rb0 + s * nc + c, 0)
    outmap = lambda s, c: (s * nc + c, 0)
    const2 = lambda s, c: (0, 0)
    const3 = lambda s, c: (0, 0, 0)
    seq3 = lambda s, c: (s, 0, 0)
    return pl.pallas_call(
        functools.partial(_lru_scan_kernel, tc=tc, starts_at_pos0=starts_at_pos0),
        grid=(n_seq, nc),
        in_specs=[pl.BlockSpec((tc, dr), inmap),
                  pl.BlockSpec((tc, dr), inmap),
                  pl.BlockSpec((1, CONV_W - 1, dr), seq3),
                  pl.BlockSpec((1, 1, dr), seq3),
                  pl.BlockSpec((CONV_W, dr), const2),
                  pl.BlockSpec((1, dr), const2),
                  pl.BlockSpec(wa.shape, const3),
                  pl.BlockSpec((1, dr), const2),
                  pl.BlockSpec(wx.shape, const3),
                  pl.BlockSpec((1, dr), const2),
                  pl.BlockSpec((1, dr), const2)],
        out_specs=[pl.BlockSpec((tc, dr), outmap),
                   pl.BlockSpec((1, CONV_W - 1, dr), seq3),
                   pl.BlockSpec((1, 1, dr), seq3)],
        out_shape=[jax.ShapeDtypeStruct((n_seq * t, dr), BF16),
                   jax.ShapeDtypeStruct((n_seq, CONV_W - 1, dr), F32),
                   jax.ShapeDtypeStruct((n_seq, 1, dr), F32)],
        scratch_shapes=[pltpu.VMEM((SUBLANES + tc, dr), F32),
                        pltpu.VMEM((tc, dr), F32),
                        pltpu.VMEM((tc, dr), F32),
                        pltpu.VMEM((SUBLANES, dr), F32)],
        compiler_params=_cparams(("arbitrary", "arbitrary")),
        name="lru_scan",
    )(yb, xb, conv_buf, h0.reshape(n_seq, 1, dr), cw, cbias.reshape(1, dr), wa, ba.reshape(1, dr),
      wx, bx.reshape(1, dr), lam.reshape(1, dr))


def _rope_tables(pos):
    half = ROPE_DIM // 2
    inv = 1.0 / (ROPE_THETA ** (jnp.arange(0, ROPE_DIM, 2, dtype=F32) / ROPE_DIM))
    ang = pos.astype(F32)[:, None] * inv[None, :]
    cos, sin = jnp.cos(ang), jnp.sin(ang)
    z = jnp.zeros((pos.shape[0], LANES - ROPE_DIM), F32)
    return jnp.concatenate([cos, cos, z], axis=1), jnp.concatenate([-sin, sin, z], axis=1)


def _head_slab_weights(w_uq):
    ql = w_uq.shape[0]
    w = w_uq.reshape(ql, N_HEADS, NOPE_DIM + ROPE_DIM)
    z = jnp.zeros((ql, N_HEADS, HEAD_W - NOPE_DIM - ROPE_DIM), w.dtype)
    return jnp.concatenate([w, z], axis=2).reshape(ql, N_HEADS * HEAD_W).astype(BF16)


def kernel(x_prompt, x_sample, cache_ckv, cache_krope, state_conv, state_h, c_prompt, c_sample,
           mod_w, mod_b, norm1_g, norm2_g,
           mla_w_in, mla_q_norm_g, mla_kv_norm_g, mla_w_uq, mla_w_ukv, mla_w_o,
           lru_w_in, lru_conv_w, lru_conv_b, lru_w_a, lru_b_a, lru_w_x, lru_b_x, lru_lambda, lru_w_o,
           router_w, router_b, moe_w_gu, moe_b_gu, moe_w_down, moe_b_down, final_g):
    bp, tp, d = x_prompt.shape
    bs, ts, _ = x_sample.shape
    past = cache_ckv.shape[2]
    depth = mod_w.shape[0]
    n_p, n_s = bp * tp, bs * ts
    n = n_p + n_s
    group = math.gcd(tp, ts)
    assert group % SUBLANES == 0 and ROW_TILE % group == 0 and n_p % ROW_TILE == 0 and n_s % ROW_TILE == 0
    assert NOPE_DIM == LANES and V_DIM == LANES and ROPE_DIM <= LANES and n_p % n_s == 0

    x = jnp.concatenate([x_prompt.reshape(n_p, d), x_sample.reshape(n_s, d)], axis=0)
    nb = bp + bs
    nb_pad = -(-nb // SUBLANES) * SUBLANES
    c_all = jnp.concatenate([c_prompt, c_sample, jnp.zeros((nb_pad - nb, d), F32)], axis=0)
    grp_batch = np.concatenate([np.repeat(np.arange(bp), tp // group), bp + np.repeat(np.arange(bs), ts // group)])
    pos = jnp.concatenate([jnp.tile(jnp.arange(tp), bp), jnp.tile(past + jnp.arange(ts), bs)])
    cos, sin = _rope_tables(pos)

    outs = {k: [] for k in ("ckv_p", "kr_p", "conv_p", "h_p", "ckv_s", "kr_s", "conv_s", "h_s")}
    for i in range(depth):
        mod = _adaln(c_all, mod_w, i, mod_b[i])
        modg = jnp.take(mod, jnp.asarray(grp_batch), axis=0)
        j = i // 2
        if i % 2 == 0:
            w_in = mla_w_in[j]
            zpad = jnp.zeros((d, LANES - ROPE_DIM), F32)
            win = jnp.concatenate([w_in, zpad], axis=1).astype(BF16)
            wuq = _head_slab_weights(mla_w_uq[j])
            wukv = mla_w_ukv[j].reshape(KV_LORA, N_HEADS, NOPE_DIM + V_DIM)
            wk = wukv[:, :, :NOPE_DIM].reshape(KV_LORA, N_HEADS * NOPE_DIM).astype(BF16)
            wv = wukv[:, :, NOPE_DIM:].reshape(KV_LORA, N_HEADS * V_DIM).astype(BF16)
            q, ckv, kr, ckvb, krp = _mla_proj(x, norm1_g[i], modg, win, mla_q_norm_g[j], mla_kv_norm_g[j],
                                              wuq, cos, sin, group)
            kk, vv = _kv_expand(ckvb, krp, wk, wv, n_p)
            o_p = _attn_prompt(q, kk, vv, bp, tp)
            qa = _absorb(q, wk, n_p, n_s)
            ckr_pad = jnp.pad(cache_krope[j], ((0, 0), (0, 0), (0, LANES - ROPE_DIM)))
            o_lat = _attn_sample(qa, q, cache_ckv[j], ckr_pad, ckvb, krp, n_p, bs, ts, past)
            o_s = _unabsorb(o_lat, wv, n_s)
            wo = mla_w_o[j].astype(BF16)
            outs["ckv_p"].append(ckv[:n_p].reshape(bp, tp, KV_LORA))
            outs["kr_p"].append(kr[:n_p].reshape(bp, tp, ROPE_DIM))
            outs["ckv_s"].append(ckv[n_p:].reshape(bs, ts, KV_LORA))
            outs["kr_s"].append(kr[n_p:].reshape(bs, ts, ROPE_DIM))
        else:
            dr = lru_w_in.shape[2] // 2
            wy = lru_w_in[j][:, :dr].astype(BF16)
            wx = lru_w_in[j][:, dr:].astype(BF16)
            yb, xb = _lru_in(x, norm1_g[i], modg, wy, wx, group)
            wa = lru_w_a[j].astype(BF16)
            wxg = lru_w_x[j].astype(BF16)
            lru_args = (lru_conv_w[j], lru_conv_b[j], wa, lru_b_a[j], wxg, lru_b_x[j], lru_lambda[j])
            zbuf = jnp.zeros((bp, CONV_W - 1, dr), F32)
            zh = jnp.zeros((bp, dr), F32)
            o_p, cb_p, h_p = _lru_scan(yb, xb, zbuf, zh, *lru_args, 0, bp, tp, ROW_TILE, True)
            o_s, cb_s, h_s = _lru_scan(yb, xb, state_conv[j], state_h[j], *lru_args, n_p, bs, ts, ts, False)
            wo = lru_w_o[j].astype(BF16)
            outs["conv_p"].append(cb_p)
            outs["h_p"].append(h_p.reshape(bp, dr))
            outs["conv_s"].append(cb_s)
            outs["h_s"].append(h_s.reshape(bs, dr))
        x1, hp, te128, tg128, pos128, cnt = _post_mixer(o_p, o_s, wo, x, norm2_g[i], modg, router_w[i], router_b[i],
                                                        group)
        last = i == depth - 1
        splits = [(0, n_p, True), (n_p, n_s, True)] if last else [(0, n, False)]
        res = _moe(hp, te128, tg128, pos128, cnt, x1, modg, moe_w_gu, moe_b_gu[i], moe_w_down, moe_b_down[i], i,
                   final_g, group, splits)
        x = res[0]
    y_prompt = res[0].reshape(bp, tp, d)
    y_sample = res[1].reshape(bs, ts, d)
    return (y_prompt, y_sample,
            jnp.stack(outs["ckv_p"]), jnp.stack(outs["kr_p"]), jnp.stack(outs["conv_p"]), jnp.stack(outs["h_p"]),
            jnp.stack(outs["ckv_s"]), jnp.stack(outs["kr_s"]), jnp.stack(outs["conv_s"]), jnp.stack(outs["h_s"]))
```

```python
import functools
import math

import jax
import jax.numpy as jnp
import numpy as np
from jax import lax
from jax.experimental import pallas as pl
from jax.experimental.pallas import tpu as pltpu

F32 = jnp.float32
BF16 = jnp.bfloat16
I32 = jnp.int32

CHUNK = 64
N_HEADS = 16
Q_LORA = 512
KV_LORA = 512
NOPE_DIM = 128
ROPE_DIM = 64
V_DIM = 128
ROPE_THETA = 10000.0
LRU_BLOCKS = 8
CONV_W = 4
LRU_C = 8.0
N_EXPERTS = 32
TOP_K = 4
SWIGLU_LIMIT = 7.0
SWIGLU_ALPHA = 1.702
N_MOD = 6
EPS = 1e-6

LANES = 128
SUBLANES = 8
HEAD_W = 2 * LANES

ROW_TILE = 256
MOE_TM = 256
VMEM_LIMIT = 56 * 1024 * 1024


def _cparams(sem):
    return pltpu.CompilerParams(dimension_semantics=sem, vmem_limit_bytes=VMEM_LIMIT)


def _rms(x, g):
    ms = jnp.mean(x * x, axis=-1, keepdims=True)
    return x * lax.rsqrt(ms + EPS) * g


def _norm_mod(x, g, shift, scale):
    tm, d = x.shape
    ng = shift.shape[0]
    y = _rms(x, g).reshape(ng, tm // ng, d)
    return (y * (1.0 + scale[:, None, :]) + shift[:, None, :]).reshape(tm, d)


def _gated_residual(x, gate, y):
    tm, d = x.shape
    ng = gate.shape[0]
    return (x.reshape(ng, tm // ng, d) + gate[:, None, :] * y.reshape(ng, tm // ng, d)).reshape(tm, d)


def _adaln_kernel(c_ref, w_ref, b_ref, o_ref):
    c = c_ref[...]
    a = (c * jax.nn.sigmoid(c)).astype(BF16)
    o_ref[...] = jnp.dot(a, w_ref[0].astype(BF16), preferred_element_type=F32) + b_ref[...]


def _adaln(c_all, w_all, layer, b):
    bp, d = c_all.shape
    n = w_all.shape[2]
    tn = 1024
    return pl.pallas_call(
        _adaln_kernel,
        grid=(n // tn,),
        in_specs=[pl.BlockSpec((bp, d), lambda j: (0, 0)),
                  pl.BlockSpec((1, d, tn), lambda j: (layer, 0, j)),
                  pl.BlockSpec((1, tn), lambda j: (0, j))],
        out_specs=pl.BlockSpec((bp, tn), lambda j: (0, j)),
        out_shape=jax.ShapeDtypeStruct((bp, n), F32),
        compiler_params=_cparams(("arbitrary",)),
        name="adaln",
    )(c_all, w_all, b.reshape(1, n))


def _rope128(v, cos, sin):
    half = ROPE_DIM // 2
    lane = lax.broadcasted_iota(I32, v.shape, 1)
    sw = jnp.where(lane < half, pltpu.roll(v, LANES - half, 1), pltpu.roll(v, half, 1))
    return v * cos + sw * sin


def _mla_proj_kernel(x_ref, g1_ref, sh_ref, sc_ref, win_ref, qg_ref, kvg_ref, wuq_ref, cos_ref, sin_ref,
                     q_ref, ckv_ref, kr_ref, ckvb_ref, krp_ref):
    h = _norm_mod(x_ref[...], g1_ref[...], sh_ref[...], sc_ref[...]).astype(BF16)
    lat = jnp.dot(h, win_ref[...], preferred_element_type=F32)
    q_lat = lat[:, :Q_LORA]
    c_kv = lat[:, Q_LORA:Q_LORA + KV_LORA]
    k_r = lat[:, Q_LORA + KV_LORA:]
    qn = _rms(q_lat, qg_ref[...]).astype(BF16)
    q = jnp.dot(qn, wuq_ref[...], preferred_element_type=F32)
    cos = cos_ref[...]
    sin = sin_ref[...]
    scale = (NOPE_DIM + ROPE_DIM) ** -0.5
    for hh in range(N_HEADS):
        lo = hh * HEAD_W
        q_ref[:, lo:lo + LANES] = (q[:, lo:lo + LANES] * scale).astype(BF16)
        q_ref[:, lo + LANES:lo + HEAD_W] = (_rope128(q[:, lo + LANES:lo + HEAD_W], cos, sin) * scale).astype(BF16)
    ckv = _rms(c_kv, kvg_ref[...])
    ckv_ref[...] = ckv
    ckvb_ref[...] = ckv.astype(BF16)
    kr = _rope128(k_r, cos, sin)
    kr_ref[...] = kr[:, :ROPE_DIM]
    krp_ref[...] = kr.astype(BF16)


def _mla_proj(x, g1, modg, win, qg, kvg, wuq, cos, sin, group):
    n, d = x.shape
    tm = ROW_TILE
    ng = tm // group
    wl = win.shape[1]
    qw = wuq.shape[1]
    row = lambda i: (i, 0)
    const = lambda i: (0, 0)
    return pl.pallas_call(
        _mla_proj_kernel,
        grid=(n // tm,),
        in_specs=[pl.BlockSpec((tm, d), row),
                  pl.BlockSpec((1, d), const),
                  pl.BlockSpec((ng, d), lambda i: (i, 0)),
                  pl.BlockSpec((ng, d), lambda i: (i, 1)),
                  pl.BlockSpec((d, wl), const),
                  pl.BlockSpec((1, Q_LORA), const),
                  pl.BlockSpec((1, KV_LORA), const),
                  pl.BlockSpec((Q_LORA, qw), const),
                  pl.BlockSpec((tm, LANES), row),
                  pl.BlockSpec((tm, LANES), row)],
        out_specs=[pl.BlockSpec((tm, qw), row),
                   pl.BlockSpec((tm, KV_LORA), row),
                   pl.BlockSpec((tm, ROPE_DIM), row),
                   pl.BlockSpec((tm, KV_LORA), row),
                   pl.BlockSpec((tm, LANES), row)],
        out_shape=[jax.ShapeDtypeStruct((n, qw), BF16),
                   jax.ShapeDtypeStruct((n, KV_LORA), F32),
                   jax.ShapeDtypeStruct((n, ROPE_DIM), F32),
                   jax.ShapeDtypeStruct((n, KV_LORA), BF16),
                   jax.ShapeDtypeStruct((n, LANES), BF16)],
        compiler_params=_cparams(("arbitrary",)),
        name="mla_proj",
    )(x, g1.reshape(1, d), modg, modg, win, qg.reshape(1, -1), kvg.reshape(1, -1), wuq, cos, sin)


def _kv_expand_kernel(c_ref, krp_ref, wk_ref, wv_ref, k_ref, v_ref):
    c = c_ref[...]
    kn = jnp.dot(c, wk_ref[...], preferred_element_type=F32).astype(BF16)
    krp = krp_ref[...]
    for hh in range(N_HEADS):
        k_ref[:, hh * HEAD_W:hh * HEAD_W + LANES] = kn[:, hh * NOPE_DIM:(hh + 1) * NOPE_DIM]
        k_ref[:, hh * HEAD_W + LANES:(hh + 1) * HEAD_W] = krp
    v_ref[...] = jnp.dot(c, wv_ref[...], preferred_element_type=F32).astype(BF16)


def _kv_expand(ckvb, krp, wk, wv, rows):
    tm = 512
    row = lambda i: (i, 0)
    const = lambda i: (0, 0)
    return pl.pallas_call(
        _kv_expand_kernel,
        grid=(rows // tm,),
        in_specs=[pl.BlockSpec((tm, KV_LORA), row),
                  pl.BlockSpec((tm, LANES), row),
                  pl.BlockSpec(wk.shape, const),
                  pl.BlockSpec(wv.shape, const)],
        out_specs=[pl.BlockSpec((tm, N_HEADS * HEAD_W), row),
                   pl.BlockSpec((tm, N_HEADS * V_DIM), row)],
        out_shape=[jax.ShapeDtypeStruct((rows, N_HEADS * HEAD_W), BF16),
                   jax.ShapeDtypeStruct((rows, N_HEADS * V_DIM), BF16)],
        compiler_params=_cparams(("arbitrary",)),
        name="kv_expand",
    )(ckvb, krp, wk, wv)


ATTN_HEADS_PER_STEP = 2


def _attn_prompt_kernel(q_ref, k_ref, v_ref, o_ref, *, tq, nq):
    qi = pl.program_id(2)
    dn = (((1,), (1,)), ((), ()))
    hs = ATTN_HEADS_PER_STEP
    r = lax.broadcasted_iota(I32, (tq, tq), 0) // CHUNK
    c = lax.broadcasted_iota(I32, (tq, tq), 1) // CHUNK
    diag_visible = c <= r
    for qs in range(nq):
        @pl.when(qi == qs)
        def _(qs=qs):
            past = qs * tq
            for h in range(hs):
                q = q_ref[:, h * HEAD_W:(h + 1) * HEAD_W]
                kcols = slice(h * HEAD_W, (h + 1) * HEAD_W)
                vcols = slice(h * V_DIM, (h + 1) * V_DIM)
                s_d = lax.dot_general(q, k_ref[past:past + tq, kcols], dn, preferred_element_type=F32)
                s_d = jnp.where(diag_visible, s_d, -jnp.inf)
                m = jnp.max(s_d, axis=-1, keepdims=True)
                if past:
                    s_f = lax.dot_general(q, k_ref[0:past, kcols], dn, preferred_element_type=F32)
                    m = jnp.maximum(m, jnp.max(s_f, axis=-1, keepdims=True))
                p_d = jnp.exp(s_d - m)
                l = jnp.sum(p_d, axis=-1, keepdims=True)
                acc = jnp.dot(p_d.astype(BF16), v_ref[past:past + tq, vcols], preferred_element_type=F32)
                if past:
                    p_f = jnp.exp(s_f - m)
                    l = l + jnp.sum(p_f, axis=-1, keepdims=True)
                    acc = acc + jnp.dot(p_f.astype(BF16), v_ref[0:past, vcols], preferred_element_type=F32)
                o_ref[:, vcols] = (acc / l).astype(BF16)


def _attn_prompt(q, k, v, bp, tp):
    n_rows = bp * tp
    tq = 256
    nq = tp // tq
    hs = ATTN_HEADS_PER_STEP
    return pl.pallas_call(
        functools.partial(_attn_prompt_kernel, tq=tq, nq=nq),
        grid=(bp, N_HEADS // hs, nq),
        in_specs=[pl.BlockSpec((tq, hs * HEAD_W), lambda b, h, i: (b * nq + i, h)),
                  pl.BlockSpec((tp, hs * HEAD_W), lambda b, h, i: (b, h)),
                  pl.BlockSpec((tp, hs * V_DIM), lambda b, h, i: (b, h))],
        out_specs=pl.BlockSpec((tq, hs * V_DIM), lambda b, h, i: (b * nq + i, h)),
        out_shape=jax.ShapeDtypeStruct((n_rows, N_HEADS * V_DIM), BF16),
        compiler_params=_cparams(("arbitrary", "arbitrary", "arbitrary")),
        name="attn_prompt",
    )(q, k, v)


def _absorb_kernel(q_ref, wk_ref, o_ref):
    dn = (((1,), (1,)), ((), ()))
    o_ref[0] = lax.dot_general(q_ref[...], wk_ref[...], dn, preferred_element_type=F32).astype(BF16)


def _absorb(q, wk, row0, rows):
    rb = row0 // rows
    return pl.pallas_call(
        _absorb_kernel,
        grid=(N_HEADS,),
        in_specs=[pl.BlockSpec((rows, LANES), lambda h: (rb, 2 * h)),
                  pl.BlockSpec((KV_LORA, NOPE_DIM), lambda h: (0, h))],
        out_specs=pl.BlockSpec((1, rows, KV_LORA), lambda h: (h, 0, 0)),
        out_shape=jax.ShapeDtypeStruct((N_HEADS, rows, KV_LORA), BF16),
        compiler_params=_cparams(("arbitrary",)),
        name="absorb",
    )(q, wk)


def _attn_sample_kernel(qa_ref, q_ref, cc_ref, ckr_ref, cn_ref, krn_ref, o_ref, *, ts, past):
    hn = N_HEADS
    qa = qa_ref[...].reshape(hn * ts, KV_LORA)
    qfull = q_ref[...]
    qr = jnp.concatenate([qfull[:, h * HEAD_W + LANES:(h + 1) * HEAD_W] for h in range(hn)], axis=0)
    cc = cc_ref[0].astype(BF16)
    ckr = ckr_ref[0].astype(BF16)
    cn = cn_ref[...]
    krn = krn_ref[...]
    dn = (((1,), (1,)), ((), ()))
    s_c = (lax.dot_general(qa, cc, dn, preferred_element_type=F32)
           + lax.dot_general(qr, ckr, dn, preferred_element_type=F32))
    s_n = (lax.dot_general(qa, cn, dn, preferred_element_type=F32)
           + lax.dot_general(qr, krn, dn, preferred_element_type=F32))
    qchunk_c = (past + lax.broadcasted_iota(I32, s_c.shape, 0) % ts) // CHUNK
    s_c = jnp.where(lax.broadcasted_iota(I32, s_c.shape, 1) // CHUNK <= qchunk_c, s_c, -jnp.inf)
    qchunk_n = (past + lax.broadcasted_iota(I32, s_n.shape, 0) % ts) // CHUNK
    s_n = jnp.where((past + lax.broadcasted_iota(I32, s_n.shape, 1)) // CHUNK <= qchunk_n, s_n, -jnp.inf)
    m = jnp.maximum(jnp.max(s_c, axis=-1, keepdims=True), jnp.max(s_n, axis=-1, keepdims=True))
    p_c = jnp.exp(s_c - m)
    p_n = jnp.exp(s_n - m)
    l = jnp.sum(p_c, axis=-1, keepdims=True) + jnp.sum(p_n, axis=-1, keepdims=True)
    o = (jnp.dot(p_c.astype(BF16), cc, preferred_element_type=F32)
         + jnp.dot(p_n.astype(BF16), cn, preferred_element_type=F32)) / l
    o_ref[...] = o.astype(BF16).reshape(hn, ts, KV_LORA)


def _attn_sample(qa, q, cache_c, cache_kr, ckvb, krp, row0, bs, ts, past):
    rb0 = row0 // ts
    return pl.pallas_call(
        functools.partial(_attn_sample_kernel, ts=ts, past=past),
        grid=(bs,),
        in_specs=[pl.BlockSpec((N_HEADS, ts, KV_LORA), lambda b: (0, b, 0)),
                  pl.BlockSpec((ts, N_HEADS * HEAD_W), lambda b: (rb0 + b, 0)),
                  pl.BlockSpec((1, past, KV_LORA), lambda b: (b, 0, 0)),
                  pl.BlockSpec((1, past, LANES), lambda b: (b, 0, 0)),
                  pl.BlockSpec((ts, KV_LORA), lambda b: (rb0 + b, 0)),
                  pl.BlockSpec((ts, LANES), lambda b: (rb0 + b, 0))],
        out_specs=pl.BlockSpec((N_HEADS, ts, KV_LORA), lambda b: (0, b, 0)),
        out_shape=jax.ShapeDtypeStruct((N_HEADS, bs * ts, KV_LORA), BF16),
        compiler_params=_cparams(("arbitrary",)),
        name="attn_sample",
    )(qa, q, cache_c, cache_kr, ckvb, krp)


def _unabsorb_kernel(ol_ref, wv_ref, o_ref):
    o_ref[...] = jnp.dot(ol_ref[0], wv_ref[...], preferred_element_type=F32).astype(BF16)


def _unabsorb(o_lat, wv, rows):
    return pl.pallas_call(
        _unabsorb_kernel,
        grid=(N_HEADS,),
        in_specs=[pl.BlockSpec((1, rows, KV_LORA), lambda h: (h, 0, 0)),
                  pl.BlockSpec((KV_LORA, V_DIM), lambda h: (0, h))],
        out_specs=pl.BlockSpec((rows, V_DIM), lambda h: (0, h)),
        out_shape=jax.ShapeDtypeStruct((rows, N_HEADS * V_DIM), BF16),
        compiler_params=_cparams(("arbitrary",)),
        name="unabsorb",
    )(o_lat, wv)


def _post_mixer_kernel(op_ref, os_ref, wo_ref, x_ref, gate_ref, g2_ref, sh_ref, sc_ref, rw_ref, rb_ref,
                       x1_ref, hp_ref, te_ref, tg_ref, pos_ref, cnt_ref, carry, *, prompt_tiles):
    @pl.when(pl.program_id(0) == 0)
    def _():
        carry[...] = jnp.zeros(carry.shape, F32)

    o = jnp.where(pl.program_id(0) < prompt_tiles, op_ref[...], os_ref[...])
    y = jnp.dot(o, wo_ref[...], preferred_element_type=F32)
    x1 = _gated_residual(x_ref[...], gate_ref[...], y)
    x1_ref[...] = x1
    h2 = _norm_mod(x1, g2_ref[...], sh_ref[...], sc_ref[...])
    hp_ref[...] = h2
    rw = rw_ref[...]
    h_hi = h2.astype(BF16)
    h_lo = (h2 - h_hi.astype(F32)).astype(BF16)
    w_hi = rw.astype(BF16)
    w_lo = (rw - w_hi.astype(F32)).astype(BF16)
    logits = (jnp.dot(h_hi, w_hi, preferred_element_type=F32) + jnp.dot(h_lo, w_hi, preferred_element_type=F32)
              + jnp.dot(h_hi, w_lo, preferred_element_type=F32) + rb_ref[...])
    tm, ne = logits.shape
    eid = lax.broadcasted_iota(I32, (tm, ne), 1)
    lane = lax.broadcasted_iota(I32, (tm, LANES), 1)
    te = jnp.zeros((tm, LANES), I32)
    tv = jnp.full((tm, LANES), -jnp.inf, F32)
    work = logits
    picks = []
    for k in range(TOP_K):
        mx = jnp.max(work, axis=-1, keepdims=True)
        idx = jnp.min(jnp.where(work == mx, eid, ne), axis=-1, keepdims=True)
        picks.append(idx)
        te = jnp.where(lane == k, idx, te)
        tv = jnp.where(lane == k, mx, tv)
        work = jnp.where(eid == idx, -jnp.inf, work)
    ex = jnp.exp(tv - jnp.max(tv, axis=-1, keepdims=True))
    te_ref[...] = te
    tg_ref[...] = ex / jnp.sum(ex, axis=-1, keepdims=True)
    onehot = jnp.zeros((tm, LANES), F32)
    for idx in picks:
        onehot = onehot + (lane == idx).astype(F32)
    tri = (lax.broadcasted_iota(I32, (tm, tm), 1) < lax.broadcasted_iota(I32, (tm, tm), 0)).astype(BF16)
    rank = jnp.dot(tri, onehot.astype(BF16), preferred_element_type=F32) + carry[0:1, :]
    pos = jnp.zeros((tm, LANES), I32)
    for k, idx in enumerate(picks):
        pk = jnp.sum(jnp.where(lane == idx, rank, 0.0), axis=-1, keepdims=True)
        pos = jnp.where(lane == k, pk.astype(I32), pos)
    pos_ref[...] = pos
    total = carry[0:1, :] + jnp.sum(onehot, axis=0, keepdims=True)
    carry[...] = jnp.broadcast_to(total, carry.shape)
    cnt_ref[...] = carry[...]


def _post_mixer(o_p, o_s, wo, x, g2, modg, rw, rb, group):
    n, d = x.shape
    tm = ROW_TILE
    ng = tm // group
    npt = o_p.shape[0] // tm
    nst = o_s.shape[0] // tm
    row = lambda i: (i, 0)
    const = lambda i: (0, 0)
    return pl.pallas_call(
        functools.partial(_post_mixer_kernel, prompt_tiles=npt),
        grid=(n // tm,),
        in_specs=[pl.BlockSpec((tm, o_p.shape[1]), lambda i: (jnp.minimum(i, npt - 1), 0)),
                  pl.BlockSpec((tm, o_s.shape[1]), lambda i: (jnp.clip(i - npt, 0, nst - 1), 0)),
                  pl.BlockSpec(wo.shape, const),
                  pl.BlockSpec((tm, d), row),
                  pl.BlockSpec((ng, d), lambda i: (i, 2)),
                  pl.BlockSpec((1, d), const),
                  pl.BlockSpec((ng, d), lambda i: (i, 3)),
                  pl.BlockSpec((ng, d), lambda i: (i, 4)),
                  pl.BlockSpec(rw.shape, const),
                  pl.BlockSpec((1, rw.shape[1]), const)],
        out_specs=[pl.BlockSpec((tm, d), row),
                   pl.BlockSpec((tm, d), row),
                   pl.BlockSpec((tm, LANES), row),
                   pl.BlockSpec((tm, LANES), row),
                   pl.BlockSpec((tm, LANES), row),
                   pl.BlockSpec((SUBLANES, LANES), const)],
        out_shape=[jax.ShapeDtypeStruct((n, d), F32),
                   jax.ShapeDtypeStruct((n, d), F32),
                   jax.ShapeDtypeStruct((n, LANES), I32),
                   jax.ShapeDtypeStruct((n, LANES), F32),
                   jax.ShapeDtypeStruct((n, LANES), I32),
                   jax.ShapeDtypeStruct((SUBLANES, LANES), F32)],
        scratch_shapes=[pltpu.VMEM((SUBLANES, LANES), F32)],
        compiler_params=_cparams(("arbitrary",)),
        name="post_mixer",
    )(o_p, o_s, wo, x, modg, g2.reshape(1, d), modg, modg, rw, rb.reshape(1, -1))


DEST_GROUP = LANES // TOP_K


def _moe_dest_kernel(te_ref, pos_ref, cnt_ref, dest_ref, meta_ref, *, n_blocks):
    shift = MOE_TM.bit_length() - 1
    lane8 = lax.broadcasted_iota(I32, (SUBLANES, LANES), 1)
    cnt = cnt_ref[...].astype(I32)
    padded = ((cnt + (MOE_TM - 1)) >> shift) << shift
    ends = padded.astype(F32)
    s = 1
    while s < N_EXPERTS:
        ends = ends + jnp.where(lane8 >= s, pltpu.roll(ends, s, 1), 0.0)
        s *= 2
    ends_row = ends[0:1, :]
    starts_row = (ends - padded.astype(F32))[0:1, :]
    te = te_ref[...]
    pos = pos_ref[...]
    tm = te.shape[0]
    lane = lax.broadcasted_iota(I32, (tm, LANES), 1)
    dest = jnp.zeros((tm, LANES), F32)
    for k in range(TOP_K):
        sk = jnp.sum(jnp.where(lane == te[:, k:k + 1], starts_row, 0.0), axis=-1, keepdims=True)
        dest = jnp.where(lane == k, sk + pos[:, k:k + 1].astype(F32), dest)
    hi = jnp.floor(dest * (1.0 / 256.0))
    lo = dest - 256.0 * hi
    sel = (lax.broadcasted_iota(I32, (LANES, LANES), 0)
           == lax.broadcasted_iota(I32, (LANES, LANES), 1) % TOP_K).astype(BF16)
    spread = (256.0 * jnp.dot(hi.astype(BF16), sel, preferred_element_type=F32)
              + jnp.dot(lo.astype(BF16), sel, preferred_element_type=F32))
    row = lax.broadcasted_iota(I32, (tm, LANES), 0)
    keep = lane // TOP_K == row % DEST_GROUP
    dense = jnp.sum(jnp.where(keep, spread, 0.0).reshape(tm // DEST_GROUP, DEST_GROUP, LANES), axis=1)
    dest_ref[...] = dense.astype(I32)

    @pl.when(pl.program_id(0) == 0)
    def _():
        nl = meta_ref.shape[1]
        r_i = lax.broadcasted_iota(I32, (LANES, LANES), 0)
        l_i = lax.broadcasted_iota(I32, (LANES, LANES), 1)
        ends_col = jnp.sum(jnp.where(l_i == r_i, ends_row, 0.0), axis=-1, keepdims=True)
        e_i = lax.broadcasted_iota(I32, (LANES, nl), 0)
        b_i = lax.broadcasted_iota(I32, (LANES, nl), 1)
        closed = (e_i < N_EXPERTS) & (ends_col <= (b_i * MOE_TM).astype(F32))
        be = jnp.minimum(jnp.sum(jnp.where(closed, 1.0, 0.0), axis=0, keepdims=True), N_EXPERTS - 1.0)
        total = jnp.sum(jnp.where(lane8[0:1, :] == N_EXPERTS - 1, ends_row, 0.0), axis=-1, keepdims=True)
        n_used = (total.astype(I32) >> shift).astype(F32)
        meta = jnp.where(b_i[0:1, :] < n_blocks, be, n_used).astype(I32)
        meta_ref[...] = jnp.broadcast_to(meta, meta_ref.shape)


def _moe_dest(te128, pos128, cnt, n_blocks):
    n = te128.shape[0]
    tm = ROW_TILE
    nl = -(-(n_blocks + 1) // LANES) * LANES
    row = lambda i: (i, 0)
    const = lambda i: (0, 0)
    return pl.pallas_call(
        functools.partial(_moe_dest_kernel, n_blocks=n_blocks),
        grid=(n // tm,),
        in_specs=[pl.BlockSpec((tm, LANES), row),
                  pl.BlockSpec((tm, LANES), row),
                  pl.BlockSpec((SUBLANES, LANES), const)],
        out_specs=[pl.BlockSpec((tm // DEST_GROUP, LANES), row),
                   pl.BlockSpec((SUBLANES, nl), const)],
        out_shape=[jax.ShapeDtypeStruct((n // DEST_GROUP, LANES), I32),
                   jax.ShapeDtypeStruct((SUBLANES, nl), I32)],
        compiler_params=_cparams(("arbitrary",)),
        name="moe_dest",
    )(te128, pos128, cnt)


def _row_token_kernel(dest_ref, rt_ref, *, n_assign):
    def clear(r, c):
        rt_ref[r] = 0
        return c

    def place(t, c):
        for k in range(TOP_K):
            rt_ref[dest_ref[t * TOP_K + k]] = t
        return c

    lax.fori_loop(0, rt_ref.shape[0], clear, 0, unroll=8)
    lax.fori_loop(0, n_assign // TOP_K, place, 0, unroll=4)


def _row_token(dest, rows_total):
    return pl.pallas_call(
        functools.partial(_row_token_kernel, n_assign=dest.shape[0]),
        grid_spec=pltpu.PrefetchScalarGridSpec(
            num_scalar_prefetch=1,
            grid=(1,),
            in_specs=[],
            out_specs=pl.BlockSpec(memory_space=pltpu.SMEM)),
        out_shape=jax.ShapeDtypeStruct((rows_total,), I32),
        compiler_params=_cparams(("arbitrary",)),
        name="moe_row_token",
    )(dest)


def _new_expert(be_ref, b):
    return (b == 0) | (be_ref[b] != be_ref[jnp.maximum(b - 1, 0)])


def _swiglu_pairs(v):
    g = jnp.minimum(v, SWIGLU_LIMIT)
    glu = g * jax.nn.sigmoid(g * SWIGLU_ALPHA)
    up1 = jnp.clip(v, -SWIGLU_LIMIT, SWIGLU_LIMIT) + 1.0
    return glu, up1


def _stream_expert_weights(be_ref, nb_ref, run_ctr, copies, consume):
    j = pl.program_id(0)
    b = pl.program_id(1)
    nj = pl.num_programs(0)
    nb = nb_ref[0]
    last_blk = be_ref.shape[0] - 1
    e = be_ref[b]

    @pl.when((j == 0) & (b == 0))
    def _():
        run_ctr[0] = 0
        for c in copies(0, e, 0):
            c.start()

    @pl.when((b < nb) & _new_expert(be_ref, b))
    def _():
        k = run_ctr[0]
        slot = k % 2
        for c in copies(j, e, slot):
            c.wait()
        consume(slot)
        run_end = lax.while_loop(lambda bb: (bb < nb) & (be_ref[jnp.minimum(bb, last_blk)] == e),
                                 lambda bb: bb + 1, b + 1)
        more_runs = run_end < nb
        j_next = jnp.where(more_runs, j, j + 1)
        e_next = jnp.where(more_runs, be_ref[jnp.minimum(run_end, last_blk)], be_ref[0])

        @pl.when(more_runs | (j + 1 < nj))
        def _():
            for c in copies(j_next, e_next, 1 - slot):
                c.start()
        run_ctr[0] = k + 1


def _moe_gu_kernel(be_ref, nb_ref, rt_ref, h_ref, w_ref, ba_ref, bb_ref, o_ref, wa_s, wb_s, wbuf, sem, run_ctr,
                   xf, xbf, xsem, *, layer, tn, nj):
    j = pl.program_id(0)
    b = pl.program_id(1)
    nb = nb_ref[0]
    active = b < nb
    tm = xbf.shape[0]
    slot = (j * nb + b) % 2

    def copies(jj, e, s):
        return [pltpu.make_async_copy(
            w_ref.at[layer, e, :, pl.ds(pl.multiple_of((jj + h * nj) * tn, tn), tn)], wbuf.at[s, h], sem.at[s])
            for h in range(2)]

    def consume(s):
        wa_s[...] = wbuf[s, 0].astype(BF16)
        wb_s[...] = wbuf[s, 1].astype(BF16)

    def gather(blk, s):
        base = blk * tm
        for r in range(tm):
            pltpu.make_async_copy(h_ref.at[pl.ds(rt_ref[base + r], 1), :], xf.at[s, pl.ds(r, 1), :],
                                  xsem.at[s]).start()

    def gather_wait(s):
        pltpu.make_async_copy(h_ref.at[pl.ds(0, tm), :], xf.at[s], xsem.at[s]).wait()

    @pl.when((j == 0) & (b == 0))
    def _():
        gather(0, 0)

    _stream_expert_weights(be_ref, nb_ref, run_ctr, copies, consume)

    @pl.when(active)
    def _():
        gather_wait(slot)
        xbf[...] = xf[slot].astype(BF16)
        gather(jnp.where(b + 1 < nb, b + 1, 0), 1 - slot)
        x = xbf[...]
        ga = jnp.dot(x, wa_s[...], preferred_element_type=F32) + ba_ref[0]
        gb = jnp.dot(x, wb_s[...], preferred_element_type=F32) + bb_ref[0]
        even = lax.broadcasted_iota(I32, (tm, LANES), 1) % 2 == 0
        for c in range(tn // LANES):
            glu_a, up_a = _swiglu_pairs(ga[:, c * LANES:(c + 1) * LANES])
            glu_b, up_b = _swiglu_pairs(gb[:, c * LANES:(c + 1) * LANES])
            ra = glu_a * pltpu.roll(up_a, LANES - 1, 1)
            rb = pltpu.roll(glu_b, 1, 1) * up_b
            o_ref[:, c * LANES:(c + 1) * LANES] = jnp.where(even, ra, rb).astype(BF16)

        @pl.when((j == nj - 1) & (b == nb - 1))
        def _():
            gather_wait(1 - slot)

    @pl.when(b >= nb_ref[0])
    def _():
        o_ref[...] = jnp.zeros(o_ref.shape, o_ref.dtype)


def _moe_gu(block_e, n_blocks_used, row_tok, h, wgu_all, layer, bgu):
    d = h.shape[1]
    rows = row_tok.shape[0]
    ne, f2 = bgu.shape
    tm = MOE_TM
    tn = 1024
    nj = f2 // 2 // tn
    return pl.pallas_call(
        functools.partial(_moe_gu_kernel, layer=layer, tn=tn, nj=nj),
        grid_spec=pltpu.PrefetchScalarGridSpec(
            num_scalar_prefetch=3,
            grid=(nj, rows // tm),
            in_specs=[pl.BlockSpec(memory_space=pl.ANY),
                      pl.BlockSpec(memory_space=pl.ANY),
                      pl.BlockSpec((1, 1, tn), lambda j, b, be, nb, rt: (be[b], 0, j)),
                      pl.BlockSpec((1, 1, tn), lambda j, b, be, nb, rt: (be[b], 0, j + nj))],
            out_specs=pl.BlockSpec((tm, tn), lambda j, b, be, nb, rt: (b, j)),
            scratch_shapes=[pltpu.VMEM((d, tn), BF16), pltpu.VMEM((d, tn), BF16),
                            pltpu.VMEM((2, 2, d, tn), F32), pltpu.SemaphoreType.DMA((2,)),
                            pltpu.SMEM((1,), I32),
                            pltpu.VMEM((2, tm, d), F32), pltpu.VMEM((tm, d), BF16),
                            pltpu.SemaphoreType.DMA((2,))]),
        out_shape=jax.ShapeDtypeStruct((rows, f2 // 2), BF16),
        compiler_params=_cparams(("arbitrary", "arbitrary")),
        name="moe_gate_up",
    )(block_e, n_blocks_used, row_tok, h, wgu_all, bgu.reshape(ne, 1, f2), bgu.reshape(ne, 1, f2))


def _moe_down_kernel(be_ref, nb_ref, a_ref, w_ref, bd_ref, o_ref, wp_s, stage, wbuf, sem, run_ctr, *, layer, tn):
    b = pl.program_id(1)
    active = b < nb_ref[0]

    def copies(j, e, slot):
        return [pltpu.make_async_copy(w_ref.at[layer, e, :, pl.ds(pl.multiple_of(j * tn, tn), tn)], wbuf.at[slot],
                                      sem.at[slot])]

    def consume(slot):
        f = wbuf.shape[1]
        for c in range(tn // LANES):
            cols = slice(c * LANES, (c + 1) * LANES)
            stage[c, pl.ds(0, f // 2, stride=2), :] = wbuf[slot, :f // 2, cols]
            stage[c, pl.ds(1, f // 2, stride=2), :] = wbuf[slot, f // 2:, cols]
            wp_s[:, cols] = stage[c].astype(BF16)

    _stream_expert_weights(be_ref, nb_ref, run_ctr, copies, consume)

    @pl.when(active)
    def _():
        o_ref[...] = jnp.dot(a_ref[...], wp_s[...], preferred_element_type=F32) + bd_ref[0]

    @pl.when(b >= nb_ref[0])
    def _():
        o_ref[...] = jnp.zeros(o_ref.shape, o_ref.dtype)


def _moe_down(block_e, n_blocks_used, act, wd_all, layer, bd):
    rows, f = act.shape
    ne, d = bd.shape
    tm = MOE_TM
    tn = 1024
    return pl.pallas_call(
        functools.partial(_moe_down_kernel, layer=layer, tn=tn),
        grid_spec=pltpu.PrefetchScalarGridSpec(
            num_scalar_prefetch=2,
            grid=(d // tn, rows // tm),
            in_specs=[pl.BlockSpec((tm, f), lambda j, b, be, nb: (b, 0)),
                      pl.BlockSpec(memory_space=pl.ANY),
                      pl.BlockSpec((1, 1, tn), lambda j, b, be, nb: (be[b], 0, j))],
            out_specs=pl.BlockSpec((tm, tn), lambda j, b, be, nb: (b, j)),
            scratch_shapes=[pltpu.VMEM((f, tn), BF16), pltpu.VMEM((tn // LANES, f, LANES), F32),
                            pltpu.VMEM((2, f, tn), F32), pltpu.SemaphoreType.DMA((2,)), pltpu.SMEM((1,), I32)]),
        out_shape=jax.ShapeDtypeStruct((rows, d), F32),
        compiler_params=_cparams(("arbitrary", "arbitrary")),
        name="moe_down",
    )(block_e, n_blocks_used, act, wd_all, bd.reshape(ne, 1, d))


def _combine_kernel(dest_ref, y_ref, x1_ref, gm_ref, tg_ref, fg_ref, o_ref, buf, sem, *, tn, tile0, final_norm):
    i = pl.program_id(0) + tile0

    def issue(t, c):
        for k in range(TOP_K):
            d = dest_ref[(i * tn + t) * TOP_K + k]
            pltpu.make_async_copy(y_ref.at[pl.ds(d, 1), :], buf.at[k, pl.ds(t, 1), :], sem).start()
        return c

    lax.fori_loop(0, tn, issue, 0, unroll=2)
    for k in range(TOP_K):
        pltpu.make_async_copy(y_ref.at[pl.ds(0, tn), :], buf.at[k], sem).wait()
    tg = tg_ref[...]
    moe = tg[:, 0:1] * buf[0]
    for k in range(1, TOP_K):
        moe = moe + tg[:, k:k + 1] * buf[k]
    x2 = _gated_residual(x1_ref[...], gm_ref[...], moe)
    if final_norm:
        x2 = _rms(x2, fg_ref[...])
    o_ref[...] = x2


def _combine(dest, y, x1, modg, tgates, fg, group, final_norm, row0, rows):
    d = x1.shape[1]
    tn = ROW_TILE
    ng = tn // group
    t0 = row0 // tn
    return pl.pallas_call(
        functools.partial(_combine_kernel, tn=tn, tile0=t0, final_norm=final_norm),
        grid_spec=pltpu.PrefetchScalarGridSpec(
            num_scalar_prefetch=1,
            grid=(rows // tn,),
            in_specs=[pl.BlockSpec(memory_space=pl.ANY),
                      pl.BlockSpec((tn, d), lambda i, ds: (i + t0, 0)),
                      pl.BlockSpec((ng, d), lambda i, ds: (i + t0, 5)),
                      pl.BlockSpec((tn, LANES), lambda i, ds: (i + t0, 0)),
                      pl.BlockSpec((1, d), lambda i, ds: (0, 0))],
            out_specs=pl.BlockSpec((tn, d), lambda i, ds: (i, 0)),
            scratch_shapes=[pltpu.VMEM((TOP_K, tn, d), F32), pltpu.SemaphoreType.DMA(())]),
        out_shape=jax.ShapeDtypeStruct((rows, d), F32),
        compiler_params=_cparams(("arbitrary",)),
        name="moe_combine",
    )(dest, y, x1, modg, tgates, fg.reshape(1, d))


def _moe(hp, te128, tg128, pos128, cnt, x1, modg, wgu_all, bgu, wd_all, bd, layer, fg, group, splits):
    n = x1.shape[0]
    rows_total = n * TOP_K + N_EXPERTS * MOE_TM
    n_blocks = rows_total // MOE_TM
    dest2d, meta = _moe_dest(te128, pos128, cnt, n_blocks)
    dest = dest2d.reshape(-1)
    block_e = meta[0, :n_blocks]
    nbu = meta[0, n_blocks:n_blocks + 1]
    row_tok = _row_token(dest, rows_total)
    act = _moe_gu(block_e, nbu, row_tok, hp, wgu_all, layer, bgu)
    y = _moe_down(block_e, nbu, act, wd_all, layer, bd)
    return [_combine(dest, y, x1, modg, tg128, fg, group, fn, r0, rows) for r0, rows, fn in splits]


def _lru_in_kernel(x_ref, g1_ref, sh_ref, sc_ref, wy_ref, wx_ref, y_ref, xb_ref):
    h = _norm_mod(x_ref[...], g1_ref[...], sh_ref[...], sc_ref[...]).astype(BF16)
    y = jnp.dot(h, wy_ref[...], preferred_element_type=F32)
    y_ref[...] = jax.nn.gelu(y, approximate=True).astype(BF16)
    xb_ref[...] = jnp.dot(h, wx_ref[...], preferred_element_type=F32)


def _lru_in(x, g1, modg, wy, wx, group):
    n, d = x.shape
    dr = wy.shape[1]
    tm = ROW_TILE
    ng = tm // group
    row = lambda i: (i, 0)
    const = lambda i: (0, 0)
    return pl.pallas_call(
        _lru_in_kernel,
        grid=(n // tm,),
        in_specs=[pl.BlockSpec((tm, d), row),
                  pl.BlockSpec((1, d), const),
                  pl.BlockSpec((ng, d), lambda i: (i, 0)),
                  pl.BlockSpec((ng, d), lambda i: (i, 1)),
                  pl.BlockSpec((d, dr), const),
                  pl.BlockSpec((d, dr), const)],
        out_specs=[pl.BlockSpec((tm, dr), row), pl.BlockSpec((tm, dr), row)],
        out_shape=[jax.ShapeDtypeStruct((n, dr), BF16), jax.ShapeDtypeStruct((n, dr), F32)],
        compiler_params=_cparams(("arbitrary",)),
        name="lru_in",
    )(x, g1.reshape(1, d), modg, modg, wy, wx)


def _lru_scan_kernel(y_ref, xb_ref, cb_ref, h0_ref, cw_ref, cbias_ref, wa_ref, ba_ref, wx_ref, bx_ref, lam_ref,
                     hy_ref, cbo_ref, ho_ref, xe, a_s, u_s, hc, *, tc, starts_at_pos0):
    c = pl.program_id(1)
    dr = xb_ref.shape[1]
    nb = wa_ref.shape[0]
    bd = dr // nb
    pre = SUBLANES

    @pl.when(c == 0)
    def _():
        xe[0:pre, :] = jnp.zeros((pre, dr), F32)
        xe[pre - (CONV_W - 1):pre, :] = cb_ref[0]
        hc[...] = jnp.broadcast_to(h0_ref[0], (SUBLANES, dr))

    xe[pre:pre + tc, :] = xb_ref[...]
    cw = cw_ref[...]
    xc = cbias_ref[...] + xe[pre:pre + tc, :] * cw[CONV_W - 1:CONV_W, :]
    for k in range(1, CONV_W):
        xc = xc + xe[pre - k:pre - k + tc, :] * cw[CONV_W - 1 - k:CONV_W - k, :]
    cbo_ref[0] = xe[pre + tc - (CONV_W - 1):pre + tc, :]
    xe[0:pre, :] = xe[tc:tc + pre, :]

    xcb = xc.astype(BF16)
    ra = jnp.concatenate([jnp.dot(xcb[:, n * bd:(n + 1) * bd], wa_ref[n], preferred_element_type=F32)
                          for n in range(nb)], axis=1)
    rx = jnp.concatenate([jnp.dot(xcb[:, n * bd:(n + 1) * bd], wx_ref[n], preferred_element_type=F32)
                          for n in range(nb)], axis=1)
    r = jax.nn.sigmoid(ra + ba_ref[...])
    ig = jax.nn.sigmoid(rx + bx_ref[...])
    lam = lam_ref[...]
    log_sig = jnp.minimum(lam, 0.0) - jnp.log1p(jnp.exp(-jnp.abs(lam)))
    log_a = LRU_C * r * log_sig
    a = jnp.exp(log_a)
    th = jnp.tanh(log_a)
    mult = jnp.sqrt(-2.0 * th / (1.0 - th))
    if starts_at_pos0:
        first = (lax.broadcasted_iota(I32, (tc, 1), 0) == 0) & (c == 0)
        mult = jnp.where(first, 1.0, mult)
    a_s[...] = a
    u_s[...] = mult * ig * xc

    row8 = lax.broadcasted_iota(I32, (SUBLANES, dr), 0)

    def group_step(g, hprev):
        off = pl.multiple_of(g * SUBLANES, SUBLANES)
        aa = a_s[pl.ds(off, SUBLANES), :]
        uu = u_s[pl.ds(off, SUBLANES), :]
        s = 1
        while s < SUBLANES:
            m = row8 >= s
            uu = jnp.where(m, aa * pltpu.roll(uu, s, 0) + uu, uu)
            aa = jnp.where(m, aa * pltpu.roll(aa, s, 0), aa)
            s *= 2
        hh = aa * hprev + uu
        u_s[pl.ds(off, SUBLANES), :] = hh
        return jnp.broadcast_to(hh[SUBLANES - 1:SUBLANES, :], (SUBLANES, dr))

    hlast = lax.fori_loop(0, tc // SUBLANES, group_step, hc[...])
    hc[...] = hlast
    ho_ref[0] = hlast[0:1, :]
    hy_ref[...] = (u_s[...] * y_ref[...].astype(F32)).astype(BF16)


def _lru_scan(yb, xb, conv_buf, h0, cw, cbias, wa, ba, wx, bx, lam, row0, n_seq, t, tc, starts_at_pos0):
    dr = xb.shape[1]
    nc = t // tc
    rb0 = row0 // tc
    inmap = lambda s, c: (rb0 + s * nc + c, 0)
    outmap = lambda s, c: (s * nc + c, 0)
    const2 = lambda s, c: (0, 0)
    const3 = lambda s, c: (0, 0, 0)
    seq3 = lambda s, c: (s, 0, 0)
    return pl.pallas_call(
        functools.partial(_lru_scan_kernel, tc=tc, starts_at_pos0=starts_at_pos0),
        grid=(n_seq, nc),
        in_specs=[pl.BlockSpec((tc, dr), inmap),
                  pl.BlockSpec((tc, dr), inmap),
                  pl.BlockSpec((1, CONV_W - 1, dr), seq3),
                  pl.BlockSpec((1, 1, dr), seq3),
                  pl.BlockSpec((CONV_W, dr), const2),
                  pl.BlockSpec((1, dr), const2),
                  pl.BlockSpec(wa.shape, const3),
                  pl.BlockSpec((1, dr), const2),
                  pl.BlockSpec(wx.shape, const3),
                  pl.BlockSpec((1, dr), const2),
                  pl.BlockSpec((1, dr), const2)],
        out_specs=[pl.BlockSpec((tc, dr), outmap),
                   pl.BlockSpec((1, CONV_W - 1, dr), seq3),
                   pl.BlockSpec((1, 1, dr), seq3)],
        out_shape=[jax.ShapeDtypeStruct((n_seq * t, dr), BF16),
                   jax.ShapeDtypeStruct((n_seq, CONV_W - 1, dr), F32),
                   jax.ShapeDtypeStruct((n_seq, 1, dr), F32)],
        scratch_shapes=[pltpu.VMEM((SUBLANES + tc, dr), F32),
                        pltpu.VMEM((tc, dr), F32),
                        pltpu.VMEM((tc, dr), F32),
                        pltpu.VMEM((SUBLANES, dr), F32)],
        compiler_params=_cparams(("arbitrary", "arbitrary")),
        name="lru_scan",
    )(yb, xb, conv_buf, h0.reshape(n_seq, 1, dr), cw, cbias.reshape(1, dr), wa, ba.reshape(1, dr),
      wx, bx.reshape(1, dr), lam.reshape(1, dr))


def _rope_tables(pos):
    half = ROPE_DIM // 2
    inv = 1.0 / (ROPE_THETA ** (jnp.arange(0, ROPE_DIM, 2, dtype=F32) / ROPE_DIM))
    ang = pos.astype(F32)[:, None] * inv[None, :]
    cos, sin = jnp.cos(ang), jnp.sin(ang)
    z = jnp.zeros((pos.shape[0], LANES - ROPE_DIM), F32)
    return jnp.concatenate([cos, cos, z], axis=1), jnp.concatenate([-sin, sin, z], axis=1)


def _head_slab_weights(w_uq):
    ql = w_uq.shape[0]
    w = w_uq.reshape(ql, N_HEADS, NOPE_DIM + ROPE_DIM)
    z = jnp.zeros((ql, N_HEADS, HEAD_W - NOPE_DIM - ROPE_DIM), w.dtype)
    return jnp.concatenate([w, z], axis=2).reshape(ql, N_HEADS * HEAD_W).astype(BF16)


def kernel(x_prompt, x_sample, cache_ckv, cache_krope, state_conv, state_h, c_prompt, c_sample,
           mod_w, mod_b, norm1_g, norm2_g,
           mla_w_in, mla_q_norm_g, mla_kv_norm_g, mla_w_uq, mla_w_ukv, mla_w_o,
           lru_w_in, lru_conv_w, lru_conv_b, lru_w_a, lru_b_a, lru_w_x, lru_b_x, lru_lambda, lru_w_o,
           router_w, router_b, moe_w_gu, moe_b_gu, moe_w_down, moe_b_down, final_g):
    bp, tp, d = x_prompt.shape
    bs, ts, _ = x_sample.shape
    past = cache_ckv.shape[2]
    depth = mod_w.shape[0]
    n_p, n_s = bp * tp, bs * ts
    n = n_p + n_s
    group = math.gcd(tp, ts)
    assert group % SUBLANES == 0 and ROW_TILE % group == 0 and n_p % ROW_TILE == 0 and n_s % ROW_TILE == 0
    assert NOPE_DIM == LANES and V_DIM == LANES and ROPE_DIM <= LANES and n_p % n_s == 0

    x = jnp.concatenate([x_prompt.reshape(n_p, d), x_sample.reshape(n_s, d)], axis=0)
    nb = bp + bs
    nb_pad = -(-nb // SUBLANES) * SUBLANES
    c_all = jnp.concatenate([c_prompt, c_sample, jnp.zeros((nb_pad - nb, d), F32)], axis=0)
    grp_batch = np.concatenate([np.repeat(np.arange(bp), tp // group), bp + np.repeat(np.arange(bs), ts // group)])
    pos = jnp.concatenate([jnp.tile(jnp.arange(tp), bp), jnp.tile(past + jnp.arange(ts), bs)])
    cos, sin = _rope_tables(pos)

    outs = {k: [] for k in ("ckv_p", "kr_p", "conv_p", "h_p", "ckv_s", "kr_s", "conv_s", "h_s")}
    for i in range(depth):
        mod = _adaln(c_all, mod_w, i, mod_b[i])
        modg = jnp.take(mod, jnp.asarray(grp_batch), axis=0)
        j = i // 2
        if i % 2 == 0:
            w_in = mla_w_in[j]
            zpad = jnp.zeros((d, LANES - ROPE_DIM), F32)
            win = jnp.concatenate([w_in, zpad], axis=1).astype(BF16)
            wuq = _head_slab_weights(mla_w_uq[j])
            wukv = mla_w_ukv[j].reshape(KV_LORA, N_HEADS, NOPE_DIM + V_DIM)
            wk = wukv[:, :, :NOPE_DIM].reshape(KV_LORA, N_HEADS * NOPE_DIM).astype(BF16)
            wv = wukv[:, :, NOPE_DIM:].reshape(KV_LORA, N_HEADS * V_DIM).astype(BF16)
            q, ckv, kr, ckvb, krp = _mla_proj(x, norm1_g[i], modg, win, mla_q_norm_g[j], mla_kv_norm_g[j],
                                              wuq, cos, sin, group)
            kk, vv = _kv_expand(ckvb, krp, wk, wv, n_p)
            o_p = _attn_prompt(q, kk, vv, bp, tp)
            qa = _absorb(q, wk, n_p, n_s)
            ckr_pad = jnp.pad(cache_krope[j], ((0, 0), (0, 0), (0, LANES - ROPE_DIM)))
            o_lat = _attn_sample(qa, q, cache_ckv[j], ckr_pad, ckvb, krp, n_p, bs, ts, past)
            o_s = _unabsorb(o_lat, wv, n_s)
            wo = mla_w_o[j].astype(BF16)
            outs["ckv_p"].append(ckv[:n_p].reshape(bp, tp, KV_LORA))
            outs["kr_p"].append(kr[:n_p].reshape(bp, tp, ROPE_DIM))
            outs["ckv_s"].append(ckv[n_p:].reshape(bs, ts, KV_LORA))
            outs["kr_s"].append(kr[n_p:].reshape(bs, ts, ROPE_DIM))
        else:
            dr = lru_w_in.shape[2] // 2
            wy = lru_w_in[j][:, :dr].astype(BF16)
            wx = lru_w_in[j][:, dr:].astype(BF16)
            yb, xb = _lru_in(x, norm1_g[i], modg, wy, wx, group)
            wa = lru_w_a[j].astype(BF16)
            wxg = lru_w_x[j].astype(BF16)
            lru_args = (lru_conv_w[j], lru_conv_b[j], wa, lru_b_a[j], wxg, lru_b_x[j], lru_lambda[j])
            zbuf = jnp.zeros((bp, CONV_W - 1, dr), F32)
            zh = jnp.zeros((bp, dr), F32)
            o_p, cb_p, h_p = _lru_scan(yb, xb, zbuf, zh, *lru_args, 0, bp, tp, ROW_TILE, True)
            o_s, cb_s, h_s = _lru_scan(yb, xb, state_conv[j], state_h[j], *lru_args, n_p, bs, ts, ts, False)
            wo = lru_w_o[j].astype(BF16)
            outs["conv_p"].append(cb_p)
            outs["h_p"].append(h_p.reshape(bp, dr))
            outs["conv_s"].append(cb_s)
            outs["h_s"].append(h_s.reshape(bs, dr))
        x1, hp, te128, tg128, pos128, cnt = _post_mixer(o_p, o_s, wo, x, norm2_g[i], modg, router_w[i], router_b[i],
                                                        group)
        last = i == depth - 1
        splits = [(0, n_p, True), (n_p, n_s, True)] if last else [(0, n, False)]
        res = _moe(hp, te128, tg128, pos128, cnt, x1, modg, moe_w_gu, moe_b_gu[i], moe_w_down, moe_b_down[i], i,
                   final_g, group, splits)
        x = res[0]
    y_prompt = res[0].reshape(bp, tp, d)
    y_sample = res[1].reshape(bs, ts, d)
    return (y_prompt, y_sample,
            jnp.stack(outs["ckv_p"]), jnp.stack(outs["kr_p"]), jnp.stack(outs["conv_p"]), jnp.stack(outs["h_p"]),
            jnp.stack(outs["ckv_s"]), jnp.stack(outs["kr_s"]), jnp.stack(outs["conv_s"]), jnp.stack(outs["h_s"]))
```

```python
import functools
import math

import jax
import jax.numpy as jnp
import numpy as np
from jax import lax
from jax.experimental import pallas as pl
from jax.experimental.pallas import tpu as pltpu

F32 = jnp.float32
BF16 = jnp.bfloat16
I32 = jnp.int32

CHUNK = 64
N_HEADS = 16
Q_LORA = 512
KV_LORA = 512
NOPE_DIM = 128
ROPE_DIM = 64
V_DIM = 128
ROPE_THETA = 10000.0
LRU_BLOCKS = 8
CONV_W = 4
LRU_C = 8.0
N_EXPERTS = 32
TOP_K = 4
SWIGLU_LIMIT = 7.0
SWIGLU_ALPHA = 1.702
N_MOD = 6
EPS = 1e-6

LANES = 128
SUBLANES = 8
HEAD_W = 2 * LANES

ROW_TILE = 256
MOE_TM = 256
VMEM_LIMIT = 56 * 1024 * 1024


def _cparams(sem):
    return pltpu.CompilerParams(dimension_semantics=sem, vmem_limit_bytes=VMEM_LIMIT)


def _rms(x, g):
    ms = jnp.mean(x * x, axis=-1, keepdims=True)
    return x * lax.rsqrt(ms + EPS) * g


def _norm_mod(x, g, shift, scale):
    tm, d = x.shape
    ng = shift.shape[0]
    y = _rms(x, g).reshape(ng, tm // ng, d)
    return (y * (1.0 + scale[:, None, :]) + shift[:, None, :]).reshape(tm, d)


def _gated_residual(x, gate, y):
    tm, d = x.shape
    ng = gate.shape[0]
    return (x.reshape(ng, tm // ng, d) + gate[:, None, :] * y.reshape(ng, tm // ng, d)).reshape(tm, d)


def _adaln_kernel(c_ref, w_ref, b_ref, o_ref):
    c = c_ref[...]
    a = (c * jax.nn.sigmoid(c)).astype(BF16)
    o_ref[...] = jnp.dot(a, w_ref[0].astype(BF16), preferred_element_type=F32) + b_ref[...]


def _adaln(c_all, w_all, layer, b):
    bp, d = c_all.shape
    n = w_all.shape[2]
    tn = 1024
    return pl.pallas_call(
        _adaln_kernel,
        grid=(n // tn,),
        in_specs=[pl.BlockSpec((bp, d), lambda j: (0, 0)),
                  pl.BlockSpec((1, d, tn), lambda j: (layer, 0, j)),
                  pl.BlockSpec((1, tn), lambda j: (0, j))],
        out_specs=pl.BlockSpec((bp, tn), lambda j: (0, j)),
        out_shape=jax.ShapeDtypeStruct((bp, n), F32),
        compiler_params=_cparams(("arbitrary",)),
        name="adaln",
    )(c_all, w_all, b.reshape(1, n))


def _rope128(v, cos, sin):
    half = ROPE_DIM // 2
    lane = lax.broadcasted_iota(I32, v.shape, 1)
    sw = jnp.where(lane < half, pltpu.roll(v, LANES - half, 1), pltpu.roll(v, half, 1))
    return v * cos + sw * sin


def _mla_proj_kernel(x_ref, g1_ref, sh_ref, sc_ref, win_ref, qg_ref, kvg_ref, wuq_ref, cos_ref, sin_ref,
                     q_ref, ckv_ref, kr_ref, ckvb_ref, krp_ref):
    h = _norm_mod(x_ref[...], g1_ref[...], sh_ref[...], sc_ref[...]).astype(BF16)
    lat = jnp.dot(h, win_ref[...], preferred_element_type=F32)
    q_lat = lat[:, :Q_LORA]
    c_kv = lat[:, Q_LORA:Q_LORA + KV_LORA]
    k_r = lat[:, Q_LORA + KV_LORA:]
    qn = _rms(q_lat, qg_ref[...]).astype(BF16)
    q = jnp.dot(qn, wuq_ref[...], preferred_element_type=F32)
    cos = cos_ref[...]
    sin = sin_ref[...]
    scale = (NOPE_DIM + ROPE_DIM) ** -0.5
    for hh in range(N_HEADS):
        lo = hh * HEAD_W
        q_ref[:, lo:lo + LANES] = (q[:, lo:lo + LANES] * scale).astype(BF16)
        q_ref[:, lo + LANES:lo + HEAD_W] = (_rope128(q[:, lo + LANES:lo + HEAD_W], cos, sin) * scale).astype(BF16)
    ckv = _rms(c_kv, kvg_ref[...])
    ckv_ref[...] = ckv
    ckvb_ref[...] = ckv.astype(BF16)
    kr = _rope128(k_r, cos, sin)
    kr_ref[...] = kr[:, :ROPE_DIM]
    krp_ref[...] = kr.astype(BF16)


def _mla_proj(x, g1, modg, win, qg, kvg, wuq, cos, sin, group):
    n, d = x.shape
    tm = ROW_TILE
    ng = tm // group
    wl = win.shape[1]
    qw = wuq.shape[1]
    row = lambda i: (i, 0)
    const = lambda i: (0, 0)
    return pl.pallas_call(
        _mla_proj_kernel,
        grid=(n // tm,),
        in_specs=[pl.BlockSpec((tm, d), row),
                  pl.BlockSpec((1, d), const),
                  pl.BlockSpec((ng, d), lambda i: (i, 0)),
                  pl.BlockSpec((ng, d), lambda i: (i, 1)),
                  pl.BlockSpec((d, wl), const),
                  pl.BlockSpec((1, Q_LORA), const),
                  pl.BlockSpec((1, KV_LORA), const),
                  pl.BlockSpec((Q_LORA, qw), const),
                  pl.BlockSpec((tm, LANES), row),
                  pl.BlockSpec((tm, LANES), row)],
        out_specs=[pl.BlockSpec((tm, qw), row),
                   pl.BlockSpec((tm, KV_LORA), row),
                   pl.BlockSpec((tm, ROPE_DIM), row),
                   pl.BlockSpec((tm, KV_LORA), row),
                   pl.BlockSpec((tm, LANES), row)],
        out_shape=[jax.ShapeDtypeStruct((n, qw), BF16),
                   jax.ShapeDtypeStruct((n, KV_LORA), F32),
                   jax.ShapeDtypeStruct((n, ROPE_DIM), F32),
                   jax.ShapeDtypeStruct((n, KV_LORA), BF16),
                   jax.ShapeDtypeStruct((n, LANES), BF16)],
        compiler_params=_cparams(("arbitrary",)),
        name="mla_proj",
    )(x, g1.reshape(1, d), modg, modg, win, qg.reshape(1, -1), kvg.reshape(1, -1), wuq, cos, sin)


def _kv_expand_kernel(c_ref, krp_ref, wk_ref, wv_ref, k_ref, v_ref):
    c = c_ref[...]
    kn = jnp.dot(c, wk_ref[...], preferred_element_type=F32).astype(BF16)
    krp = krp_ref[...]
    for hh in range(N_HEADS):
        k_ref[:, hh * HEAD_W:hh * HEAD_W + LANES] = kn[:, hh * NOPE_DIM:(hh + 1) * NOPE_DIM]
        k_ref[:, hh * HEAD_W + LANES:(hh + 1) * HEAD_W] = krp
    v_ref[...] = jnp.dot(c, wv_ref[...], preferred_element_type=F32).astype(BF16)


def _kv_expand(ckvb, krp, wk, wv, rows):
    tm = 512
    row = lambda i: (i, 0)
    const = lambda i: (0, 0)
    return pl.pallas_call(
        _kv_expand_kernel,
        grid=(rows // tm,),
        in_specs=[pl.BlockSpec((tm, KV_LORA), row),
                  pl.BlockSpec((tm, LANES), row),
                  pl.BlockSpec(wk.shape, const),
                  pl.BlockSpec(wv.shape, const)],
        out_specs=[pl.BlockSpec((tm, N_HEADS * HEAD_W), row),
                   pl.BlockSpec((tm, N_HEADS * V_DIM), row)],
        out_shape=[jax.ShapeDtypeStruct((rows, N_HEADS * HEAD_W), BF16),
                   jax.ShapeDtypeStruct((rows, N_HEADS * V_DIM), BF16)],
        compiler_params=_cparams(("arbitrary",)),
        name="kv_expand",
    )(ckvb, krp, wk, wv)


ATTN_HEADS_PER_STEP = 2


def _attn_prompt_kernel(q_ref, k_ref, v_ref, o_ref, *, tq, nq):
    qi = pl.program_id(2)
    dn = (((1,), (1,)), ((), ()))
    hs = ATTN_HEADS_PER_STEP
    r = lax.broadcasted_iota(I32, (tq, tq), 0) // CHUNK
    c = lax.broadcasted_iota(I32, (tq, tq), 1) // CHUNK
    diag_visible = c <= r
    for qs in range(nq):
        @pl.when(qi == qs)
        def _(qs=qs):
            past = qs * tq
            for h in range(hs):
                q = q_ref[:, h * HEAD_W:(h + 1) * HEAD_W]
                kcols = slice(h * HEAD_W, (h + 1) * HEAD_W)
                vcols = slice(h * V_DIM, (h + 1) * V_DIM)
                s_d = lax.dot_general(q, k_ref[past:past + tq, kcols], dn, preferred_element_type=F32)
                s_d = jnp.where(diag_visible, s_d, -jnp.inf)
                m = jnp.max(s_d, axis=-1, keepdims=True)
                if past:
                    s_f = lax.dot_general(q, k_ref[0:past, kcols], dn, preferred_element_type=F32)
                    m = jnp.maximum(m, jnp.max(s_f, axis=-1, keepdims=True))
                p_d = jnp.exp(s_d - m)
                l = jnp.sum(p_d, axis=-1, keepdims=True)
                acc = jnp.dot(p_d.astype(BF16), v_ref[past:past + tq, vcols], preferred_element_type=F32)
                if past:
                    p_f = jnp.exp(s_f - m)
                    l = l + jnp.sum(p_f, axis=-1, keepdims=True)
                    acc = acc + jnp.dot(p_f.astype(BF16), v_ref[0:past, vcols], preferred_element_type=F32)
                o_ref[:, vcols] = (acc / l).astype(BF16)


def _attn_prompt(q, k, v, bp, tp):
    n_rows = bp * tp
    tq = 256
    nq = tp // tq
    hs = ATTN_HEADS_PER_STEP
    return pl.pallas_call(
        functools.partial(_attn_prompt_kernel, tq=tq, nq=nq),
        grid=(bp, N_HEADS // hs, nq),
        in_specs=[pl.BlockSpec((tq, hs * HEAD_W), lambda b, h, i: (b * nq + i, h)),
                  pl.BlockSpec((tp, hs * HEAD_W), lambda b, h, i: (b, h)),
                  pl.BlockSpec((tp, hs * V_DIM), lambda b, h, i: (b, h))],
        out_specs=pl.BlockSpec((tq, hs * V_DIM), lambda b, h, i: (b * nq + i, h)),
        out_shape=jax.ShapeDtypeStruct((n_rows, N_HEADS * V_DIM), BF16),
        compiler_params=_cparams(("arbitrary", "arbitrary", "arbitrary")),
        name="attn_prompt",
    )(q, k, v)


def _absorb_kernel(q_ref, wk_ref, o_ref):
    dn = (((1,), (1,)), ((), ()))
    o_ref[0] = lax.dot_general(q_ref[...], wk_ref[...], dn, preferred_element_type=F32).astype(BF16)


def _absorb(q, wk, row0, rows):
    rb = row0 // rows
    return pl.pallas_call(
        _absorb_kernel,
        grid=(N_HEADS,),
        in_specs=[pl.BlockSpec((rows, LANES), lambda h: (rb, 2 * h)),
                  pl.BlockSpec((KV_LORA, NOPE_DIM), lambda h: (0, h))],
        out_specs=pl.BlockSpec((1, rows, KV_LORA), lambda h: (h, 0, 0)),
        out_shape=jax.ShapeDtypeStruct((N_HEADS, rows, KV_LORA), BF16),
        compiler_params=_cparams(("arbitrary",)),
        name="absorb",
    )(q, wk)


def _attn_sample_kernel(qa_ref, q_ref, cc_ref, ckr_ref, cn_ref, krn_ref, o_ref, *, ts, past):
    hn = N_HEADS
    qa = qa_ref[...].reshape(hn * ts, KV_LORA)
    qfull = q_ref[...]
    qr = jnp.concatenate([qfull[:, h * HEAD_W + LANES:(h + 1) * HEAD_W] for h in range(hn)], axis=0)
    cc = cc_ref[0].astype(BF16)
    ckr = ckr_ref[0].astype(BF16)
    cn = cn_ref[...]
    krn = krn_ref[...]
    dn = (((1,), (1,)), ((), ()))
    s_c = (lax.dot_general(qa, cc, dn, preferred_element_type=F32)
           + lax.dot_general(qr, ckr, dn, preferred_element_type=F32))
    s_n = (lax.dot_general(qa, cn, dn, preferred_element_type=F32)
           + lax.dot_general(qr, krn, dn, preferred_element_type=F32))
    qchunk_c = (past + lax.broadcasted_iota(I32, s_c.shape, 0) % ts) // CHUNK
    s_c = jnp.where(lax.broadcasted_iota(I32, s_c.shape, 1) // CHUNK <= qchunk_c, s_c, -jnp.inf)
    qchunk_n = (past + lax.broadcasted_iota(I32, s_n.shape, 0) % ts) // CHUNK
    s_n = jnp.where((past + lax.broadcasted_iota(I32, s_n.shape, 1)) // CHUNK <= qchunk_n, s_n, -jnp.inf)
    m = jnp.maximum(jnp.max(s_c, axis=-1, keepdims=True), jnp.max(s_n, axis=-1, keepdims=True))
    p_c = jnp.exp(s_c - m)
    p_n = jnp.exp(s_n - m)
    l = jnp.sum(p_c, axis=-1, keepdims=True) + jnp.sum(p_n, axis=-1, keepdims=True)
    o = (jnp.dot(p_c.astype(BF16), cc, preferred_element_type=F32)
         + jnp.dot(p_n.astype(BF16), cn, preferred_element_type=F32)) / l
    o_ref[...] = o.astype(BF16).reshape(hn, ts, KV_LORA)


def _attn_sample(qa, q, cache_c, cache_kr, ckvb, krp, row0, bs, ts, past):
    rb0 = row0 // ts
    return pl.pallas_call(
        functools.partial(_attn_sample_kernel, ts=ts, past=past),
        grid=(bs,),
        in_specs=[pl.BlockSpec((N_HEADS, ts, KV_LORA), lambda b: (0, b, 0)),
                  pl.BlockSpec((ts, N_HEADS * HEAD_W), lambda b: (rb0 + b, 0)),
                  pl.BlockSpec((1, past, KV_LORA), lambda b: (b, 0, 0)),
                  pl.BlockSpec((1, past, LANES), lambda b: (b, 0, 0)),
                  pl.BlockSpec((ts, KV_LORA), lambda b: (rb0 + b, 0)),
                  pl.BlockSpec((ts, LANES), lambda b: (rb0 + b, 0))],
        out_specs=pl.BlockSpec((N_HEADS, ts, KV_LORA), lambda b: (0, b, 0)),
        out_shape=jax.ShapeDtypeStruct((N_HEADS, bs * ts, KV_LORA), BF16),
        compiler_params=_cparams(("arbitrary",)),
        name="attn_sample",
    )(qa, q, cache_c, cache_kr, ckvb, krp)


def _unabsorb_kernel(ol_ref, wv_ref, o_ref):
    o_ref[...] = jnp.dot(ol_ref[0], wv_ref[...], preferred_element_type=F32).astype(BF16)


def _unabsorb(o_lat, wv, rows):
    return pl.pallas_call(
        _unabsorb_kernel,
        grid=(N_HEADS,),
        in_specs=[pl.BlockSpec((1, rows, KV_LORA), lambda h: (h, 0, 0)),
                  pl.BlockSpec((KV_LORA, V_DIM), lambda h: (0, h))],
        out_specs=pl.BlockSpec((rows, V_DIM), lambda h: (0, h)),
        out_shape=jax.ShapeDtypeStruct((rows, N_HEADS * V_DIM), BF16),
        compiler_params=_cparams(("arbitrary",)),
        name="unabsorb",
    )(o_lat, wv)


def _post_mixer_kernel(op_ref, os_ref, wo_ref, x_ref, gate_ref, g2_ref, sh_ref, sc_ref, rw_ref, rb_ref,
                       x1_ref, hp_ref, te_ref, tg_ref, pos_ref, cnt_ref, carry, *, prompt_tiles):
    @pl.when(pl.program_id(0) == 0)
    def _():
        carry[...] = jnp.zeros(carry.shape, F32)

    o = jnp.where(pl.program_id(0) < prompt_tiles, op_ref[...], os_ref[...])
    y = jnp.dot(o, wo_ref[...], preferred_element_type=F32)
    x1 = _gated_residual(x_ref[...], gate_ref[...], y)
    x1_ref[...] = x1
    h2 = _norm_mod(x1, g2_ref[...], sh_ref[...], sc_ref[...])
    hp_ref[...] = h2
    rw = rw_ref[...]
    h_hi = h2.astype(BF16)
    h_lo = (h2 - h_hi.astype(F32)).astype(BF16)
    w_hi = rw.astype(BF16)
    w_lo = (rw - w_hi.astype(F32)).astype(BF16)
    logits = (jnp.dot(h_hi, w_hi, preferred_element_type=F32) + jnp.dot(h_lo, w_hi, preferred_element_type=F32)
              + jnp.dot(h_hi, w_lo, preferred_element_type=F32) + rb_ref[...])
    tm, ne = logits.shape
    eid = lax.broadcasted_iota(I32, (tm, ne), 1)
    lane = lax.broadcasted_iota(I32, (tm, LANES), 1)
    te = jnp.zeros((tm, LANES), I32)
    tv = jnp.full((tm, LANES), -jnp.inf, F32)
    work = logits
    picks = []
    for k in range(TOP_K):
        mx = jnp.max(work, axis=-1, keepdims=True)
        idx = jnp.min(jnp.where(work == mx, eid, ne), axis=-1, keepdims=True)
        picks.append(idx)
        te = jnp.where(lane == k, idx, te)
        tv = jnp.where(lane == k, mx, tv)
        work = jnp.where(eid == idx, -jnp.inf, work)
    ex = jnp.exp(tv - jnp.max(tv, axis=-1, keepdims=True))
    te_ref[...] = te
    tg_ref[...] = ex / jnp.sum(ex, axis=-1, keepdims=True)
    onehot = jnp.zeros((tm, LANES), F32)
    for idx in picks:
        onehot = onehot + (lane == idx).astype(F32)
    tri = (lax.broadcasted_iota(I32, (tm, tm), 1) < lax.broadcasted_iota(I32, (tm, tm), 0)).astype(BF16)
    rank = jnp.dot(tri, onehot.astype(BF16), preferred_element_type=F32) + carry[0:1, :]
    pos = jnp.zeros((tm, LANES), I32)
    for k, idx in enumerate(picks):
        pk = jnp.sum(jnp.where(lane == idx, rank, 0.0), axis=-1, keepdims=True)
        pos = jnp.where(lane == k, pk.astype(I32), pos)
    pos_ref[...] = pos
    total = carry[0:1, :] + jnp.sum(onehot, axis=0, keepdims=True)
    carry[...] = jnp.broadcast_to(total, carry.shape)
    cnt_ref[...] = carry[...]


def _post_mixer(o_p, o_s, wo, x, g2, modg, rw, rb, group):
    n, d = x.shape
    tm = ROW_TILE
    ng = tm // group
    npt = o_p.shape[0] // tm
    nst = o_s.shape[0] // tm
    row = lambda i: (i, 0)
    const = lambda i: (0, 0)
    return pl.pallas_call(
        functools.partial(_post_mixer_kernel, prompt_tiles=npt),
        grid=(n // tm,),
        in_specs=[pl.BlockSpec((tm, o_p.shape[1]), lambda i: (jnp.minimum(i, npt - 1), 0)),
                  pl.BlockSpec((tm, o_s.shape[1]), lambda i: (jnp.clip(i - npt, 0, nst - 1), 0)),
                  pl.BlockSpec(wo.shape, const),
                  pl.BlockSpec((tm, d), row),
                  pl.BlockSpec((ng, d), lambda i: (i, 2)),
                  pl.BlockSpec((1, d), const),
                  pl.BlockSpec((ng, d), lambda i: (i, 3)),
                  pl.BlockSpec((ng, d), lambda i: (i, 4)),
                  pl.BlockSpec(rw.shape, const),
                  pl.BlockSpec((1, rw.shape[1]), const)],
        out_specs=[pl.BlockSpec((tm, d), row),
                   pl.BlockSpec((tm, d), row),
                   pl.BlockSpec((tm, LANES), row),
                   pl.BlockSpec((tm, LANES), row),
                   pl.BlockSpec((tm, LANES), row),
                   pl.BlockSpec((SUBLANES, LANES), const)],
        out_shape=[jax.ShapeDtypeStruct((n, d), F32),
                   jax.ShapeDtypeStruct((n, d), F32),
                   jax.ShapeDtypeStruct((n, LANES), I32),
                   jax.ShapeDtypeStruct((n, LANES), F32),
                   jax.ShapeDtypeStruct((n, LANES), I32),
                   jax.ShapeDtypeStruct((SUBLANES, LANES), F32)],
        scratch_shapes=[pltpu.VMEM((SUBLANES, LANES), F32)],
        compiler_params=_cparams(("arbitrary",)),
        name="post_mixer",
    )(o_p, o_s, wo, x, modg, g2.reshape(1, d), modg, modg, rw, rb.reshape(1, -1))


DEST_GROUP = LANES // TOP_K


def _moe_dest_kernel(te_ref, pos_ref, cnt_ref, dest_ref, meta_ref, *, n_blocks):
    shift = MOE_TM.bit_length() - 1
    lane8 = lax.broadcasted_iota(I32, (SUBLANES, LANES), 1)
    cnt = cnt_ref[...].astype(I32)
    padded = ((cnt + (MOE_TM - 1)) >> shift) << shift
    ends = padded.astype(F32)
    s = 1
    while s < N_EXPERTS:
        ends = ends + jnp.where(lane8 >= s, pltpu.roll(ends, s, 1), 0.0)
        s *= 2
    ends_row = ends[0:1, :]
    starts_row = (ends - padded.astype(F32))[0:1, :]
    te = te_ref[...]
    pos = pos_ref[...]
    tm = te.shape[0]
    lane = lax.broadcasted_iota(I32, (tm, LANES), 1)
    dest = jnp.zeros((tm, LANES), F32)
    for k in range(TOP_K):
        sk = jnp.sum(jnp.where(lane == te[:, k:k + 1], starts_row, 0.0), axis=-1, keepdims=True)
        dest = jnp.where(lane == k, sk + pos[:, k:k + 1].astype(F32), dest)
    hi = jnp.floor(dest * (1.0 / 256.0))
    lo = dest - 256.0 * hi
    sel = (lax.broadcasted_iota(I32, (LANES, LANES), 0)
           == lax.broadcasted_iota(I32, (LANES, LANES), 1) % TOP_K).astype(BF16)
    spread = (256.0 * jnp.dot(hi.astype(BF16), sel, preferred_element_type=F32)
              + jnp.dot(lo.astype(BF16), sel, preferred_element_type=F32))
    row = lax.broadcasted_iota(I32, (tm, LANES), 0)
    keep = lane // TOP_K == row % DEST_GROUP
    dense = jnp.sum(jnp.where(keep, spread, 0.0).reshape(tm // DEST_GROUP, DEST_GROUP, LANES), axis=1)
    dest_ref[...] = dense.astype(I32)

    @pl.when(pl.program_id(0) == 0)
    def _():
        nl = meta_ref.shape[1]
        r_i = lax.broadcasted_iota(I32, (LANES, LANES), 0)
        l_i = lax.broadcasted_iota(I32, (LANES, LANES), 1)
        ends_col = jnp.sum(jnp.where(l_i == r_i, ends_row, 0.0), axis=-1, keepdims=True)
        e_i = lax.broadcasted_iota(I32, (LANES, nl), 0)
        b_i = lax.broadcasted_iota(I32, (LANES, nl), 1)
        closed = (e_i < N_EXPERTS) & (ends_col <= (b_i * MOE_TM).astype(F32))
        be = jnp.minimum(jnp.sum(jnp.where(closed, 1.0, 0.0), axis=0, keepdims=True), N_EXPERTS - 1.0)
        total = jnp.sum(jnp.where(lane8[0:1, :] == N_EXPERTS - 1, ends_row, 0.0), axis=-1, keepdims=True)
        n_used = (total.astype(I32) >> shift).astype(F32)
        meta = jnp.where(b_i[0:1, :] < n_blocks, be, n_used).astype(I32)
        meta_ref[...] = jnp.broadcast_to(meta, meta_ref.shape)


def _moe_dest(te128, pos128, cnt, n_blocks):
    n = te128.shape[0]
    tm = ROW_TILE
    nl = -(-(n_blocks + 1) // LANES) * LANES
    row = lambda i: (i, 0)
    const = lambda i: (0, 0)
    return pl.pallas_call(
        functools.partial(_moe_dest_kernel, n_blocks=n_blocks),
        grid=(n // tm,),
        in_specs=[pl.BlockSpec((tm, LANES), row),
                  pl.BlockSpec((tm, LANES), row),
                  pl.BlockSpec((SUBLANES, LANES), const)],
        out_specs=[pl.BlockSpec((tm // DEST_GROUP, LANES), row),
                   pl.BlockSpec((SUBLANES, nl), const)],
        out_shape=[jax.ShapeDtypeStruct((n // DEST_GROUP, LANES), I32),
                   jax.ShapeDtypeStruct((SUBLANES, nl), I32)],
        compiler_params=_cparams(("arbitrary",)),
        name="moe_dest",
    )(te128, pos128, cnt)


def _row_token_kernel(dest_ref, rt_ref, *, n_assign):
    def clear(r, c):
        rt_ref[r] = 0
        return c

    def place(t, c):
        for k in range(TOP_K):
            rt_ref[dest_ref[t * TOP_K + k]] = t
        return c

    lax.fori_loop(0, rt_ref.shape[0], clear, 0, unroll=8)
    lax.fori_loop(0, n_assign // TOP_K, place, 0, unroll=4)


def _row_token(dest, rows_total):
    return pl.pallas_call(
        functools.partial(_row_token_kernel, n_assign=dest.shape[0]),
        grid_spec=pltpu.PrefetchScalarGridSpec(
            num_scalar_prefetch=1,
            grid=(1,),
            in_specs=[],
            out_specs=pl.BlockSpec(memory_space=pltpu.SMEM)),
        out_shape=jax.ShapeDtypeStruct((rows_total,), I32),
        compiler_params=_cparams(("arbitrary",)),
        name="moe_row_token",
    )(dest)


def _new_expert(be_ref, b):
    return (b == 0) | (be_ref[b] != be_ref[jnp.maximum(b - 1, 0)])


def _swiglu_pairs(v):
    g = jnp.minimum(v, SWIGLU_LIMIT)
    glu = g * jax.nn.sigmoid(g * SWIGLU_ALPHA)
    up1 = jnp.clip(v, -SWIGLU_LIMIT, SWIGLU_LIMIT) + 1.0
    return glu, up1


WEIGHT_DMA_PRIORITY = 1


def _stream_expert_weights(be_ref, nb_ref, run_ctr, copies, consume):
    j = pl.program_id(0)
    b = pl.program_id(1)
    nj = pl.num_programs(0)
    nb = nb_ref[0]
    last_blk = be_ref.shape[0] - 1
    e = be_ref[b]

    @pl.when((j == 0) & (b == 0))
    def _():
        run_ctr[0] = 0
        for c in copies(0, e, 0):
            c.start(priority=WEIGHT_DMA_PRIORITY)

    @pl.when((b < nb) & _new_expert(be_ref, b))
    def _():
        k = run_ctr[0]
        slot = k % 2
        for c in copies(j, e, slot):
            c.wait()
        consume(slot)
        run_end = lax.while_loop(lambda bb: (bb < nb) & (be_ref[jnp.minimum(bb, last_blk)] == e),
                                 lambda bb: bb + 1, b + 1)
        more_runs = run_end < nb
        j_next = jnp.where(more_runs, j, j + 1)
        e_next = jnp.where(more_runs, be_ref[jnp.minimum(run_end, last_blk)], be_ref[0])

        @pl.when(more_runs | (j + 1 < nj))
        def _():
            for c in copies(j_next, e_next, 1 - slot):
                c.start(priority=WEIGHT_DMA_PRIORITY)
        run_ctr[0] = k + 1


def _moe_gu_kernel(be_ref, nb_ref, rt_ref, h_ref, w_ref, ba_ref, bb_ref, o_ref, wa_s, wb_s, wbuf, sem, run_ctr,
                   xf, xbf, xsem, *, layer, tn, nj):
    j = pl.program_id(0)
    b = pl.program_id(1)
    nb = nb_ref[0]
    active = b < nb
    tm = xbf.shape[0]
    slot = (j * nb + b) % 2

    def copies(jj, e, s):
        return [pltpu.make_async_copy(
            w_ref.at[layer, e, :, pl.ds(pl.multiple_of((jj + h * nj) * tn, tn), tn)], wbuf.at[s, h], sem.at[s])
            for h in range(2)]

    def consume(s):
        wa_s[...] = wbuf[s, 0].astype(BF16)
        wb_s[...] = wbuf[s, 1].astype(BF16)

    def gather(blk, s):
        base = blk * tm
        for r in range(tm):
            pltpu.make_async_copy(h_ref.at[pl.ds(rt_ref[base + r], 1), :], xf.at[s, pl.ds(r, 1), :],
                                  xsem.at[s]).start()

    def gather_wait(s):
        pltpu.make_async_copy(h_ref.at[pl.ds(0, tm), :], xf.at[s], xsem.at[s]).wait()

    @pl.when((j == 0) & (b == 0))
    def _():
        gather(0, 0)

    _stream_expert_weights(be_ref, nb_ref, run_ctr, copies, consume)

    @pl.when(active)
    def _():
        gather_wait(slot)
        xbf[...] = xf[slot].astype(BF16)
        gather(jnp.where(b + 1 < nb, b + 1, 0), 1 - slot)
        x = xbf[...]
        ga = jnp.dot(x, wa_s[...], preferred_element_type=F32) + ba_ref[0]
        gb = jnp.dot(x, wb_s[...], preferred_element_type=F32) + bb_ref[0]
        even = lax.broadcasted_iota(I32, (tm, LANES), 1) % 2 == 0
        for c in range(tn // LANES):
            glu_a, up_a = _swiglu_pairs(ga[:, c * LANES:(c + 1) * LANES])
            glu_b, up_b = _swiglu_pairs(gb[:, c * LANES:(c + 1) * LANES])
            ra = glu_a * pltpu.roll(up_a, LANES - 1, 1)
            rb = pltpu.roll(glu_b, 1, 1) * up_b
            o_ref[:, c * LANES:(c + 1) * LANES] = jnp.where(even, ra, rb).astype(BF16)

        @pl.when((j == nj - 1) & (b == nb - 1))
        def _():
            gather_wait(1 - slot)

    @pl.when(b >= nb_ref[0])
    def _():
        o_ref[...] = jnp.zeros(o_ref.shape, o_ref.dtype)


def _moe_gu(block_e, n_blocks_used, row_tok, h, wgu_all, layer, bgu):
    d = h.shape[1]
    rows = row_tok.shape[0]
    ne, f2 = bgu.shape
    tm = MOE_TM
    tn = 1024
    nj = f2 // 2 // tn
    return pl.pallas_call(
        functools.partial(_moe_gu_kernel, layer=layer, tn=tn, nj=nj),
        grid_spec=pltpu.PrefetchScalarGridSpec(
            num_scalar_prefetch=3,
            grid=(nj, rows // tm),
            in_specs=[pl.BlockSpec(memory_space=pl.ANY),
                      pl.BlockSpec(memory_space=pl.ANY),
                      pl.BlockSpec((1, 1, tn), lambda j, b, be, nb, rt: (be[b], 0, j)),
                      pl.BlockSpec((1, 1, tn), lambda j, b, be, nb, rt: (be[b], 0, j + nj))],
            out_specs=pl.BlockSpec((tm, tn), lambda j, b, be, nb, rt: (b, j)),
            scratch_shapes=[pltpu.VMEM((d, tn), BF16), pltpu.VMEM((d, tn), BF16),
                            pltpu.VMEM((2, 2, d, tn), F32), pltpu.SemaphoreType.DMA((2,)),
                            pltpu.SMEM((1,), I32),
                            pltpu.VMEM((2, tm, d), F32), pltpu.VMEM((tm, d), BF16),
                            pltpu.SemaphoreType.DMA((2,))]),
        out_shape=jax.ShapeDtypeStruct((rows, f2 // 2), BF16),
        compiler_params=_cparams(("arbitrary", "arbitrary")),
        name="moe_gate_up",
    )(block_e, n_blocks_used, row_tok, h, wgu_all, bgu.reshape(ne, 1, f2), bgu.reshape(ne, 1, f2))


def _moe_down_kernel(be_ref, nb_ref, a_ref, w_ref, bd_ref, o_ref, wp_s, stage, wbuf, sem, run_ctr, *, layer, tn):
    b = pl.program_id(1)
    active = b < nb_ref[0]

    def copies(j, e, slot):
        return [pltpu.make_async_copy(w_ref.at[layer, e, :, pl.ds(pl.multiple_of(j * tn, tn), tn)], wbuf.at[slot],
                                      sem.at[slot])]

    def consume(slot):
        f = wbuf.shape[1]
        for c in range(tn // LANES):
            cols = slice(c * LANES, (c + 1) * LANES)
            stage[c, pl.ds(0, f // 2, stride=2), :] = wbuf[slot, :f // 2, cols]
            stage[c, pl.ds(1, f // 2, stride=2), :] = wbuf[slot, f // 2:, cols]
            wp_s[:, cols] = stage[c].astype(BF16)

    _stream_expert_weights(be_ref, nb_ref, run_ctr, copies, consume)

    @pl.when(active)
    def _():
        o_ref[...] = jnp.dot(a_ref[...], wp_s[...], preferred_element_type=F32) + bd_ref[0]

    @pl.when(b >= nb_ref[0])
    def _():
        o_ref[...] = jnp.zeros(o_ref.shape, o_ref.dtype)


def _moe_down(block_e, n_blocks_used, act, wd_all, layer, bd):
    rows, f = act.shape
    ne, d = bd.shape
    tm = MOE_TM
    tn = 1024
    return pl.pallas_call(
        functools.partial(_moe_down_kernel, layer=layer, tn=tn),
        grid_spec=pltpu.PrefetchScalarGridSpec(
            num_scalar_prefetch=2,
            grid=(d // tn, rows // tm),
            in_specs=[pl.BlockSpec((tm, f), lambda j, b, be, nb: (b, 0)),
                      pl.BlockSpec(memory_space=pl.ANY),
                      pl.BlockSpec((1, 1, tn), lambda j, b, be, nb: (be[b], 0, j))],
            out_specs=pl.BlockSpec((tm, tn), lambda j, b, be, nb: (b, j)),
            scratch_shapes=[pltpu.VMEM((f, tn), BF16), pltpu.VMEM((tn // LANES, f, LANES), F32),
                            pltpu.VMEM((2, f, tn), F32), pltpu.SemaphoreType.DMA((2,)), pltpu.SMEM((1,), I32)]),
        out_shape=jax.ShapeDtypeStruct((rows, d), F32),
        compiler_params=_cparams(("arbitrary", "arbitrary")),
        name="moe_down",
    )(block_e, n_blocks_used, act, wd_all, bd.reshape(ne, 1, d))


def _combine_kernel(dest_ref, y_ref, x1_ref, gm_ref, tg_ref, fg_ref, o_ref, buf, sem, *, tn, tile0, final_norm):
    i = pl.program_id(0) + tile0

    def issue(t, c):
        for k in range(TOP_K):
            d = dest_ref[(i * tn + t) * TOP_K + k]
            pltpu.make_async_copy(y_ref.at[pl.ds(d, 1), :], buf.at[k, pl.ds(t, 1), :], sem).start(priority=k % 2)
        return c

    lax.fori_loop(0, tn, issue, 0, unroll=8)
    for k in range(TOP_K):
        pltpu.make_async_copy(y_ref.at[pl.ds(0, tn), :], buf.at[k], sem).wait()
    tg = tg_ref[...]
    moe = tg[:, 0:1] * buf[0]
    for k in range(1, TOP_K):
        moe = moe + tg[:, k:k + 1] * buf[k]
    x2 = _gated_residual(x1_ref[...], gm_ref[...], moe)
    if final_norm:
        x2 = _rms(x2, fg_ref[...])
    o_ref[...] = x2


def _combine(dest, y, x1, modg, tgates, fg, group, final_norm, row0, rows):
    d = x1.shape[1]
    tn = ROW_TILE
    ng = tn // group
    t0 = row0 // tn
    return pl.pallas_call(
        functools.partial(_combine_kernel, tn=tn, tile0=t0, final_norm=final_norm),
        grid_spec=pltpu.PrefetchScalarGridSpec(
            num_scalar_prefetch=1,
            grid=(rows // tn,),
            in_specs=[pl.BlockSpec(memory_space=pl.ANY),
                      pl.BlockSpec((tn, d), lambda i, ds: (i + t0, 0)),
                      pl.BlockSpec((ng, d), lambda i, ds: (i + t0, 5)),
                      pl.BlockSpec((tn, LANES), lambda i, ds: (i + t0, 0)),
                      pl.BlockSpec((1, d), lambda i, ds: (0, 0))],
            out_specs=pl.BlockSpec((tn, d), lambda i, ds: (i, 0)),
            scratch_shapes=[pltpu.VMEM((TOP_K, tn, d), F32), pltpu.SemaphoreType.DMA(())]),
        out_shape=jax.ShapeDtypeStruct((rows, d), F32),
        compiler_params=_cparams(("arbitrary",)),
        name="moe_combine",
    )(dest, y, x1, modg, tgates, fg.reshape(1, d))


def _moe(hp, te128, tg128, pos128, cnt, x1, modg, wgu_all, bgu, wd_all, bd, layer, fg, group, splits):
    n = x1.shape[0]
    rows_total = n * TOP_K + N_EXPERTS * MOE_TM
    n_blocks = rows_total // MOE_TM
    dest2d, meta = _moe_dest(te128, pos128, cnt, n_blocks)
    dest = dest2d.reshape(-1)
    block_e = meta[0, :n_blocks]
    nbu = meta[0, n_blocks:n_blocks + 1]
    row_tok = _row_token(dest, rows_total)
    act = _moe_gu(block_e, nbu, row_tok, hp, wgu_all, layer, bgu)
    y = _moe_down(block_e, nbu, act, wd_all, layer, bd)
    return [_combine(dest, y, x1, modg, tg128, fg, group, fn, r0, rows) for r0, rows, fn in splits]


def _lru_in_kernel(x_ref, g1_ref, sh_ref, sc_ref, wy_ref, wx_ref, y_ref, xb_ref):
    h = _norm_mod(x_ref[...], g1_ref[...], sh_ref[...], sc_ref[...]).astype(BF16)
    y = jnp.dot(h, wy_ref[...], preferred_element_type=F32)
    y_ref[...] = jax.nn.gelu(y, approximate=True).astype(BF16)
    xb_ref[...] = jnp.dot(h, wx_ref[...], preferred_element_type=F32)


def _lru_in(x, g1, modg, wy, wx, group):
    n, d = x.shape
    dr = wy.shape[1]
    tm = ROW_TILE
    ng = tm // group
    row = lambda i: (i, 0)
    const = lambda i: (0, 0)
    return pl.pallas_call(
        _lru_in_kernel,
        grid=(n // tm,),
        in_specs=[pl.BlockSpec((tm, d), row),
                  pl.BlockSpec((1, d), const),
                  pl.BlockSpec((ng, d), lambda i: (i, 0)),
                  pl.BlockSpec((ng, d), lambda i: (i, 1)),
                  pl.BlockSpec((d, dr), const),
                  pl.BlockSpec((d, dr), const)],
        out_specs=[pl.BlockSpec((tm, dr), row), pl.BlockSpec((tm, dr), row)],
        out_shape=[jax.ShapeDtypeStruct((n, dr), BF16), jax.ShapeDtypeStruct((n, dr), F32)],
        compiler_params=_cparams(("arbitrary",)),
        name="lru_in",
    )(x, g1.reshape(1, d), modg, modg, wy, wx)


def _lru_scan_kernel(y_ref, xb_ref, cb_ref, h0_ref, cw_ref, cbias_ref, wa_ref, ba_ref, wx_ref, bx_ref, lam_ref,
                     hy_ref, cbo_ref, ho_ref, xe, a_s, u_s, hc, *, tc, starts_at_pos0):
    c = pl.program_id(1)
    dr = xb_ref.shape[1]
    nb = wa_ref.shape[0]
    bd = dr // nb
    pre = SUBLANES

    @pl.when(c == 0)
    def _():
        xe[0:pre, :] = jnp.zeros((pre, dr), F32)
        xe[pre - (CONV_W - 1):pre, :] = cb_ref[0]
        hc[...] = jnp.broadcast_to(h0_ref[0], (SUBLANES, dr))

    xe[pre:pre + tc, :] = xb_ref[...]
    cw = cw_ref[...]
    xc = cbias_ref[...] + xe[pre:pre + tc, :] * cw[CONV_W - 1:CONV_W, :]
    for k in range(1, CONV_W):
        xc = xc + xe[pre - k:pre - k + tc, :] * cw[CONV_W - 1 - k:CONV_W - k, :]
    cbo_ref[0] = xe[pre + tc - (CONV_W - 1):pre + tc, :]
    xe[0:pre, :] = xe[tc:tc + pre, :]

    xcb = xc.astype(BF16)
    ra = jnp.concatenate([jnp.dot(xcb[:, n * bd:(n + 1) * bd], wa_ref[n], preferred_element_type=F32)
                          for n in range(nb)], axis=1)
    rx = jnp.concatenate([jnp.dot(xcb[:, n * bd:(n + 1) * bd], wx_ref[n], preferred_element_type=F32)
                          for n in range(nb)], axis=1)
    r = jax.nn.sigmoid(ra + ba_ref[...])
    ig = jax.nn.sigmoid(rx + bx_ref[...])
    lam = lam_ref[...]
    log_sig = jnp.minimum(lam, 0.0) - jnp.log1p(jnp.exp(-jnp.abs(lam)))
    log_a = LRU_C * r * log_sig
    a = jnp.exp(log_a)
    th = jnp.tanh(log_a)
    mult = jnp.sqrt(-2.0 * th / (1.0 - th))
    if starts_at_pos0:
        first = (lax.broadcasted_iota(I32, (tc, 1), 0) == 0) & (c == 0)
        mult = jnp.where(first, 1.0, mult)
    a_s[...] = a
    u_s[...] = mult * ig * xc

    row8 = lax.broadcasted_iota(I32, (SUBLANES, dr), 0)

    def group_step(g, hprev):
        off = pl.multiple_of(g * SUBLANES, SUBLANES)
        aa = a_s[pl.ds(off, SUBLANES), :]
        uu = u_s[pl.ds(off, SUBLANES), :]
        s = 1
        while s < SUBLANES:
            m = row8 >= s
            uu = jnp.where(m, aa * pltpu.roll(uu, s, 0) + uu, uu)
            aa = jnp.where(m, aa * pltpu.roll(aa, s, 0), aa)
            s *= 2
        hh = aa * hprev + uu
        u_s[pl.ds(off, SUBLANES), :] = hh
        return jnp.broadcast_to(hh[SUBLANES - 1:SUBLANES, :], (SUBLANES, dr))

    hlast = lax.fori_loop(0, tc // SUBLANES, group_step, hc[...])
    hc[...] = hlast
    ho_ref[0] = hlast[0:1, :]
    hy_ref[...] = (u_s[...] * y_ref[...].astype(F32)).astype(BF16)


def _lru_scan(yb, xb, conv_buf, h0, cw, cbias, wa, ba, wx, bx, lam, row0, n_seq, t, tc, starts_at_pos0):
    dr = xb.shape[1]
    nc = t // tc
    rb0 = row0 // tc
    inmap = lambda s, c: (rb0 + s * nc + c, 0)
    outmap = lambda s, c: (s * nc + c, 0)
    const2 = lambda s, c: (0, 0)
    const3 = lambda s, c: (0, 0, 0)
    seq3 = lambda s, c: (s, 0, 0)
    return pl.pallas_call(
        functools.partial(_lru_scan_kernel, tc=tc, starts_at_pos0=starts_at_pos0),
        grid=(n_seq, nc),
        in_specs=[pl.BlockSpec((tc, dr), inmap),
                  pl.BlockSpec((tc, dr), inmap),
                  pl.BlockSpec((1, CONV_W - 1, dr), seq3),
                  pl.BlockSpec((1, 1, dr), seq3),
                  pl.BlockSpec((CONV_W, dr), const2),
                  pl.BlockSpec((1, dr), const2),
                  pl.BlockSpec(wa.shape, const3),
                  pl.BlockSpec((1, dr), const2),
                  pl.BlockSpec(wx.shape, const3),
                  pl.BlockSpec((1, dr), const2),
                  pl.BlockSpec((1, dr), const2)],
        out_specs=[pl.BlockSpec((tc, dr), outmap),
                   pl.BlockSpec((1, CONV_W - 1, dr), seq3),
                   pl.BlockSpec((1, 1, dr), seq3)],
        out_shape=[jax.ShapeDtypeStruct((n_seq * t, dr), BF16),
                   jax.ShapeDtypeStruct((n_seq, CONV_W - 1, dr), F32),
                   jax.ShapeDtypeStruct((n_seq, 1, dr), F32)],
        scratch_shapes=[pltpu.VMEM((SUBLANES + tc, dr), F32),
                        pltpu.VMEM((tc, dr), F32),
                        pltpu.VMEM((tc, dr), F32),
                        pltpu.VMEM((SUBLANES, dr), F32)],
        compiler_params=_cparams(("arbitrary", "arbitrary")),
        name="lru_scan",
    )(yb, xb, conv_buf, h0.reshape(n_seq, 1, dr), cw, cbias.reshape(1, dr), wa, ba.reshape(1, dr),
      wx, bx.reshape(1, dr), lam.reshape(1, dr))


def _rope_tables(pos):
    half = ROPE_DIM // 2
    inv = 1.0 / (ROPE_THETA ** (jnp.arange(0, ROPE_DIM, 2, dtype=F32) / ROPE_DIM))
    ang = pos.astype(F32)[:, None] * inv[None, :]
    cos, sin = jnp.cos(ang), jnp.sin(ang)
    z = jnp.zeros((pos.shape[0], LANES - ROPE_DIM), F32)
    return jnp.concatenate([cos, cos, z], axis=1), jnp.concatenate([-sin, sin, z], axis=1)


def _head_slab_weights(w_uq):
    ql = w_uq.shape[0]
    w = w_uq.reshape(ql, N_HEADS, NOPE_DIM + ROPE_DIM)
    z = jnp.zeros((ql, N_HEADS, HEAD_W - NOPE_DIM - ROPE_DIM), w.dtype)
    return jnp.concatenate([w, z], axis=2).reshape(ql, N_HEADS * HEAD_W).astype(BF16)


def kernel(x_prompt, x_sample, cache_ckv, cache_krope, state_conv, state_h, c_prompt, c_sample,
           mod_w, mod_b, norm1_g, norm2_g,
           mla_w_in, mla_q_norm_g, mla_kv_norm_g, mla_w_uq, mla_w_ukv, mla_w_o,
           lru_w_in, lru_conv_w, lru_conv_b, lru_w_a, lru_b_a, lru_w_x, lru_b_x, lru_lambda, lru_w_o,
           router_w, router_b, moe_w_gu, moe_b_gu, moe_w_down, moe_b_down, final_g):
    bp, tp, d = x_prompt.shape
    bs, ts, _ = x_sample.shape
    past = cache_ckv.shape[2]
    depth = mod_w.shape[0]
    n_p, n_s = bp * tp, bs * ts
    n = n_p + n_s
    group = math.gcd(tp, ts)
    assert group % SUBLANES == 0 and ROW_TILE % group == 0 and n_p % ROW_TILE == 0 and n_s % ROW_TILE == 0
    assert NOPE_DIM == LANES and V_DIM == LANES and ROPE_DIM <= LANES and n_p % n_s == 0

    x = jnp.concatenate([x_prompt.reshape(n_p, d), x_sample.reshape(n_s, d)], axis=0)
    nb = bp + bs
    nb_pad = -(-nb // SUBLANES) * SUBLANES
    c_all = jnp.concatenate([c_prompt, c_sample, jnp.zeros((nb_pad - nb, d), F32)], axis=0)
    grp_batch = np.concatenate([np.repeat(np.arange(bp), tp // group), bp + np.repeat(np.arange(bs), ts // group)])
    pos = jnp.concatenate([jnp.tile(jnp.arange(tp), bp), jnp.tile(past + jnp.arange(ts), bs)])
    cos, sin = _rope_tables(pos)

    outs = {k: [] for k in ("ckv_p", "kr_p", "conv_p", "h_p", "ckv_s", "kr_s", "conv_s", "h_s")}
    for i in range(depth):
        mod = _adaln(c_all, mod_w, i, mod_b[i])
        modg = jnp.take(mod, jnp.asarray(grp_batch), axis=0)
        j = i // 2
        if i % 2 == 0:
            w_in = mla_w_in[j]
            zpad = jnp.zeros((d, LANES - ROPE_DIM), F32)
            win = jnp.concatenate([w_in, zpad], axis=1).astype(BF16)
            wuq = _head_slab_weights(mla_w_uq[j])
            wukv = mla_w_ukv[j].reshape(KV_LORA, N_HEADS, NOPE_DIM + V_DIM)
            wk = wukv[:, :, :NOPE_DIM].reshape(KV_LORA, N_HEADS * NOPE_DIM).astype(BF16)
            wv = wukv[:, :, NOPE_DIM:].reshape(KV_LORA, N_HEADS * V_DIM).astype(BF16)
            q, ckv, kr, ckvb, krp = _mla_proj(x, norm1_g[i], modg, win, mla_q_norm_g[j], mla_kv_norm_g[j],
                                              wuq, cos, sin, group)
            kk, vv = _kv_expand(ckvb, krp, wk, wv, n_p)
            o_p = _attn_prompt(q, kk, vv, bp, tp)
            qa = _absorb(q, wk, n_p, n_s)
            ckr_pad = jnp.pad(cache_krope[j], ((0, 0), (0, 0), (0, LANES - ROPE_DIM)))
            o_lat = _attn_sample(qa, q, cache_ckv[j], ckr_pad, ckvb, krp, n_p, bs, ts, past)
            o_s = _unabsorb(o_lat, wv, n_s)
            wo = mla_w_o[j].astype(BF16)
            outs["ckv_p"].append(ckv[:n_p].reshape(bp, tp, KV_LORA))
            outs["kr_p"].append(kr[:n_p].reshape(bp, tp, ROPE_DIM))
            outs["ckv_s"].append(ckv[n_p:].reshape(bs, ts, KV_LORA))
            outs["kr_s"].append(kr[n_p:].reshape(bs, ts, ROPE_DIM))
        else:
            dr = lru_w_in.shape[2] // 2
            wy = lru_w_in[j][:, :dr].astype(BF16)
            wx = lru_w_in[j][:, dr:].astype(BF16)
            yb, xb = _lru_in(x, norm1_g[i], modg, wy, wx, group)
            wa = lru_w_a[j].astype(BF16)
            wxg = lru_w_x[j].astype(BF16)
            lru_args = (lru_conv_w[j], lru_conv_b[j], wa, lru_b_a[j], wxg, lru_b_x[j], lru_lambda[j])
            zbuf = jnp.zeros((bp, CONV_W - 1, dr), F32)
            zh = jnp.zeros((bp, dr), F32)
            o_p, cb_p, h_p = _lru_scan(yb, xb, zbuf, zh, *lru_args, 0, bp, tp, ROW_TILE, True)
            o_s, cb_s, h_s = _lru_scan(yb, xb, state_conv[j], state_h[j], *lru_args, n_p, bs, ts, ts, False)
            wo = lru_w_o[j].astype(BF16)
            outs["conv_p"].append(cb_p)
            outs["h_p"].append(h_p.reshape(bp, dr))
            outs["conv_s"].append(cb_s)
            outs["h_s"].append(h_s.reshape(bs, dr))
        x1, hp, te128, tg128, pos128, cnt = _post_mixer(o_p, o_s, wo, x, norm2_g[i], modg, router_w[i], router_b[i],
                                                        group)
        last = i == depth - 1
        splits = [(0, n_p, True), (n_p, n_s, True)] if last else [(0, n, False)]
        res = _moe(hp, te128, tg128, pos128, cnt, x1, modg, moe_w_gu, moe_b_gu[i], moe_w_down, moe_b_down[i], i,
                   final_g, group, splits)
        x = res[0]
    y_prompt = res[0].reshape(bp, tp, d)
    y_sample = res[1].reshape(bs, ts, d)
    return (y_prompt, y_sample,
            jnp.stack(outs["ckv_p"]), jnp.stack(outs["kr_p"]), jnp.stack(outs["conv_p"]), jnp.stack(outs["h_p"]),
            jnp.stack(outs["ckv_s"]), jnp.stack(outs["kr_s"]), jnp.stack(outs["conv_s"]), jnp.stack(outs["h_s"]))
```

```python
import functools
import math

import jax
import jax.numpy as jnp
import numpy as np
from jax import lax
from jax.experimental import pallas as pl
from jax.experimental.pallas import tpu as pltpu

F32 = jnp.float32
BF16 = jnp.bfloat16
I32 = jnp.int32

CHUNK = 64
N_HEADS = 16
Q_LORA = 512
KV_LORA = 512
NOPE_DIM = 128
ROPE_DIM = 64
V_DIM = 128
ROPE_THETA = 10000.0
LRU_BLOCKS = 8
CONV_W = 4
LRU_C = 8.0
N_EXPERTS = 32
TOP_K = 4
SWIGLU_LIMIT = 7.0
SWIGLU_ALPHA = 1.702
N_MOD = 6
EPS = 1e-6

LANES = 128
SUBLANES = 8
HEAD_W = 2 * LANES

ROW_TILE = 256
MOE_TM = 256
VMEM_LIMIT = 56 * 1024 * 1024


def _cparams(sem):
    return pltpu.CompilerParams(dimension_semantics=sem, vmem_limit_bytes=VMEM_LIMIT)


def _rms(x, g):
    ms = jnp.mean(x * x, axis=-1, keepdims=True)
    return x * lax.rsqrt(ms + EPS) * g


def _norm_mod(x, g, shift, scale):
    tm, d = x.shape
    ng = shift.shape[0]
    y = _rms(x, g).reshape(ng, tm // ng, d)
    return (y * (1.0 + scale[:, None, :]) + shift[:, None, :]).reshape(tm, d)


def _gated_residual(x, gate, y):
    tm, d = x.shape
    ng = gate.shape[0]
    return (x.reshape(ng, tm // ng, d) + gate[:, None, :] * y.reshape(ng, tm // ng, d)).reshape(tm, d)


def _adaln_kernel(c_ref, w_ref, b_ref, o_ref):
    c = c_ref[...]
    a = (c * jax.nn.sigmoid(c)).astype(BF16)
    o_ref[...] = jnp.dot(a, w_ref[0].astype(BF16), preferred_element_type=F32) + b_ref[...]


def _adaln(c_all, w_all, layer, b):
    bp, d = c_all.shape
    n = w_all.shape[2]
    tn = 1024
    return pl.pallas_call(
        _adaln_kernel,
        grid=(n // tn,),
        in_specs=[pl.BlockSpec((bp, d), lambda j: (0, 0)),
                  pl.BlockSpec((1, d, tn), lambda j: (layer, 0, j)),
                  pl.BlockSpec((1, tn), lambda j: (0, j))],
        out_specs=pl.BlockSpec((bp, tn), lambda j: (0, j)),
        out_shape=jax.ShapeDtypeStruct((bp, n), F32),
        compiler_params=_cparams(("arbitrary",)),
        name="adaln",
    )(c_all, w_all, b.reshape(1, n))


def _rope128(v, cos, sin):
    half = ROPE_DIM // 2
    lane = lax.broadcasted_iota(I32, v.shape, 1)
    sw = jnp.where(lane < half, pltpu.roll(v, LANES - half, 1), pltpu.roll(v, half, 1))
    return v * cos + sw * sin


def _mla_proj_kernel(x_ref, g1_ref, sh_ref, sc_ref, win_ref, qg_ref, kvg_ref, wuq_ref, cos_ref, sin_ref,
                     q_ref, ckv_ref, kr_ref, ckvb_ref, krp_ref):
    h = _norm_mod(x_ref[...], g1_ref[...], sh_ref[...], sc_ref[...]).astype(BF16)
    lat = jnp.dot(h, win_ref[...], preferred_element_type=F32)
    q_lat = lat[:, :Q_LORA]
    c_kv = lat[:, Q_LORA:Q_LORA + KV_LORA]
    k_r = lat[:, Q_LORA + KV_LORA:]
    qn = _rms(q_lat, qg_ref[...]).astype(BF16)
    q = jnp.dot(qn, wuq_ref[...], preferred_element_type=F32)
    cos = cos_ref[...]
    sin = sin_ref[...]
    scale = (NOPE_DIM + ROPE_DIM) ** -0.5
    for hh in range(N_HEADS):
        lo = hh * HEAD_W
        q_ref[:, lo:lo + LANES] = (q[:, lo:lo + LANES] * scale).astype(BF16)
        q_ref[:, lo + LANES:lo + HEAD_W] = (_rope128(q[:, lo + LANES:lo + HEAD_W], cos, sin) * scale).astype(BF16)
    ckv = _rms(c_kv, kvg_ref[...])
    ckv_ref[...] = ckv
    ckvb_ref[...] = ckv.astype(BF16)
    kr = _rope128(k_r, cos, sin)
    kr_ref[...] = kr[:, :ROPE_DIM]
    krp_ref[...] = kr.astype(BF16)


def _mla_proj(x, g1, modg, win, qg, kvg, wuq, cos, sin, group):
    n, d = x.shape
    tm = ROW_TILE
    ng = tm // group
    wl = win.shape[1]
    qw = wuq.shape[1]
    row = lambda i: (i, 0)
    const = lambda i: (0, 0)
    return pl.pallas_call(
        _mla_proj_kernel,
        grid=(n // tm,),
        in_specs=[pl.BlockSpec((tm, d), row),
                  pl.BlockSpec((1, d), const),
                  pl.BlockSpec((ng, d), lambda i: (i, 0)),
                  pl.BlockSpec((ng, d), lambda i: (i, 1)),
                  pl.BlockSpec((d, wl), const),
                  pl.BlockSpec((1, Q_LORA), const),
                  pl.BlockSpec((1, KV_LORA), const),
                  pl.BlockSpec((Q_LORA, qw), const),
                  pl.BlockSpec((tm, LANES), row),
                  pl.BlockSpec((tm, LANES), row)],
        out_specs=[pl.BlockSpec((tm, qw), row),
                   pl.BlockSpec((tm, KV_LORA), row),
                   pl.BlockSpec((tm, ROPE_DIM), row),
                   pl.BlockSpec((tm, KV_LORA), row),
                   pl.BlockSpec((tm, LANES), row)],
        out_shape=[jax.ShapeDtypeStruct((n, qw), BF16),
                   jax.ShapeDtypeStruct((n, KV_LORA), F32),
                   jax.ShapeDtypeStruct((n, ROPE_DIM), F32),
                   jax.ShapeDtypeStruct((n, KV_LORA), BF16),
                   jax.ShapeDtypeStruct((n, LANES), BF16)],
        compiler_params=_cparams(("arbitrary",)),
        name="mla_proj",
    )(x, g1.reshape(1, d), modg, modg, win, qg.reshape(1, -1), kvg.reshape(1, -1), wuq, cos, sin)


def _kv_expand_kernel(c_ref, krp_ref, wk_ref, wv_ref, k_ref, v_ref):
    c = c_ref[...]
    kn = jnp.dot(c, wk_ref[...], preferred_element_type=F32).astype(BF16)
    krp = krp_ref[...]
    for hh in range(N_HEADS):
        k_ref[:, hh * HEAD_W:hh * HEAD_W + LANES] = kn[:, hh * NOPE_DIM:(hh + 1) * NOPE_DIM]
        k_ref[:, hh * HEAD_W + LANES:(hh + 1) * HEAD_W] = krp
    v_ref[...] = jnp.dot(c, wv_ref[...], preferred_element_type=F32).astype(BF16)


def _kv_expand(ckvb, krp, wk, wv, rows):
    tm = 512
    row = lambda i: (i, 0)
    const = lambda i: (0, 0)
    return pl.pallas_call(
        _kv_expand_kernel,
        grid=(rows // tm,),
        in_specs=[pl.BlockSpec((tm, KV_LORA), row),
                  pl.BlockSpec((tm, LANES), row),
                  pl.BlockSpec(wk.shape, const),
                  pl.BlockSpec(wv.shape, const)],
        out_specs=[pl.BlockSpec((tm, N_HEADS * HEAD_W), row),
                   pl.BlockSpec((tm, N_HEADS * V_DIM), row)],
        out_shape=[jax.ShapeDtypeStruct((rows, N_HEADS * HEAD_W), BF16),
                   jax.ShapeDtypeStruct((rows, N_HEADS * V_DIM), BF16)],
        compiler_params=_cparams(("arbitrary",)),
        name="kv_expand",
    )(ckvb, krp, wk, wv)


ATTN_HEADS_PER_STEP = 2


def _attn_prompt_kernel(q_ref, k_ref, v_ref, o_ref, *, tq, nq):
    qi = pl.program_id(2)
    dn = (((1,), (1,)), ((), ()))
    hs = ATTN_HEADS_PER_STEP
    r = lax.broadcasted_iota(I32, (tq, tq), 0) // CHUNK
    c = lax.broadcasted_iota(I32, (tq, tq), 1) // CHUNK
    diag_visible = c <= r
    for qs in range(nq):
        @pl.when(qi == qs)
        def _(qs=qs):
            past = qs * tq
            for h in range(hs):
                q = q_ref[:, h * HEAD_W:(h + 1) * HEAD_W]
                kcols = slice(h * HEAD_W, (h + 1) * HEAD_W)
                vcols = slice(h * V_DIM, (h + 1) * V_DIM)
                s_d = lax.dot_general(q, k_ref[past:past + tq, kcols], dn, preferred_element_type=F32)
                s_d = jnp.where(diag_visible, s_d, -jnp.inf)
                m = jnp.max(s_d, axis=-1, keepdims=True)
                if past:
                    s_f = lax.dot_general(q, k_ref[0:past, kcols], dn, preferred_element_type=F32)
                    m = jnp.maximum(m, jnp.max(s_f, axis=-1, keepdims=True))
                p_d = jnp.exp(s_d - m)
                l = jnp.sum(p_d, axis=-1, keepdims=True)
                acc = jnp.dot(p_d.astype(BF16), v_ref[past:past + tq, vcols], preferred_element_type=F32)
                if past:
                    p_f = jnp.exp(s_f - m)
                    l = l + jnp.sum(p_f, axis=-1, keepdims=True)
                    acc = acc + jnp.dot(p_f.astype(BF16), v_ref[0:past, vcols], preferred_element_type=F32)
                o_ref[:, vcols] = (acc / l).astype(BF16)


def _attn_prompt(q, k, v, bp, tp):
    n_rows = bp * tp
    tq = 256
    nq = tp // tq
    hs = ATTN_HEADS_PER_STEP
    return pl.pallas_call(
        functools.partial(_attn_prompt_kernel, tq=tq, nq=nq),
        grid=(bp, N_HEADS // hs, nq),
        in_specs=[pl.BlockSpec((tq, hs * HEAD_W), lambda b, h, i: (b * nq + i, h)),
                  pl.BlockSpec((tp, hs * HEAD_W), lambda b, h, i: (b, h)),
                  pl.BlockSpec((tp, hs * V_DIM), lambda b, h, i: (b, h))],
        out_specs=pl.BlockSpec((tq, hs * V_DIM), lambda b, h, i: (b * nq + i, h)),
        out_shape=jax.ShapeDtypeStruct((n_rows, N_HEADS * V_DIM), BF16),
        compiler_params=_cparams(("arbitrary", "arbitrary", "arbitrary")),
        name="attn_prompt",
    )(q, k, v)


def _absorb_kernel(q_ref, wk_ref, o_ref):
    dn = (((1,), (1,)), ((), ()))
    o_ref[0] = lax.dot_general(q_ref[...], wk_ref[...], dn, preferred_element_type=F32).astype(BF16)


def _absorb(q, wk, row0, rows):
    rb = row0 // rows
    return pl.pallas_call(
        _absorb_kernel,
        grid=(N_HEADS,),
        in_specs=[pl.BlockSpec((rows, LANES), lambda h: (rb, 2 * h)),
                  pl.BlockSpec((KV_LORA, NOPE_DIM), lambda h: (0, h))],
        out_specs=pl.BlockSpec((1, rows, KV_LORA), lambda h: (h, 0, 0)),
        out_shape=jax.ShapeDtypeStruct((N_HEADS, rows, KV_LORA), BF16),
        compiler_params=_cparams(("arbitrary",)),
        name="absorb",
    )(q, wk)


def _attn_sample_kernel(qa_ref, q_ref, cc_ref, ckr_ref, cn_ref, krn_ref, o_ref, *, ts, past):
    hn = N_HEADS
    qa = qa_ref[...].reshape(hn * ts, KV_LORA)
    qfull = q_ref[...]
    qr = jnp.concatenate([qfull[:, h * HEAD_W + LANES:(h + 1) * HEAD_W] for h in range(hn)], axis=0)
    cc = cc_ref[0].astype(BF16)
    ckr = ckr_ref[0].astype(BF16)
    cn = cn_ref[...]
    krn = krn_ref[...]
    dn = (((1,), (1,)), ((), ()))
    s_c = (lax.dot_general(qa, cc, dn, preferred_element_type=F32)
           + lax.dot_general(qr, ckr, dn, preferred_element_type=F32))
    s_n = (lax.dot_general(qa, cn, dn, preferred_element_type=F32)
           + lax.dot_general(qr, krn, dn, preferred_element_type=F32))
    qchunk_c = (past + lax.broadcasted_iota(I32, s_c.shape, 0) % ts) // CHUNK
    s_c = jnp.where(lax.broadcasted_iota(I32, s_c.shape, 1) // CHUNK <= qchunk_c, s_c, -jnp.inf)
    qchunk_n = (past + lax.broadcasted_iota(I32, s_n.shape, 0) % ts) // CHUNK
    s_n = jnp.where((past + lax.broadcasted_iota(I32, s_n.shape, 1)) // CHUNK <= qchunk_n, s_n, -jnp.inf)
    m = jnp.maximum(jnp.max(s_c, axis=-1, keepdims=True), jnp.max(s_n, axis=-1, keepdims=True))
    p_c = jnp.exp(s_c - m)
    p_n = jnp.exp(s_n - m)
    l = jnp.sum(p_c, axis=-1, keepdims=True) + jnp.sum(p_n, axis=-1, keepdims=True)
    o = (jnp.dot(p_c.astype(BF16), cc, preferred_element_type=F32)
         + jnp.dot(p_n.astype(BF16), cn, preferred_element_type=F32)) / l
    o_ref[...] = o.astype(BF16).reshape(hn, ts, KV_LORA)


def _attn_sample(qa, q, cache_c, cache_kr, ckvb, krp, row0, bs, ts, past):
    rb0 = row0 // ts
    return pl.pallas_call(
        functools.partial(_attn_sample_kernel, ts=ts, past=past),
        grid=(bs,),
        in_specs=[pl.BlockSpec((N_HEADS, ts, KV_LORA), lambda b: (0, b, 0)),
                  pl.BlockSpec((ts, N_HEADS * HEAD_W), lambda b: (rb0 + b, 0)),
                  pl.BlockSpec((1, past, KV_LORA), lambda b: (b, 0, 0)),
                  pl.BlockSpec((1, past, LANES), lambda b: (b, 0, 0)),
                  pl.BlockSpec((ts, KV_LORA), lambda b: (rb0 + b, 0)),
                  pl.BlockSpec((ts, LANES), lambda b: (rb0 + b, 0))],
        out_specs=pl.BlockSpec((N_HEADS, ts, KV_LORA), lambda b: (0, b, 0)),
        out_shape=jax.ShapeDtypeStruct((N_HEADS, bs * ts, KV_LORA), BF16),
        compiler_params=_cparams(("arbitrary",)),
        name="attn_sample",
    )(qa, q, cache_c, cache_kr, ckvb, krp)


def _unabsorb_kernel(ol_ref, wv_ref, o_ref):
    o_ref[...] = jnp.dot(ol_ref[0], wv_ref[...], preferred_element_type=F32).astype(BF16)


def _unabsorb(o_lat, wv, rows):
    return pl.pallas_call(
        _unabsorb_kernel,
        grid=(N_HEADS,),
        in_specs=[pl.BlockSpec((1, rows, KV_LORA), lambda h: (h, 0, 0)),
                  pl.BlockSpec((KV_LORA, V_DIM), lambda h: (0, h))],
        out_specs=pl.BlockSpec((rows, V_DIM), lambda h: (0, h)),
        out_shape=jax.ShapeDtypeStruct((rows, N_HEADS * V_DIM), BF16),
        compiler_params=_cparams(("arbitrary",)),
        name="unabsorb",
    )(o_lat, wv)


def _post_mixer_kernel(op_ref, os_ref, wo_ref, x_ref, gate_ref, g2_ref, sh_ref, sc_ref, rw_ref, rb_ref,
                       x1_ref, hp_ref, te_ref, tg_ref, pos_ref, cnt_ref, carry, *, prompt_tiles):
    @pl.when(pl.program_id(0) == 0)
    def _():
        carry[...] = jnp.zeros(carry.shape, F32)

    o = jnp.where(pl.program_id(0) < prompt_tiles, op_ref[...], os_ref[...])
    y = jnp.dot(o, wo_ref[...], preferred_element_type=F32)
    x1 = _gated_residual(x_ref[...], gate_ref[...], y)
    x1_ref[...] = x1
    h2 = _norm_mod(x1, g2_ref[...], sh_ref[...], sc_ref[...])
    hp_ref[...] = h2
    rw = rw_ref[...]
    h_hi = h2.astype(BF16)
    h_lo = (h2 - h_hi.astype(F32)).astype(BF16)
    w_hi = rw.astype(BF16)
    w_lo = (rw - w_hi.astype(F32)).astype(BF16)
    logits = (jnp.dot(h_hi, w_hi, preferred_element_type=F32) + jnp.dot(h_lo, w_hi, preferred_element_type=F32)
              + jnp.dot(h_hi, w_lo, preferred_element_type=F32) + rb_ref[...])
    tm, ne = logits.shape
    eid = lax.broadcasted_iota(I32, (tm, ne), 1)
    lane = lax.broadcasted_iota(I32, (tm, LANES), 1)
    te = jnp.zeros((tm, LANES), I32)
    tv = jnp.full((tm, LANES), -jnp.inf, F32)
    work = logits
    picks = []
    for k in range(TOP_K):
        mx = jnp.max(work, axis=-1, keepdims=True)
        idx = jnp.min(jnp.where(work == mx, eid, ne), axis=-1, keepdims=True)
        picks.append(idx)
        te = jnp.where(lane == k, idx, te)
        tv = jnp.where(lane == k, mx, tv)
        work = jnp.where(eid == idx, -jnp.inf, work)
    ex = jnp.exp(tv - jnp.max(tv, axis=-1, keepdims=True))
    te_ref[...] = te
    tg_ref[...] = ex / jnp.sum(ex, axis=-1, keepdims=True)
    onehot = jnp.zeros((tm, LANES), F32)
    for idx in picks:
        onehot = onehot + (lane == idx).astype(F32)
    tri = (lax.broadcasted_iota(I32, (tm, tm), 1) < lax.broadcasted_iota(I32, (tm, tm), 0)).astype(BF16)
    rank = jnp.dot(tri, onehot.astype(BF16), preferred_element_type=F32) + carry[0:1, :]
    pos = jnp.zeros((tm, LANES), I32)
    for k, idx in enumerate(picks):
        pk = jnp.sum(jnp.where(lane == idx, rank, 0.0), axis=-1, keepdims=True)
        pos = jnp.where(lane == k, pk.astype(I32), pos)
    pos_ref[...] = pos
    total = carry[0:1, :] + jnp.sum(onehot, axis=0, keepdims=True)
    carry[...] = jnp.broadcast_to(total, carry.shape)
    cnt_ref[...] = carry[...]


def _post_mixer(o_p, o_s, wo, x, g2, modg, rw, rb, group):
    n, d = x.shape
    tm = ROW_TILE
    ng = tm // group
    npt = o_p.shape[0] // tm
    nst = o_s.shape[0] // tm
    row = lambda i: (i, 0)
    const = lambda i: (0, 0)
    return pl.pallas_call(
        functools.partial(_post_mixer_kernel, prompt_tiles=npt),
        grid=(n // tm,),
        in_specs=[pl.BlockSpec((tm, o_p.shape[1]), lambda i: (jnp.minimum(i, npt - 1), 0)),
                  pl.BlockSpec((tm, o_s.shape[1]), lambda i: (jnp.clip(i - npt, 0, nst - 1), 0)),
                  pl.BlockSpec(wo.shape, const),
                  pl.BlockSpec((tm, d), row),
                  pl.BlockSpec((ng, d), lambda i: (i, 2)),
                  pl.BlockSpec((1, d), const),
                  pl.BlockSpec((ng, d), lambda i: (i, 3)),
                  pl.BlockSpec((ng, d), lambda i: (i, 4)),
                  pl.BlockSpec(rw.shape, const),
                  pl.BlockSpec((1, rw.shape[1]), const)],
        out_specs=[pl.BlockSpec((tm, d), row),
                   pl.BlockSpec((tm, d), row),
                   pl.BlockSpec((tm, LANES), row),
                   pl.BlockSpec((tm, LANES), row),
                   pl.BlockSpec((tm, LANES), row),
                   pl.BlockSpec((SUBLANES, LANES), const)],
        out_shape=[jax.ShapeDtypeStruct((n, d), F32),
                   jax.ShapeDtypeStruct((n, d), F32),
                   jax.ShapeDtypeStruct((n, LANES), I32),
                   jax.ShapeDtypeStruct((n, LANES), F32),
                   jax.ShapeDtypeStruct((n, LANES), I32),
                   jax.ShapeDtypeStruct((SUBLANES, LANES), F32)],
        scratch_shapes=[pltpu.VMEM((SUBLANES, LANES), F32)],
        compiler_params=_cparams(("arbitrary",)),
        name="post_mixer",
    )(o_p, o_s, wo, x, modg, g2.reshape(1, d), modg, modg, rw, rb.reshape(1, -1))


DEST_GROUP = LANES // TOP_K


def _moe_dest_kernel(te_ref, pos_ref, cnt_ref, dest_ref, meta_ref, *, n_blocks):
    shift = MOE_TM.bit_length() - 1
    lane8 = lax.broadcasted_iota(I32, (SUBLANES, LANES), 1)
    cnt = cnt_ref[...].astype(I32)
    padded = ((cnt + (MOE_TM - 1)) >> shift) << shift
    ends = padded.astype(F32)
    s = 1
    while s < N_EXPERTS:
        ends = ends + jnp.where(lane8 >= s, pltpu.roll(ends, s, 1), 0.0)
        s *= 2
    ends_row = ends[0:1, :]
    starts_row = (ends - padded.astype(F32))[0:1, :]
    te = te_ref[...]
    pos = pos_ref[...]
    tm = te.shape[0]
    lane = lax.broadcasted_iota(I32, (tm, LANES), 1)
    dest = jnp.zeros((tm, LANES), F32)
    for k in range(TOP_K):
        sk = jnp.sum(jnp.where(lane == te[:, k:k + 1], starts_row, 0.0), axis=-1, keepdims=True)
        dest = jnp.where(lane == k, sk + pos[:, k:k + 1].astype(F32), dest)
    hi = jnp.floor(dest * (1.0 / 256.0))
    lo = dest - 256.0 * hi
    sel = (lax.broadcasted_iota(I32, (LANES, LANES), 0)
           == lax.broadcasted_iota(I32, (LANES, LANES), 1) % TOP_K).astype(BF16)
    spread = (256.0 * jnp.dot(hi.astype(BF16), sel, preferred_element_type=F32)
              + jnp.dot(lo.astype(BF16), sel, preferred_element_type=F32))
    row = lax.broadcasted_iota(I32, (tm, LANES), 0)
    keep = lane // TOP_K == row % DEST_GROUP
    dense = jnp.sum(jnp.where(keep, spread, 0.0).reshape(tm // DEST_GROUP, DEST_GROUP, LANES), axis=1)
    dest_ref[...] = dense.astype(I32)

    @pl.when(pl.program_id(0) == 0)
    def _():
        nl = meta_ref.shape[1]
        r_i = lax.broadcasted_iota(I32, (LANES, LANES), 0)
        l_i = lax.broadcasted_iota(I32, (LANES, LANES), 1)
        ends_col = jnp.sum(jnp.where(l_i == r_i, ends_row, 0.0), axis=-1, keepdims=True)
        e_i = lax.broadcasted_iota(I32, (LANES, nl), 0)
        b_i = lax.broadcasted_iota(I32, (LANES, nl), 1)
        closed = (e_i < N_EXPERTS) & (ends_col <= (b_i * MOE_TM).astype(F32))
        be = jnp.minimum(jnp.sum(jnp.where(closed, 1.0, 0.0), axis=0, keepdims=True), N_EXPERTS - 1.0)
        total = jnp.sum(jnp.where(lane8[0:1, :] == N_EXPERTS - 1, ends_row, 0.0), axis=-1, keepdims=True)
        n_used = (total.astype(I32) >> shift).astype(F32)
        meta = jnp.where(b_i[0:1, :] < n_blocks, be, n_used).astype(I32)
        meta_ref[...] = jnp.broadcast_to(meta, meta_ref.shape)


def _moe_dest(te128, pos128, cnt, n_blocks):
    n = te128.shape[0]
    tm = ROW_TILE
    nl = -(-(n_blocks + 1) // LANES) * LANES
    row = lambda i: (i, 0)
    const = lambda i: (0, 0)
    return pl.pallas_call(
        functools.partial(_moe_dest_kernel, n_blocks=n_blocks),
        grid=(n // tm,),
        in_specs=[pl.BlockSpec((tm, LANES), row),
                  pl.BlockSpec((tm, LANES), row),
                  pl.BlockSpec((SUBLANES, LANES), const)],
        out_specs=[pl.BlockSpec((tm // DEST_GROUP, LANES), row),
                   pl.BlockSpec((SUBLANES, nl), const)],
        out_shape=[jax.ShapeDtypeStruct((n // DEST_GROUP, LANES), I32),
                   jax.ShapeDtypeStruct((SUBLANES, nl), I32)],
        compiler_params=_cparams(("arbitrary",)),
        name="moe_dest",
    )(te128, pos128, cnt)


def _row_token_kernel(dest_ref, rt_ref, *, n_assign):
    def clear(r, c):
        rt_ref[r] = 0
        return c

    def place(t, c):
        for k in range(TOP_K):
            rt_ref[dest_ref[t * TOP_K + k]] = t
        return c

    lax.fori_loop(0, rt_ref.shape[0], clear, 0, unroll=8)
    lax.fori_loop(0, n_assign // TOP_K, place, 0, unroll=4)


def _row_token(dest, rows_total):
    return pl.pallas_call(
        functools.partial(_row_token_kernel, n_assign=dest.shape[0]),
        grid_spec=pltpu.PrefetchScalarGridSpec(
            num_scalar_prefetch=1,
            grid=(1,),
            in_specs=[],
            out_specs=pl.BlockSpec(memory_space=pltpu.SMEM)),
        out_shape=jax.ShapeDtypeStruct((rows_total,), I32),
        compiler_params=_cparams(("arbitrary",)),
        name="moe_row_token",
    )(dest)


def _dispatch_kernel(rt_ref, nb_ref, h_ref, o_ref, buf, sem, *, tg):
    i = pl.program_id(0)
    n_used = (nb_ref[0] * MOE_TM + tg - 1) // tg
    slot = i % 2
    unroll = 8

    def issue(blk, s):
        def body(g, c):
            for u in range(unroll):
                r = g * unroll + u
                tok = rt_ref[blk * tg + r]
                pltpu.make_async_copy(h_ref.at[pl.ds(tok, 1), :], buf.at[s, pl.ds(r, 1), :],
                                      sem.at[s]).start(priority=u % 2)
            return c
        lax.fori_loop(0, tg // unroll, body, 0)

    @pl.when(i == 0)
    def _():
        issue(0, 0)

    @pl.when(i + 1 < n_used)
    def _():
        issue(i + 1, 1 - slot)

    @pl.when(i < n_used)
    def _():
        pltpu.make_async_copy(h_ref.at[pl.ds(0, tg), :], buf.at[slot], sem.at[slot]).wait()
        o_ref[...] = buf[slot].astype(BF16)

    @pl.when(i >= n_used)
    def _():
        o_ref[...] = jnp.zeros(o_ref.shape, o_ref.dtype)


def _dispatch(row_tok, n_blocks_used, h, rows_total):
    tg = 2 * MOE_TM
    d = h.shape[1]
    return pl.pallas_call(
        functools.partial(_dispatch_kernel, tg=tg),
        grid_spec=pltpu.PrefetchScalarGridSpec(
            num_scalar_prefetch=2,
            grid=(rows_total // tg,),
            in_specs=[pl.BlockSpec(memory_space=pl.ANY)],
            out_specs=pl.BlockSpec((tg, d), lambda i, rt, nb: (i, 0)),
            scratch_shapes=[pltpu.VMEM((2, tg, d), F32), pltpu.SemaphoreType.DMA((2,))]),
        out_shape=jax.ShapeDtypeStruct((rows_total, d), BF16),
        compiler_params=_cparams(("arbitrary",)),
        name="moe_dispatch",
    )(row_tok, n_blocks_used, h)


def _new_expert(be_ref, b):
    return (b == 0) | (be_ref[b] != be_ref[jnp.maximum(b - 1, 0)])


def _swiglu_pairs(v):
    g = jnp.minimum(v, SWIGLU_LIMIT)
    glu = g * jax.nn.sigmoid(g * SWIGLU_ALPHA)
    up1 = jnp.clip(v, -SWIGLU_LIMIT, SWIGLU_LIMIT) + 1.0
    return glu, up1


WEIGHT_DMA_PRIORITY = 1


def _stream_expert_weights(be_ref, nb_ref, run_ctr, copies, consume):
    j = pl.program_id(0)
    b = pl.program_id(1)
    nj = pl.num_programs(0)
    nb = nb_ref[0]
    last_blk = be_ref.shape[0] - 1
    e = be_ref[b]

    @pl.when((j == 0) & (b == 0))
    def _():
        run_ctr[0] = 0
        for c in copies(0, e, 0):
            c.start(priority=WEIGHT_DMA_PRIORITY)

    @pl.when((b < nb) & _new_expert(be_ref, b))
    def _():
        k = run_ctr[0]
        slot = k % 2
        for c in copies(j, e, slot):
            c.wait()
        consume(slot)
        run_end = lax.while_loop(lambda bb: (bb < nb) & (be_ref[jnp.minimum(bb, last_blk)] == e),
                                 lambda bb: bb + 1, b + 1)
        more_runs = run_end < nb
        j_next = jnp.where(more_runs, j, j + 1)
        e_next = jnp.where(more_runs, be_ref[jnp.minimum(run_end, last_blk)], be_ref[0])

        @pl.when(more_runs | (j + 1 < nj))
        def _():
            for c in copies(j_next, e_next, 1 - slot):
                c.start(priority=WEIGHT_DMA_PRIORITY)
        run_ctr[0] = k + 1


def _moe_gu_kernel(be_ref, nb_ref, xb_ref, w_ref, bias_ref, o_ref, wa_s, wb_s, wbuf, sem, run_ctr,
                   *, layer, tn, nj):
    j = pl.program_id(0)
    b = pl.program_id(1)
    nb = nb_ref[0]
    active = b < nb
    tm = xb_ref.shape[0]

    def copies(jj, e, s):
        return [pltpu.make_async_copy(
            w_ref.at[layer, e, :, pl.ds(pl.multiple_of((jj + h * nj) * tn, tn), tn)], wbuf.at[s, h], sem.at[s])
            for h in range(2)]

    def consume(s):
        wa_s[...] = wbuf[s, 0].astype(BF16)
        wb_s[...] = wbuf[s, 1].astype(BF16)

    _stream_expert_weights(be_ref, nb_ref, run_ctr, copies, consume)

    @pl.when(active)
    def _():
        x = xb_ref[...]
        brow = be_ref[b] * (2 * nj) + j
        ga = jnp.dot(x, wa_s[...], preferred_element_type=F32) + bias_ref[pl.ds(brow, 1), :]
        gb = jnp.dot(x, wb_s[...], preferred_element_type=F32) + bias_ref[pl.ds(brow + nj, 1), :]
        even = lax.broadcasted_iota(I32, (tm, LANES), 1) % 2 == 0
        for c in range(tn // LANES):
            glu_a, up_a = _swiglu_pairs(ga[:, c * LANES:(c + 1) * LANES])
            glu_b, up_b = _swiglu_pairs(gb[:, c * LANES:(c + 1) * LANES])
            ra = glu_a * pltpu.roll(up_a, LANES - 1, 1)
            rb = pltpu.roll(glu_b, 1, 1) * up_b
            o_ref[:, c * LANES:(c + 1) * LANES] = jnp.where(even, ra, rb).astype(BF16)

    @pl.when(b >= nb_ref[0])
    def _():
        o_ref[...] = jnp.zeros(o_ref.shape, o_ref.dtype)


def _moe_gu(block_e, n_blocks_used, xb, wgu_all, layer, bgu):
    rows, d = xb.shape
    ne, f2 = bgu.shape
    tm = MOE_TM
    tn = 1024
    nj = f2 // 2 // tn
    bias = bgu.reshape(ne * 2 * nj, tn)
    return pl.pallas_call(
        functools.partial(_moe_gu_kernel, layer=layer, tn=tn, nj=nj),
        grid_spec=pltpu.PrefetchScalarGridSpec(
            num_scalar_prefetch=2,
            grid=(nj, rows // tm),
            in_specs=[pl.BlockSpec((tm, d), lambda j, b, be, nb: (b, 0)),
                      pl.BlockSpec(memory_space=pl.ANY),
                      pl.BlockSpec(bias.shape, lambda j, b, be, nb: (0, 0))],
            out_specs=pl.BlockSpec((tm, tn), lambda j, b, be, nb: (b, j)),
            scratch_shapes=[pltpu.VMEM((d, tn), BF16), pltpu.VMEM((d, tn), BF16),
                            pltpu.VMEM((2, 2, d, tn), F32), pltpu.SemaphoreType.DMA((2,)),
                            pltpu.SMEM((1,), I32)]),
        out_shape=jax.ShapeDtypeStruct((rows, f2 // 2), BF16),
        compiler_params=_cparams(("arbitrary", "arbitrary")),
        name="moe_gate_up",
    )(block_e, n_blocks_used, xb, wgu_all, bias)


def _moe_down_kernel(be_ref, nb_ref, a_ref, w_ref, bd_ref, o_ref, wp_s, stage, wbuf, sem, run_ctr, *, layer, tn):
    b = pl.program_id(1)
    active = b < nb_ref[0]

    def copies(j, e, slot):
        return [pltpu.make_async_copy(w_ref.at[layer, e, :, pl.ds(pl.multiple_of(j * tn, tn), tn)], wbuf.at[slot],
                                      sem.at[slot])]

    def consume(slot):
        f = wbuf.shape[1]
        ns = stage.shape[0]
        for c in range(tn // LANES):
            cols = slice(c * LANES, (c + 1) * LANES)
            stage[c % ns, pl.ds(0, f // 2, stride=2), :] = wbuf[slot, :f // 2, cols]
            stage[c % ns, pl.ds(1, f // 2, stride=2), :] = wbuf[slot, f // 2:, cols]
            wp_s[:, cols] = stage[c % ns].astype(BF16)

    _stream_expert_weights(be_ref, nb_ref, run_ctr, copies, consume)

    @pl.when(active)
    def _():
        brow = be_ref[b] * pl.num_programs(0) + pl.program_id(0)
        o_ref[...] = jnp.dot(a_ref[...], wp_s[...], preferred_element_type=F32) + bd_ref[pl.ds(brow, 1), :]

    @pl.when(b >= nb_ref[0])
    def _():
        o_ref[...] = jnp.zeros(o_ref.shape, o_ref.dtype)


def _moe_down(block_e, n_blocks_used, act, wd_all, layer, bd):
    rows, f = act.shape
    ne, d = bd.shape
    tm = MOE_TM
    tn = d
    bias = bd.reshape(ne * (d // tn), tn)
    return pl.pallas_call(
        functools.partial(_moe_down_kernel, layer=layer, tn=tn),
        grid_spec=pltpu.PrefetchScalarGridSpec(
            num_scalar_prefetch=2,
            grid=(d // tn, rows // tm),
            in_specs=[pl.BlockSpec((tm, f), lambda j, b, be, nb: (b, 0)),
                      pl.BlockSpec(memory_space=pl.ANY),
                      pl.BlockSpec(bias.shape, lambda j, b, be, nb: (0, 0))],
            out_specs=pl.BlockSpec((tm, tn), lambda j, b, be, nb: (b, j)),
            scratch_shapes=[pltpu.VMEM((f, tn), BF16), pltpu.VMEM((2, f, LANES), F32),
                            pltpu.VMEM((2, f, tn), F32), pltpu.SemaphoreType.DMA((2,)), pltpu.SMEM((1,), I32)]),
        out_shape=jax.ShapeDtypeStruct((rows, d), F32),
        compiler_params=_cparams(("arbitrary", "arbitrary")),
        name="moe_down",
    )(block_e, n_blocks_used, act, wd_all, bias)


def _combine_kernel(dest_ref, y_ref, x1_ref, gm_ref, tg_ref, fg_ref, o_ref, buf, sem, *, tn, tile0, final_norm):
    i = pl.program_id(0) + tile0

    def issue(t, c):
        for k in range(TOP_K):
            d = dest_ref[(i * tn + t) * TOP_K + k]
            pltpu.make_async_copy(y_ref.at[pl.ds(d, 1), :], buf.at[k, pl.ds(t, 1), :], sem).start(priority=k % 2)
        return c

    lax.fori_loop(0, tn, issue, 0, unroll=8)
    for k in range(TOP_K):
        pltpu.make_async_copy(y_ref.at[pl.ds(0, tn), :], buf.at[k], sem).wait()
    tg = tg_ref[...]
    moe = tg[:, 0:1] * buf[0]
    for k in range(1, TOP_K):
        moe = moe + tg[:, k:k + 1] * buf[k]
    x2 = _gated_residual(x1_ref[...], gm_ref[...], moe)
    if final_norm:
        x2 = _rms(x2, fg_ref[...])
    o_ref[...] = x2


def _combine(dest, y, x1, modg, tgates, fg, group, final_norm, row0, rows):
    d = x1.shape[1]
    tn = ROW_TILE
    ng = tn // group
    t0 = row0 // tn
    return pl.pallas_call(
        functools.partial(_combine_kernel, tn=tn, tile0=t0, final_norm=final_norm),
        grid_spec=pltpu.PrefetchScalarGridSpec(
            num_scalar_prefetch=1,
            grid=(rows // tn,),
            in_specs=[pl.BlockSpec(memory_space=pl.ANY),
                      pl.BlockSpec((tn, d), lambda i, ds: (i + t0, 0)),
                      pl.BlockSpec((ng, d), lambda i, ds: (i + t0, 5)),
                      pl.BlockSpec((tn, LANES), lambda i, ds: (i + t0, 0)),
                      pl.BlockSpec((1, d), lambda i, ds: (0, 0))],
            out_specs=pl.BlockSpec((tn, d), lambda i, ds: (i, 0)),
            scratch_shapes=[pltpu.VMEM((TOP_K, tn, d), F32), pltpu.SemaphoreType.DMA(())]),
        out_shape=jax.ShapeDtypeStruct((rows, d), F32),
        compiler_params=_cparams(("arbitrary",)),
        name="moe_combine",
    )(dest, y, x1, modg, tgates, fg.reshape(1, d))


def _moe(hp, te128, tg128, pos128, cnt, x1, modg, wgu_all, bgu, wd_all, bd, layer, fg, group, splits):
    n = x1.shape[0]
    rows_total = n * TOP_K + N_EXPERTS * MOE_TM
    n_blocks = rows_total // MOE_TM
    dest2d, meta = _moe_dest(te128, pos128, cnt, n_blocks)
    dest = dest2d.reshape(-1)
    block_e = meta[0, :n_blocks]
    nbu = meta[0, n_blocks:n_blocks + 1]
    row_tok = _row_token(dest, rows_total)
    xb = _dispatch(row_tok, nbu, hp, rows_total)
    act = _moe_gu(block_e, nbu, xb, wgu_all, layer, bgu)
    y = _moe_down(block_e, nbu, act, wd_all, layer, bd)
    return [_combine(dest, y, x1, modg, tg128, fg, group, fn, r0, rows) for r0, rows, fn in splits]


def _lru_in_kernel(x_ref, g1_ref, sh_ref, sc_ref, wy_ref, wx_ref, y_ref, xb_ref):
    h = _norm_mod(x_ref[...], g1_ref[...], sh_ref[...], sc_ref[...]).astype(BF16)
    y = jnp.dot(h, wy_ref[...], preferred_element_type=F32)
    y_ref[...] = jax.nn.gelu(y, approximate=True).astype(BF16)
    xb_ref[...] = jnp.dot(h, wx_ref[...], preferred_element_type=F32)


def _lru_in(x, g1, modg, wy, wx, group):
    n, d = x.shape
    dr = wy.shape[1]
    tm = ROW_TILE
    ng = tm // group
    row = lambda i: (i, 0)
    const = lambda i: (0, 0)
    return pl.pallas_call(
        _lru_in_kernel,
        grid=(n // tm,),
        in_specs=[pl.BlockSpec((tm, d), row),
                  pl.BlockSpec((1, d), const),
                  pl.BlockSpec((ng, d), lambda i: (i, 0)),
                  pl.BlockSpec((ng, d), lambda i: (i, 1)),
                  pl.BlockSpec((d, dr), const),
                  pl.BlockSpec((d, dr), const)],
        out_specs=[pl.BlockSpec((tm, dr), row), pl.BlockSpec((tm, dr), row)],
        out_shape=[jax.ShapeDtypeStruct((n, dr), BF16), jax.ShapeDtypeStruct((n, dr), F32)],
        compiler_params=_cparams(("arbitrary",)),
        name="lru_in",
    )(x, g1.reshape(1, d), modg, modg, wy, wx)


def _lru_scan_kernel(y_ref, xb_ref, cb_ref, h0_ref, cw_ref, cbias_ref, wa_ref, ba_ref, wx_ref, bx_ref, lam_ref,
                     hy_ref, cbo_ref, ho_ref, xe, a_s, u_s, hc, *, tc, starts_at_pos0):
    c = pl.program_id(1)
    dr = xb_ref.shape[1]
    nb = wa_ref.shape[0]
    bd = dr // nb
    pre = SUBLANES

    @pl.when(c == 0)
    def _():
        xe[0:pre, :] = jnp.zeros((pre, dr), F32)
        xe[pre - (CONV_W - 1):pre, :] = cb_ref[0]
        hc[...] = jnp.broadcast_to(h0_ref[0], (SUBLANES, dr))

    xe[pre:pre + tc, :] = xb_ref[...]
    cw = cw_ref[...]
    xc = cbias_ref[...] + xe[pre:pre + tc, :] * cw[CONV_W - 1:CONV_W, :]
    for k in range(1, CONV_W):
        xc = xc + xe[pre - k:pre - k + tc, :] * cw[CONV_W - 1 - k:CONV_W - k, :]
    cbo_ref[0] = xe[pre + tc - (CONV_W - 1):pre + tc, :]
    xe[0:pre, :] = xe[tc:tc + pre, :]

    xcb = xc.astype(BF16)
    ra = jnp.concatenate([jnp.dot(xcb[:, n * bd:(n + 1) * bd], wa_ref[n], preferred_element_type=F32)
                          for n in range(nb)], axis=1)
    rx = jnp.concatenate([jnp.dot(xcb[:, n * bd:(n + 1) * bd], wx_ref[n], preferred_element_type=F32)
                          for n in range(nb)], axis=1)
    r = jax.nn.sigmoid(ra + ba_ref[...])
    ig = jax.nn.sigmoid(rx + bx_ref[...])
    lam = lam_ref[...]
    log_sig = jnp.minimum(lam, 0.0) - jnp.log1p(jnp.exp(-jnp.abs(lam)))
    log_a = LRU_C * r * log_sig
    a = jnp.exp(log_a)
    th = jnp.tanh(log_a)
    mult = jnp.sqrt(-2.0 * th / (1.0 - th))
    if starts_at_pos0:
        first = (lax.broadcasted_iota(I32, (tc, 1), 0) == 0) & (c == 0)
        mult = jnp.where(first, 1.0, mult)
    a_s[...] = a
    u_s[...] = mult * ig * xc

    row8 = lax.broadcasted_iota(I32, (SUBLANES, dr), 0)

    def group_step(g, hprev):
        off = pl.multiple_of(g * SUBLANES, SUBLANES)
        aa = a_s[pl.ds(off, SUBLANES), :]
        uu = u_s[pl.ds(off, SUBLANES), :]
        s = 1
        while s < SUBLANES:
            m = row8 >= s
            uu = jnp.where(m, aa * pltpu.roll(uu, s, 0) + uu, uu)
            aa = jnp.where(m, aa * pltpu.roll(aa, s, 0), aa)
            s *= 2
        hh = aa * hprev + uu
        u_s[pl.ds(off, SUBLANES), :] = hh
        return jnp.broadcast_to(hh[SUBLANES - 1:SUBLANES, :], (SUBLANES, dr))

    hlast = lax.fori_loop(0, tc // SUBLANES, group_step, hc[...])
    hc[...] = hlast
    ho_ref[0] = hlast[0:1, :]
    hy_ref[...] = (u_s[...] * y_ref[...].astype(F32)).astype(BF16)


def _lru_scan(yb, xb, conv_buf, h0, cw, cbias, wa, ba, wx, bx, lam, row0, n_seq, t, tc, starts_at_pos0):
    dr = xb.shape[1]
    nc = t // tc
    rb0 = row0 // tc
    inmap = lambda s, c: (rb0 + s * nc + c, 0)
    outmap = lambda s, c: (s * nc + c, 0)
    const2 = lambda s, c: (0, 0)
    const3 = lambda s, c: (0, 0, 0)
    seq3 = lambda s, c: (s, 0, 0)
    return pl.pallas_call(
        functools.partial(_lru_scan_kernel, tc=tc, starts_at_pos0=starts_at_pos0),
        grid=(n_seq, nc),
        in_specs=[pl.BlockSpec((tc, dr), inmap),
                  pl.BlockSpec((tc, dr), inmap),
                  pl.BlockSpec((1, CONV_W - 1, dr), seq3),
                  pl.BlockSpec((1, 1, dr), seq3),
                  pl.BlockSpec((CONV_W, dr), const2),
                  pl.BlockSpec((1, dr), const2),
                  pl.BlockSpec(wa.shape, const3),
                  pl.BlockSpec((1, dr), const2),
                  pl.BlockSpec(wx.shape, const3),
                  pl.BlockSpec((1, dr), const2),
                  pl.BlockSpec((1, dr), const2)],
        out_specs=[pl.BlockSpec((tc, dr), outmap),
                   pl.BlockSpec((1, CONV_W - 1, dr), seq3),
                   pl.BlockSpec((1, 1, dr), seq3)],
        out_shape=[jax.ShapeDtypeStruct((n_seq * t, dr), BF16),
                   jax.ShapeDtypeStruct((n_seq, CONV_W - 1, dr), F32),
                   jax.ShapeDtypeStruct((n_seq, 1, dr), F32)],
        scratch_shapes=[pltpu.VMEM((SUBLANES + tc, dr), F32),
                        pltpu.VMEM((tc, dr), F32),
                        pltpu.VMEM((tc, dr), F32),
                        pltpu.VMEM((SUBLANES, dr), F32)],
        compiler_params=_cparams(("arbitrary", "arbitrary")),
        name="lru_scan",
    )(yb, xb, conv_buf, h0.reshape(n_seq, 1, dr), cw, cbias.reshape(1, dr), wa, ba.reshape(1, dr),
      wx, bx.reshape(1, dr), lam.reshape(1, dr))


def _rope_tables(pos):
    half = ROPE_DIM // 2
    inv = 1.0 / (ROPE_THETA ** (jnp.arange(0, ROPE_DIM, 2, dtype=F32) / ROPE_DIM))
    ang = pos.astype(F32)[:, None] * inv[None, :]
    cos, sin = jnp.cos(ang), jnp.sin(ang)
    z = jnp.zeros((pos.shape[0], LANES - ROPE_DIM), F32)
    return jnp.concatenate([cos, cos, z], axis=1), jnp.concatenate([-sin, sin, z], axis=1)


def _head_slab_weights(w_uq):
    ql = w_uq.shape[0]
    w = w_uq.reshape(ql, N_HEADS, NOPE_DIM + ROPE_DIM)
    z = jnp.zeros((ql, N_HEADS, HEAD_W - NOPE_DIM - ROPE_DIM), w.dtype)
    return jnp.concatenate([w, z], axis=2).reshape(ql, N_HEADS * HEAD_W).astype(BF16)


def kernel(x_prompt, x_sample, cache_ckv, cache_krope, state_conv, state_h, c_prompt, c_sample,
           mod_w, mod_b, norm1_g, norm2_g,
           mla_w_in, mla_q_norm_g, mla_kv_norm_g, mla_w_uq, mla_w_ukv, mla_w_o,
           lru_w_in, lru_conv_w, lru_conv_b, lru_w_a, lru_b_a, lru_w_x, lru_b_x, lru_lambda, lru_w_o,
           router_w, router_b, moe_w_gu, moe_b_gu, moe_w_down, moe_b_down, final_g):
    bp, tp, d = x_prompt.shape
    bs, ts, _ = x_sample.shape
    past = cache_ckv.shape[2]
    depth = mod_w.shape[0]
    n_p, n_s = bp * tp, bs * ts
    n = n_p + n_s
    group = math.gcd(tp, ts)
    assert group % SUBLANES == 0 and ROW_TILE % group == 0 and n_p % ROW_TILE == 0 and n_s % ROW_TILE == 0
    assert NOPE_DIM == LANES and V_DIM == LANES and ROPE_DIM <= LANES and n_p % n_s == 0

    x = jnp.concatenate([x_prompt.reshape(n_p, d), x_sample.reshape(n_s, d)], axis=0)
    nb = bp + bs
    nb_pad = -(-nb // SUBLANES) * SUBLANES
    c_all = jnp.concatenate([c_prompt, c_sample, jnp.zeros((nb_pad - nb, d), F32)], axis=0)
    grp_batch = np.concatenate([np.repeat(np.arange(bp), tp // group), bp + np.repeat(np.arange(bs), ts // group)])
    pos = jnp.concatenate([jnp.tile(jnp.arange(tp), bp), jnp.tile(past + jnp.arange(ts), bs)])
    cos, sin = _rope_tables(pos)

    outs = {k: [] for k in ("ckv_p", "kr_p", "conv_p", "h_p", "ckv_s", "kr_s", "conv_s", "h_s")}
    for i in range(depth):
        mod = _adaln(c_all, mod_w, i, mod_b[i])
        modg = jnp.take(mod, jnp.asarray(grp_batch), axis=0)
        j = i // 2
        if i % 2 == 0:
            w_in = mla_w_in[j]
            zpad = jnp.zeros((d, LANES - ROPE_DIM), F32)
            win = jnp.concatenate([w_in, zpad], axis=1).astype(BF16)
            wuq = _head_slab_weights(mla_w_uq[j])
            wukv = mla_w_ukv[j].reshape(KV_LORA, N_HEADS, NOPE_DIM + V_DIM)
            wk = wukv[:, :, :NOPE_DIM].reshape(KV_LORA, N_HEADS * NOPE_DIM).astype(BF16)
            wv = wukv[:, :, NOPE_DIM:].reshape(KV_LORA, N_HEADS * V_DIM).astype(BF16)
            q, ckv, kr, ckvb, krp = _mla_proj(x, norm1_g[i], modg, win, mla_q_norm_g[j], mla_kv_norm_g[j],
                                              wuq, cos, sin, group)
            kk, vv = _kv_expand(ckvb, krp, wk, wv, n_p)
            o_p = _attn_prompt(q, kk, vv, bp, tp)
            qa = _absorb(q, wk, n_p, n_s)
            ckr_pad = jnp.pad(cache_krope[j], ((0, 0), (0, 0), (0, LANES - ROPE_DIM)))
            o_lat = _attn_sample(qa, q, cache_ckv[j], ckr_pad, ckvb, krp, n_p, bs, ts, past)
            o_s = _unabsorb(o_lat, wv, n_s)
            wo = mla_w_o[j].astype(BF16)
            outs["ckv_p"].append(ckv[:n_p].reshape(bp, tp, KV_LORA))
            outs["kr_p"].append(kr[:n_p].reshape(bp, tp, ROPE_DIM))
            outs["ckv_s"].append(ckv[n_p:].reshape(bs, ts, KV_LORA))
            outs["kr_s"].append(kr[n_p:].reshape(bs, ts, ROPE_DIM))
        else:
            dr = lru_w_in.shape[2] // 2
            wy = lru_w_in[j][:, :dr].astype(BF16)
            wx = lru_w_in[j][:, dr:].astype(BF16)
            yb, xb = _lru_in(x, norm1_g[i], modg, wy, wx, group)
            wa = lru_w_a[j].astype(BF16)
            wxg = lru_w_x[j].astype(BF16)
            lru_args = (lru_conv_w[j], lru_conv_b[j], wa, lru_b_a[j], wxg, lru_b_x[j], lru_lambda[j])
            zbuf = jnp.zeros((bp, CONV_W - 1, dr), F32)
            zh = jnp.zeros((bp, dr), F32)
            o_p, cb_p, h_p = _lru_scan(yb, xb, zbuf, zh, *lru_args, 0, bp, tp, ROW_TILE, True)
            o_s, cb_s, h_s = _lru_scan(yb, xb, state_conv[j], state_h[j], *lru_args, n_p, bs, ts, ts, False)
            wo = lru_w_o[j].astype(BF16)
            outs["conv_p"].append(cb_p)
            outs["h_p"].append(h_p.reshape(bp, dr))
            outs["conv_s"].append(cb_s)
            outs["h_s"].append(h_s.reshape(bs, dr))
        x1, hp, te128, tg128, pos128, cnt = _post_mixer(o_p, o_s, wo, x, norm2_g[i], modg, router_w[i], router_b[i],
                                                        group)
        last = i == depth - 1
        splits = [(0, n_p, True), (n_p, n_s, True)] if last else [(0, n, False)]
        res = _moe(hp, te128, tg128, pos128, cnt, x1, modg, moe_w_gu, moe_b_gu[i], moe_w_down, moe_b_down[i], i,
                   final_g, group, splits)
        x = res[0]
    y_prompt = res[0].reshape(bp, tp, d)
    y_sample = res[1].reshape(bs, ts, d)
    return (y_prompt, y_sample,
            jnp.stack(outs["ckv_p"]), jnp.stack(outs["kr_p"]), jnp.stack(outs["conv_p"]), jnp.stack(outs["h_p"]),
            jnp.stack(outs["ckv_s"]), jnp.stack(outs["kr_s"]), jnp.stack(outs["conv_s"]), jnp.stack(outs["h_s"]))
```

```python
import functools
import math

import jax
import jax.numpy as jnp
import numpy as np
from jax import lax
from jax.experimental import pallas as pl
from jax.experimental.pallas import tpu as pltpu

F32 = jnp.float32
BF16 = jnp.bfloat16
I32 = jnp.int32

CHUNK = 64
N_HEADS = 16
Q_LORA = 512
KV_LORA = 512
NOPE_DIM = 128
ROPE_DIM = 64
V_DIM = 128
ROPE_THETA = 10000.0
LRU_BLOCKS = 8
CONV_W = 4
LRU_C = 8.0
N_EXPERTS = 32
TOP_K = 4
SWIGLU_LIMIT = 7.0
SWIGLU_ALPHA = 1.702
N_MOD = 6
EPS = 1e-6

LANES = 128
SUBLANES = 8
HEAD_W = 2 * LANES

ROW_TILE = 256
MOE_TM = 256
VMEM_LIMIT = 56 * 1024 * 1024
VMEM_LIMIT_DOWN = 60 * 1024 * 1024


def _cparams(sem, vmem_limit=VMEM_LIMIT):
    return pltpu.CompilerParams(dimension_semantics=sem, vmem_limit_bytes=vmem_limit)


def _rms(x, g):
    ms = jnp.mean(x * x, axis=-1, keepdims=True)
    return x * lax.rsqrt(ms + EPS) * g


def _norm_mod(x, g, shift, scale):
    tm, d = x.shape
    ng = shift.shape[0]
    y = _rms(x, g).reshape(ng, tm // ng, d)
    return (y * (1.0 + scale[:, None, :]) + shift[:, None, :]).reshape(tm, d)


def _gated_residual(x, gate, y):
    tm, d = x.shape
    ng = gate.shape[0]
    return (x.reshape(ng, tm // ng, d) + gate[:, None, :] * y.reshape(ng, tm // ng, d)).reshape(tm, d)


def _adaln_kernel(c_ref, w_ref, b_ref, o_ref):
    c = c_ref[...]
    a = (c * jax.nn.sigmoid(c)).astype(BF16)
    o_ref[...] = jnp.dot(a, w_ref[0].astype(BF16), preferred_element_type=F32) + b_ref[...]


def _adaln(c_all, w_all, layer, b):
    bp, d = c_all.shape
    n = w_all.shape[2]
    tn = 1024
    return pl.pallas_call(
        _adaln_kernel,
        grid=(n // tn,),
        in_specs=[pl.BlockSpec((bp, d), lambda j: (0, 0)),
                  pl.BlockSpec((1, d, tn), lambda j: (layer, 0, j)),
                  pl.BlockSpec((1, tn), lambda j: (0, j))],
        out_specs=pl.BlockSpec((bp, tn), lambda j: (0, j)),
        out_shape=jax.ShapeDtypeStruct((bp, n), F32),
        compiler_params=_cparams(("arbitrary",)),
        name="adaln",
    )(c_all, w_all, b.reshape(1, n))


def _rope128(v, cos, sin):
    half = ROPE_DIM // 2
    lane = lax.broadcasted_iota(I32, v.shape, 1)
    sw = jnp.where(lane < half, pltpu.roll(v, LANES - half, 1), pltpu.roll(v, half, 1))
    return v * cos + sw * sin


def _mla_proj_kernel(x_ref, g1_ref, sh_ref, sc_ref, win_ref, qg_ref, kvg_ref, wuq_ref, cos_ref, sin_ref,
                     q_ref, ckv_ref, kr_ref, ckvb_ref, krp_ref):
    h = _norm_mod(x_ref[...], g1_ref[...], sh_ref[...], sc_ref[...]).astype(BF16)
    lat = jnp.dot(h, win_ref[...], preferred_element_type=F32)
    q_lat = lat[:, :Q_LORA]
    c_kv = lat[:, Q_LORA:Q_LORA + KV_LORA]
    k_r = lat[:, Q_LORA + KV_LORA:]
    qn = _rms(q_lat, qg_ref[...]).astype(BF16)
    q = jnp.dot(qn, wuq_ref[...], preferred_element_type=F32)
    cos = cos_ref[...]
    sin = sin_ref[...]
    scale = (NOPE_DIM + ROPE_DIM) ** -0.5
    for hh in range(N_HEADS):
        lo = hh * HEAD_W
        q_ref[:, lo:lo + LANES] = (q[:, lo:lo + LANES] * scale).astype(BF16)
        q_ref[:, lo + LANES:lo + HEAD_W] = (_rope128(q[:, lo + LANES:lo + HEAD_W], cos, sin) * scale).astype(BF16)
    ckv = _rms(c_kv, kvg_ref[...])
    ckv_ref[...] = ckv
    ckvb_ref[...] = ckv.astype(BF16)
    kr = _rope128(k_r, cos, sin)
    kr_ref[...] = kr[:, :ROPE_DIM]
    krp_ref[...] = kr.astype(BF16)


def _mla_proj(x, g1, modg, win, qg, kvg, wuq, cos, sin, group):
    n, d = x.shape
    tm = ROW_TILE
    ng = tm // group
    wl = win.shape[1]
    qw = wuq.shape[1]
    row = lambda i: (i, 0)
    const = lambda i: (0, 0)
    return pl.pallas_call(
        _mla_proj_kernel,
        grid=(n // tm,),
        in_specs=[pl.BlockSpec((tm, d), row),
                  pl.BlockSpec((1, d), const),
                  pl.BlockSpec((ng, d), lambda i: (i, 0)),
                  pl.BlockSpec((ng, d), lambda i: (i, 1)),
                  pl.BlockSpec((d, wl), const),
                  pl.BlockSpec((1, Q_LORA), const),
                  pl.BlockSpec((1, KV_LORA), const),
                  pl.BlockSpec((Q_LORA, qw), const),
                  pl.BlockSpec((tm, LANES), row),
                  pl.BlockSpec((tm, LANES), row)],
        out_specs=[pl.BlockSpec((tm, qw), row),
                   pl.BlockSpec((tm, KV_LORA), row),
                   pl.BlockSpec((tm, ROPE_DIM), row),
                   pl.BlockSpec((tm, KV_LORA), row),
                   pl.BlockSpec((tm, LANES), row)],
        out_shape=[jax.ShapeDtypeStruct((n, qw), BF16),
                   jax.ShapeDtypeStruct((n, KV_LORA), F32),
                   jax.ShapeDtypeStruct((n, ROPE_DIM), F32),
                   jax.ShapeDtypeStruct((n, KV_LORA), BF16),
                   jax.ShapeDtypeStruct((n, LANES), BF16)],
        compiler_params=_cparams(("arbitrary",)),
        name="mla_proj",
    )(x, g1.reshape(1, d), modg, modg, win, qg.reshape(1, -1), kvg.reshape(1, -1), wuq, cos, sin)


def _kv_expand_kernel(c_ref, krp_ref, wk_ref, wv_ref, k_ref, v_ref):
    c = c_ref[...]
    kn = jnp.dot(c, wk_ref[...], preferred_element_type=F32).astype(BF16)
    krp = krp_ref[...]
    for hh in range(N_HEADS):
        k_ref[:, hh * HEAD_W:hh * HEAD_W + LANES] = kn[:, hh * NOPE_DIM:(hh + 1) * NOPE_DIM]
        k_ref[:, hh * HEAD_W + LANES:(hh + 1) * HEAD_W] = krp
    v_ref[...] = jnp.dot(c, wv_ref[...], preferred_element_type=F32).astype(BF16)


def _kv_expand(ckvb, krp, wk, wv, rows):
    tm = 512
    row = lambda i: (i, 0)
    const = lambda i: (0, 0)
    return pl.pallas_call(
        _kv_expand_kernel,
        grid=(rows // tm,),
        in_specs=[pl.BlockSpec((tm, KV_LORA), row),
                  pl.BlockSpec((tm, LANES), row),
                  pl.BlockSpec(wk.shape, const),
                  pl.BlockSpec(wv.shape, const)],
        out_specs=[pl.BlockSpec((tm, N_HEADS * HEAD_W), row),
                   pl.BlockSpec((tm, N_HEADS * V_DIM), row)],
        out_shape=[jax.ShapeDtypeStruct((rows, N_HEADS * HEAD_W), BF16),
                   jax.ShapeDtypeStruct((rows, N_HEADS * V_DIM), BF16)],
        compiler_params=_cparams(("arbitrary",)),
        name="kv_expand",
    )(ckvb, krp, wk, wv)


ATTN_HEADS_PER_STEP = 2


def _attn_prompt_kernel(q_ref, k_ref, v_ref, o_ref, *, tq, nq):
    qi = pl.program_id(2)
    dn = (((1,), (1,)), ((), ()))
    hs = ATTN_HEADS_PER_STEP
    r = lax.broadcasted_iota(I32, (tq, tq), 0) // CHUNK
    c = lax.broadcasted_iota(I32, (tq, tq), 1) // CHUNK
    diag_visible = c <= r
    for qs in range(nq):
        @pl.when(qi == qs)
        def _(qs=qs):
            past = qs * tq
            for h in range(hs):
                q = q_ref[:, h * HEAD_W:(h + 1) * HEAD_W]
                kcols = slice(h * HEAD_W, (h + 1) * HEAD_W)
                vcols = slice(h * V_DIM, (h + 1) * V_DIM)
                s_d = lax.dot_general(q, k_ref[past:past + tq, kcols], dn, preferred_element_type=F32)
                s_d = jnp.where(diag_visible, s_d, -jnp.inf)
                m = jnp.max(s_d, axis=-1, keepdims=True)
                if past:
                    s_f = lax.dot_general(q, k_ref[0:past, kcols], dn, preferred_element_type=F32)
                    m = jnp.maximum(m, jnp.max(s_f, axis=-1, keepdims=True))
                p_d = jnp.exp(s_d - m)
                l = jnp.sum(p_d, axis=-1, keepdims=True)
                acc = jnp.dot(p_d.astype(BF16), v_ref[past:past + tq, vcols], preferred_element_type=F32)
                if past:
                    p_f = jnp.exp(s_f - m)
                    l = l + jnp.sum(p_f, axis=-1, keepdims=True)
                    acc = acc + jnp.dot(p_f.astype(BF16), v_ref[0:past, vcols], preferred_element_type=F32)
                o_ref[:, vcols] = (acc / l).astype(BF16)


def _attn_prompt(q, k, v, bp, tp):
    n_rows = bp * tp
    tq = 256
    nq = tp // tq
    hs = ATTN_HEADS_PER_STEP
    return pl.pallas_call(
        functools.partial(_attn_prompt_kernel, tq=tq, nq=nq),
        grid=(bp, N_HEADS // hs, nq),
        in_specs=[pl.BlockSpec((tq, hs * HEAD_W), lambda b, h, i: (b * nq + i, h)),
                  pl.BlockSpec((tp, hs * HEAD_W), lambda b, h, i: (b, h)),
                  pl.BlockSpec((tp, hs * V_DIM), lambda b, h, i: (b, h))],
        out_specs=pl.BlockSpec((tq, hs * V_DIM), lambda b, h, i: (b * nq + i, h)),
        out_shape=jax.ShapeDtypeStruct((n_rows, N_HEADS * V_DIM), BF16),
        compiler_params=_cparams(("arbitrary", "arbitrary", "arbitrary")),
        name="attn_prompt",
    )(q, k, v)


def _absorb_kernel(q_ref, wk_ref, o_ref):
    dn = (((1,), (1,)), ((), ()))
    o_ref[0] = lax.dot_general(q_ref[...], wk_ref[...], dn, preferred_element_type=F32).astype(BF16)


def _absorb(q, wk, row0, rows):
    rb = row0 // rows
    return pl.pallas_call(
        _absorb_kernel,
        grid=(N_HEADS,),
        in_specs=[pl.BlockSpec((rows, LANES), lambda h: (rb, 2 * h)),
                  pl.BlockSpec((KV_LORA, NOPE_DIM), lambda h: (0, h))],
        out_specs=pl.BlockSpec((1, rows, KV_LORA), lambda h: (h, 0, 0)),
        out_shape=jax.ShapeDtypeStruct((N_HEADS, rows, KV_LORA), BF16),
        compiler_params=_cparams(("arbitrary",)),
        name="absorb",
    )(q, wk)


def _attn_sample_kernel(qa_ref, q_ref, cc_ref, ckr_ref, cn_ref, krn_ref, o_ref, *, ts, past):
    hn = N_HEADS
    qa = qa_ref[...].reshape(hn * ts, KV_LORA)
    qfull = q_ref[...]
    qr = jnp.concatenate([qfull[:, h * HEAD_W + LANES:(h + 1) * HEAD_W] for h in range(hn)], axis=0)
    cc = cc_ref[0].astype(BF16)
    ckr = ckr_ref[0].astype(BF16)
    cn = cn_ref[...]
    krn = krn_ref[...]
    dn = (((1,), (1,)), ((), ()))
    s_c = (lax.dot_general(qa, cc, dn, preferred_element_type=F32)
           + lax.dot_general(qr, ckr, dn, preferred_element_type=F32))
    s_n = (lax.dot_general(qa, cn, dn, preferred_element_type=F32)
           + lax.dot_general(qr, krn, dn, preferred_element_type=F32))
    qchunk_c = (past + lax.broadcasted_iota(I32, s_c.shape, 0) % ts) // CHUNK
    s_c = jnp.where(lax.broadcasted_iota(I32, s_c.shape, 1) // CHUNK <= qchunk_c, s_c, -jnp.inf)
    qchunk_n = (past + lax.broadcasted_iota(I32, s_n.shape, 0) % ts) // CHUNK
    s_n = jnp.where((past + lax.broadcasted_iota(I32, s_n.shape, 1)) // CHUNK <= qchunk_n, s_n, -jnp.inf)
    m = jnp.maximum(jnp.max(s_c, axis=-1, keepdims=True), jnp.max(s_n, axis=-1, keepdims=True))
    p_c = jnp.exp(s_c - m)
    p_n = jnp.exp(s_n - m)
    l = jnp.sum(p_c, axis=-1, keepdims=True) + jnp.sum(p_n, axis=-1, keepdims=True)
    o = (jnp.dot(p_c.astype(BF16), cc, preferred_element_type=F32)
         + jnp.dot(p_n.astype(BF16), cn, preferred_element_type=F32)) / l
    o_ref[...] = o.astype(BF16).reshape(hn, ts, KV_LORA)


def _attn_sample(qa, q, cache_c, cache_kr, ckvb, krp, row0, bs, ts, past):
    rb0 = row0 // ts
    return pl.pallas_call(
        functools.partial(_attn_sample_kernel, ts=ts, past=past),
        grid=(bs,),
        in_specs=[pl.BlockSpec((N_HEADS, ts, KV_LORA), lambda b: (0, b, 0)),
                  pl.BlockSpec((ts, N_HEADS * HEAD_W), lambda b: (rb0 + b, 0)),
                  pl.BlockSpec((1, past, KV_LORA), lambda b: (b, 0, 0)),
                  pl.BlockSpec((1, past, LANES), lambda b: (b, 0, 0)),
                  pl.BlockSpec((ts, KV_LORA), lambda b: (rb0 + b, 0)),
                  pl.BlockSpec((ts, LANES), lambda b: (rb0 + b, 0))],
        out_specs=pl.BlockSpec((N_HEADS, ts, KV_LORA), lambda b: (0, b, 0)),
        out_shape=jax.ShapeDtypeStruct((N_HEADS, bs * ts, KV_LORA), BF16),
        compiler_params=_cparams(("arbitrary",)),
        name="attn_sample",
    )(qa, q, cache_c, cache_kr, ckvb, krp)


def _unabsorb_kernel(ol_ref, wv_ref, o_ref):
    o_ref[...] = jnp.dot(ol_ref[0], wv_ref[...], preferred_element_type=F32).astype(BF16)


def _unabsorb(o_lat, wv, rows):
    return pl.pallas_call(
        _unabsorb_kernel,
        grid=(N_HEADS,),
        in_specs=[pl.BlockSpec((1, rows, KV_LORA), lambda h: (h, 0, 0)),
                  pl.BlockSpec((KV_LORA, V_DIM), lambda h: (0, h))],
        out_specs=pl.BlockSpec((rows, V_DIM), lambda h: (0, h)),
        out_shape=jax.ShapeDtypeStruct((rows, N_HEADS * V_DIM), BF16),
        compiler_params=_cparams(("arbitrary",)),
        name="unabsorb",
    )(o_lat, wv)


def _post_mixer_kernel(op_ref, os_ref, wo_ref, x_ref, gate_ref, g2_ref, sh_ref, sc_ref, rw_ref, rb_ref,
                       x1_ref, hp_ref, te_ref, tg_ref, pos_ref, cnt_ref, carry, *, prompt_tiles):
    @pl.when(pl.program_id(0) == 0)
    def _():
        carry[...] = jnp.zeros(carry.shape, F32)

    o = jnp.where(pl.program_id(0) < prompt_tiles, op_ref[...], os_ref[...])
    y = jnp.dot(o, wo_ref[...], preferred_element_type=F32)
    x1 = _gated_residual(x_ref[...], gate_ref[...], y)
    x1_ref[...] = x1
    h2 = _norm_mod(x1, g2_ref[...], sh_ref[...], sc_ref[...])
    hp_ref[...] = h2
    rw = rw_ref[...]
    h_hi = h2.astype(BF16)
    h_lo = (h2 - h_hi.astype(F32)).astype(BF16)
    w_hi = rw.astype(BF16)
    w_lo = (rw - w_hi.astype(F32)).astype(BF16)
    logits = (jnp.dot(h_hi, w_hi, preferred_element_type=F32) + jnp.dot(h_lo, w_hi, preferred_element_type=F32)
              + jnp.dot(h_hi, w_lo, preferred_element_type=F32) + rb_ref[...])
    tm, ne = logits.shape
    eid = lax.broadcasted_iota(I32, (tm, ne), 1)
    lane = lax.broadcasted_iota(I32, (tm, LANES), 1)
    te = jnp.zeros((tm, LANES), I32)
    tv = jnp.full((tm, LANES), -jnp.inf, F32)
    work = logits
    picks = []
    for k in range(TOP_K):
        mx = jnp.max(work, axis=-1, keepdims=True)
        idx = jnp.min(jnp.where(work == mx, eid, ne), axis=-1, keepdims=True)
        picks.append(idx)
        te = jnp.where(lane == k, idx, te)
        tv = jnp.where(lane == k, mx, tv)
        work = jnp.where(eid == idx, -jnp.inf, work)
    ex = jnp.exp(tv - jnp.max(tv, axis=-1, keepdims=True))
    te_ref[...] = te
    tg_ref[...] = ex / jnp.sum(ex, axis=-1, keepdims=True)
    onehot = jnp.zeros((tm, LANES), F32)
    for idx in picks:
        onehot = onehot + (lane == idx).astype(F32)
    tri = (lax.broadcasted_iota(I32, (tm, tm), 1) < lax.broadcasted_iota(I32, (tm, tm), 0)).astype(BF16)
    rank = jnp.dot(tri, onehot.astype(BF16), preferred_element_type=F32) + carry[0:1, :]
    pos = jnp.zeros((tm, LANES), I32)
    for k, idx in enumerate(picks):
        pk = jnp.sum(jnp.where(lane == idx, rank, 0.0), axis=-1, keepdims=True)
        pos = jnp.where(lane == k, pk.astype(I32), pos)
    pos_ref[...] = pos
    total = carry[0:1, :] + jnp.sum(onehot, axis=0, keepdims=True)
    carry[...] = jnp.broadcast_to(total, carry.shape)
    cnt_ref[...] = carry[...]


def _post_mixer(o_p, o_s, wo, x, g2, modg, rw, rb, group):
    n, d = x.shape
    tm = ROW_TILE
    ng = tm // group
    npt = o_p.shape[0] // tm
    nst = o_s.shape[0] // tm
    row = lambda i: (i, 0)
    const = lambda i: (0, 0)
    return pl.pallas_call(
        functools.partial(_post_mixer_kernel, prompt_tiles=npt),
        grid=(n // tm,),
        in_specs=[pl.BlockSpec((tm, o_p.shape[1]), lambda i: (jnp.minimum(i, npt - 1), 0)),
                  pl.BlockSpec((tm, o_s.shape[1]), lambda i: (jnp.clip(i - npt, 0, nst - 1), 0)),
                  pl.BlockSpec(wo.shape, const),
                  pl.BlockSpec((tm, d), row),
                  pl.BlockSpec((ng, d), lambda i: (i, 2)),
                  pl.BlockSpec((1, d), const),
                  pl.BlockSpec((ng, d), lambda i: (i, 3)),
                  pl.BlockSpec((ng, d), lambda i: (i, 4)),
                  pl.BlockSpec(rw.shape, const),
                  pl.BlockSpec((1, rw.shape[1]), const)],
        out_specs=[pl.BlockSpec((tm, d), row),
                   pl.BlockSpec((tm, d), row),
                   pl.BlockSpec((tm, LANES), row),
                   pl.BlockSpec((tm, LANES), row),
                   pl.BlockSpec((tm, LANES), row),
                   pl.BlockSpec((SUBLANES, LANES), const)],
        out_shape=[jax.ShapeDtypeStruct((n, d), F32),
                   jax.ShapeDtypeStruct((n, d), F32),
                   jax.ShapeDtypeStruct((n, LANES), I32),
                   jax.ShapeDtypeStruct((n, LANES), F32),
                   jax.ShapeDtypeStruct((n, LANES), I32),
                   jax.ShapeDtypeStruct((SUBLANES, LANES), F32)],
        scratch_shapes=[pltpu.VMEM((SUBLANES, LANES), F32)],
        compiler_params=_cparams(("arbitrary",)),
        name="post_mixer",
    )(o_p, o_s, wo, x, modg, g2.reshape(1, d), modg, modg, rw, rb.reshape(1, -1))


DEST_GROUP = LANES // TOP_K


def _moe_dest_kernel(te_ref, pos_ref, cnt_ref, dest_ref, meta_ref, *, n_blocks):
    shift = MOE_TM.bit_length() - 1
    lane8 = lax.broadcasted_iota(I32, (SUBLANES, LANES), 1)
    cnt = cnt_ref[...].astype(I32)
    padded = ((cnt + (MOE_TM - 1)) >> shift) << shift
    ends = padded.astype(F32)
    s = 1
    while s < N_EXPERTS:
        ends = ends + jnp.where(lane8 >= s, pltpu.roll(ends, s, 1), 0.0)
        s *= 2
    ends_row = ends[0:1, :]
    starts_row = (ends - padded.astype(F32))[0:1, :]
    te = te_ref[...]
    pos = pos_ref[...]
    tm = te.shape[0]
    lane = lax.broadcasted_iota(I32, (tm, LANES), 1)
    dest = jnp.zeros((tm, LANES), F32)
    for k in range(TOP_K):
        sk = jnp.sum(jnp.where(lane == te[:, k:k + 1], starts_row, 0.0), axis=-1, keepdims=True)
        dest = jnp.where(lane == k, sk + pos[:, k:k + 1].astype(F32), dest)
    hi = jnp.floor(dest * (1.0 / 256.0))
    lo = dest - 256.0 * hi
    sel = (lax.broadcasted_iota(I32, (LANES, LANES), 0)
           == lax.broadcasted_iota(I32, (LANES, LANES), 1) % TOP_K).astype(BF16)
    spread = (256.0 * jnp.dot(hi.astype(BF16), sel, preferred_element_type=F32)
              + jnp.dot(lo.astype(BF16), sel, preferred_element_type=F32))
    row = lax.broadcasted_iota(I32, (tm, LANES), 0)
    keep = lane // TOP_K == row % DEST_GROUP
    dense = jnp.sum(jnp.where(keep, spread, 0.0).reshape(tm // DEST_GROUP, DEST_GROUP, LANES), axis=1)
    dest_ref[...] = dense.astype(I32)

    @pl.when(pl.program_id(0) == 0)
    def _():
        nl = meta_ref.shape[1]
        r_i = lax.broadcasted_iota(I32, (LANES, LANES), 0)
        l_i = lax.broadcasted_iota(I32, (LANES, LANES), 1)
        ends_col = jnp.sum(jnp.where(l_i == r_i, ends_row, 0.0), axis=-1, keepdims=True)
        e_i = lax.broadcasted_iota(I32, (LANES, nl), 0)
        b_i = lax.broadcasted_iota(I32, (LANES, nl), 1)
        closed = (e_i < N_EXPERTS) & (ends_col <= (b_i * MOE_TM).astype(F32))
        be = jnp.minimum(jnp.sum(jnp.where(closed, 1.0, 0.0), axis=0, keepdims=True), N_EXPERTS - 1.0)
        total = jnp.sum(jnp.where(lane8[0:1, :] == N_EXPERTS - 1, ends_row, 0.0), axis=-1, keepdims=True)
        n_used = (total.astype(I32) >> shift).astype(F32)
        meta = jnp.where(b_i[0:1, :] < n_blocks, be, n_used).astype(I32)
        meta_ref[...] = jnp.broadcast_to(meta, meta_ref.shape)


def _moe_dest(te128, pos128, cnt, n_blocks):
    n = te128.shape[0]
    tm = ROW_TILE
    nl = -(-(n_blocks + 1) // LANES) * LANES
    row = lambda i: (i, 0)
    const = lambda i: (0, 0)
    return pl.pallas_call(
        functools.partial(_moe_dest_kernel, n_blocks=n_blocks),
        grid=(n // tm,),
        in_specs=[pl.BlockSpec((tm, LANES), row),
                  pl.BlockSpec((tm, LANES), row),
                  pl.BlockSpec((SUBLANES, LANES), const)],
        out_specs=[pl.BlockSpec((tm // DEST_GROUP, LANES), row),
                   pl.BlockSpec((SUBLANES, nl), const)],
        out_shape=[jax.ShapeDtypeStruct((n // DEST_GROUP, LANES), I32),
                   jax.ShapeDtypeStruct((SUBLANES, nl), I32)],
        compiler_params=_cparams(("arbitrary",)),
        name="moe_dest",
    )(te128, pos128, cnt)


def _row_token_kernel(dest_ref, rt_ref, *, n_assign):
    def clear(r, c):
        rt_ref[r] = 0
        return c

    def place(t, c):
        for k in range(TOP_K):
            rt_ref[dest_ref[t * TOP_K + k]] = t
        return c

    lax.fori_loop(0, rt_ref.shape[0], clear, 0, unroll=8)
    lax.fori_loop(0, n_assign // TOP_K, place, 0, unroll=4)


def _row_token(dest, rows_total):
    return pl.pallas_call(
        functools.partial(_row_token_kernel, n_assign=dest.shape[0]),
        grid_spec=pltpu.PrefetchScalarGridSpec(
            num_scalar_prefetch=1,
            grid=(1,),
            in_specs=[],
            out_specs=pl.BlockSpec(memory_space=pltpu.SMEM)),
        out_shape=jax.ShapeDtypeStruct((rows_total,), I32),
        compiler_params=_cparams(("arbitrary",)),
        name="moe_row_token",
    )(dest)


def _dispatch_kernel(rt_ref, nb_ref, h_ref, o_ref, buf, sem, *, tg):
    i = pl.program_id(0)
    n_used = (nb_ref[0] * MOE_TM + tg - 1) // tg
    slot = i % 2
    unroll = 8

    def issue(blk, s):
        def body(g, c):
            for u in range(unroll):
                r = g * unroll + u
                tok = rt_ref[blk * tg + r]
                pltpu.make_async_copy(h_ref.at[pl.ds(tok, 1), :], buf.at[s, pl.ds(r, 1), :],
                                      sem.at[s]).start(priority=u % 2)
            return c
        lax.fori_loop(0, tg // unroll, body, 0)

    @pl.when(i == 0)
    def _():
        issue(0, 0)

    @pl.when(i + 1 < n_used)
    def _():
        issue(i + 1, 1 - slot)

    @pl.when(i < n_used)
    def _():
        pltpu.make_async_copy(h_ref.at[pl.ds(0, tg), :], buf.at[slot], sem.at[slot]).wait()
        o_ref[...] = buf[slot].astype(BF16)

    @pl.when(i >= n_used)
    def _():
        o_ref[...] = jnp.zeros(o_ref.shape, o_ref.dtype)


def _dispatch(row_tok, n_blocks_used, h, rows_total):
    tg = 2 * MOE_TM
    d = h.shape[1]
    return pl.pallas_call(
        functools.partial(_dispatch_kernel, tg=tg),
        grid_spec=pltpu.PrefetchScalarGridSpec(
            num_scalar_prefetch=2,
            grid=(rows_total // tg,),
            in_specs=[pl.BlockSpec(memory_space=pl.ANY)],
            out_specs=pl.BlockSpec((tg, d), lambda i, rt, nb: (i, 0)),
            scratch_shapes=[pltpu.VMEM((2, tg, d), F32), pltpu.SemaphoreType.DMA((2,))]),
        out_shape=jax.ShapeDtypeStruct((rows_total, d), BF16),
        compiler_params=_cparams(("arbitrary",)),
        name="moe_dispatch",
    )(row_tok, n_blocks_used, h)


def _new_expert(be_ref, b):
    return (b == 0) | (be_ref[b] != be_ref[jnp.maximum(b - 1, 0)])


def _swiglu_pairs(v):
    g = jnp.minimum(v, SWIGLU_LIMIT)
    glu = g * jax.nn.sigmoid(g * SWIGLU_ALPHA)
    up1 = jnp.clip(v, -SWIGLU_LIMIT, SWIGLU_LIMIT) + 1.0
    return glu, up1


WEIGHT_DMA_PRIORITY = 1


MOE_SUB_GU = 2
MOE_SUB_DOWN = 2


def _stream_expert_weights(be_ref, nb_ref, run_ctr, copies, consume, b):
    j = pl.program_id(0)
    nj = pl.num_programs(0)
    nb = nb_ref[0]
    last_blk = be_ref.shape[0] - 1
    e = be_ref[b]

    @pl.when((j == 0) & (b == 0))
    def _():
        run_ctr[0] = 0
        for c in copies(0, e, 0):
            c.start(priority=WEIGHT_DMA_PRIORITY)

    @pl.when((b < nb) & _new_expert(be_ref, b))
    def _():
        k = run_ctr[0]
        slot = k % 2
        for c in copies(j, e, slot):
            c.wait()
        consume(slot)
        run_end = lax.while_loop(lambda bb: (bb < nb) & (be_ref[jnp.minimum(bb, last_blk)] == e),
                                 lambda bb: bb + 1, b + 1)
        more_runs = run_end < nb
        j_next = jnp.where(more_runs, j, j + 1)
        e_next = jnp.where(more_runs, be_ref[jnp.minimum(run_end, last_blk)], be_ref[0])

        @pl.when(more_runs | (j + 1 < nj))
        def _():
            for c in copies(j_next, e_next, 1 - slot):
                c.start(priority=WEIGHT_DMA_PRIORITY)
        run_ctr[0] = k + 1


def _moe_gu_kernel(be_ref, nb_ref, xb_ref, w_ref, bias_ref, o_ref, wa_s, wb_s, wbuf, sem, run_ctr,
                   *, layer, tn, nj):
    j = pl.program_id(0)
    nb = nb_ref[0]
    tm = MOE_TM

    def copies(jj, e, s):
        return [pltpu.make_async_copy(
            w_ref.at[layer, e, :, pl.ds(pl.multiple_of((jj + h * nj) * tn, tn), tn)], wbuf.at[s, h], sem.at[s])
            for h in range(2)]

    def consume(s):
        wa_s[...] = wbuf[s, 0].astype(BF16)
        wb_s[...] = wbuf[s, 1].astype(BF16)

    def block(u):
        b = pl.program_id(1) * MOE_SUB_GU + u
        rows = slice(u * tm, (u + 1) * tm)
        _stream_expert_weights(be_ref, nb_ref, run_ctr, copies, consume, b)

        @pl.when(b < nb)
        def _():
            x = xb_ref[rows, :]
            brow = be_ref[b] * (2 * nj) + j
            ga = jnp.dot(x, wa_s[...], preferred_element_type=F32) + bias_ref[pl.ds(brow, 1), :]
            gb = jnp.dot(x, wb_s[...], preferred_element_type=F32) + bias_ref[pl.ds(brow + nj, 1), :]
            even = lax.broadcasted_iota(I32, (tm, LANES), 1) % 2 == 0
            for c in range(tn // LANES):
                glu_a, up_a = _swiglu_pairs(ga[:, c * LANES:(c + 1) * LANES])
                glu_b, up_b = _swiglu_pairs(gb[:, c * LANES:(c + 1) * LANES])
                ra = glu_a * pltpu.roll(up_a, LANES - 1, 1)
                rb = pltpu.roll(glu_b, 1, 1) * up_b
                o_ref[rows, c * LANES:(c + 1) * LANES] = jnp.where(even, ra, rb).astype(BF16)

        @pl.when(b >= nb)
        def _():
            o_ref[rows, :] = jnp.zeros((tm, tn), o_ref.dtype)

    for u in range(MOE_SUB_GU):
        block(u)


def _moe_gu(block_e, n_blocks_used, xb, wgu_all, layer, bgu):
    rows, d = xb.shape
    ne, f2 = bgu.shape
    tm = MOE_TM
    tn = 1024
    nj = f2 // 2 // tn
    bias = bgu.reshape(ne * 2 * nj, tn)
    return pl.pallas_call(
        functools.partial(_moe_gu_kernel, layer=layer, tn=tn, nj=nj),
        grid_spec=pltpu.PrefetchScalarGridSpec(
            num_scalar_prefetch=2,
            grid=(nj, rows // (tm * MOE_SUB_GU)),
            in_specs=[pl.BlockSpec((tm * MOE_SUB_GU, d), lambda j, b, be, nb: (b, 0)),
                      pl.BlockSpec(memory_space=pl.ANY),
                      pl.BlockSpec(bias.shape, lambda j, b, be, nb: (0, 0))],
            out_specs=pl.BlockSpec((tm * MOE_SUB_GU, tn), lambda j, b, be, nb: (b, j)),
            scratch_shapes=[pltpu.VMEM((d, tn), BF16), pltpu.VMEM((d, tn), BF16),
                            pltpu.VMEM((2, 2, d, tn), F32), pltpu.SemaphoreType.DMA((2,)),
                            pltpu.SMEM((1,), I32)]),
        out_shape=jax.ShapeDtypeStruct((rows, f2 // 2), BF16),
        compiler_params=_cparams(("arbitrary", "arbitrary")),
        name="moe_gate_up",
    )(block_e, n_blocks_used, xb, wgu_all, bias)


def _moe_down_kernel(be_ref, nb_ref, a_ref, w_ref, bd_ref, o_ref, wp_s, stage, wbuf, sem, run_ctr, *, layer, tn):
    nb = nb_ref[0]
    tm = MOE_TM

    def copies(j, e, slot):
        return [pltpu.make_async_copy(w_ref.at[layer, e, :, pl.ds(pl.multiple_of(j * tn, tn), tn)], wbuf.at[slot],
                                      sem.at[slot])]

    def consume(slot):
        f = wbuf.shape[1]
        ns = stage.shape[0]
        for c in range(tn // LANES):
            cols = slice(c * LANES, (c + 1) * LANES)
            stage[c % ns, pl.ds(0, f // 2, stride=2), :] = wbuf[slot, :f // 2, cols]
            stage[c % ns, pl.ds(1, f // 2, stride=2), :] = wbuf[slot, f // 2:, cols]
            wp_s[:, cols] = stage[c % ns].astype(BF16)

    def block(u):
        b = pl.program_id(1) * MOE_SUB_DOWN + u
        rows = slice(u * tm, (u + 1) * tm)
        _stream_expert_weights(be_ref, nb_ref, run_ctr, copies, consume, b)

        @pl.when(b < nb)
        def _():
            brow = be_ref[b] * pl.num_programs(0) + pl.program_id(0)
            o_ref[rows, :] = (jnp.dot(a_ref[rows, :], wp_s[...], preferred_element_type=F32)
                              + bd_ref[pl.ds(brow, 1), :])

        @pl.when(b >= nb)
        def _():
            o_ref[rows, :] = jnp.zeros((tm, tn), o_ref.dtype)

    for u in range(MOE_SUB_DOWN):
        block(u)


def _moe_down(block_e, n_blocks_used, act, wd_all, layer, bd):
    rows, f = act.shape
    ne, d = bd.shape
    tm = MOE_TM
    tn = d
    bias = bd.reshape(ne * (d // tn), tn)
    return pl.pallas_call(
        functools.partial(_moe_down_kernel, layer=layer, tn=tn),
        grid_spec=pltpu.PrefetchScalarGridSpec(
            num_scalar_prefetch=2,
            grid=(d // tn, rows // (tm * MOE_SUB_DOWN)),
            in_specs=[pl.BlockSpec((tm * MOE_SUB_DOWN, f), lambda j, b, be, nb: (b, 0)),
                      pl.BlockSpec(memory_space=pl.ANY),
                      pl.BlockSpec(bias.shape, lambda j, b, be, nb: (0, 0))],
            out_specs=pl.BlockSpec((tm * MOE_SUB_DOWN, tn), lambda j, b, be, nb: (b, j)),
            scratch_shapes=[pltpu.VMEM((f, tn), BF16), pltpu.VMEM((2, f, LANES), F32),
                            pltpu.VMEM((2, f, tn), F32), pltpu.SemaphoreType.DMA((2,)), pltpu.SMEM((1,), I32)]),
        out_shape=jax.ShapeDtypeStruct((rows, d), F32),
        compiler_params=_cparams(("arbitrary", "arbitrary"), VMEM_LIMIT_DOWN),
        name="moe_down",
    )(block_e, n_blocks_used, act, wd_all, bias)


def _combine_kernel(dest_ref, y_ref, x1_ref, gm_ref, tg_ref, fg_ref, o_ref, buf, sem, *, tn, tile0, final_norm):
    i = pl.program_id(0) + tile0

    def issue(t, c):
        for k in range(TOP_K):
            d = dest_ref[(i * tn + t) * TOP_K + k]
            pltpu.make_async_copy(y_ref.at[pl.ds(d, 1), :], buf.at[k, pl.ds(t, 1), :], sem).start(priority=k % 2)
        return c

    lax.fori_loop(0, tn, issue, 0, unroll=8)
    for k in range(TOP_K):
        pltpu.make_async_copy(y_ref.at[pl.ds(0, tn), :], buf.at[k], sem).wait()
    tg = tg_ref[...]
    moe = tg[:, 0:1] * buf[0]
    for k in range(1, TOP_K):
        moe = moe + tg[:, k:k + 1] * buf[k]
    x2 = _gated_residual(x1_ref[...], gm_ref[...], moe)
    if final_norm:
        x2 = _rms(x2, fg_ref[...])
    o_ref[...] = x2


def _combine(dest, y, x1, modg, tgates, fg, group, final_norm, row0, rows):
    d = x1.shape[1]
    tn = ROW_TILE
    ng = tn // group
    t0 = row0 // tn
    return pl.pallas_call(
        functools.partial(_combine_kernel, tn=tn, tile0=t0, final_norm=final_norm),
        grid_spec=pltpu.PrefetchScalarGridSpec(
            num_scalar_prefetch=1,
            grid=(rows // tn,),
            in_specs=[pl.BlockSpec(memory_space=pl.ANY),
                      pl.BlockSpec((tn, d), lambda i, ds: (i + t0, 0)),
                      pl.BlockSpec((ng, d), lambda i, ds: (i + t0, 5)),
                      pl.BlockSpec((tn, LANES), lambda i, ds: (i + t0, 0)),
                      pl.BlockSpec((1, d), lambda i, ds: (0, 0))],
            out_specs=pl.BlockSpec((tn, d), lambda i, ds: (i, 0)),
            scratch_shapes=[pltpu.VMEM((TOP_K, tn, d), F32), pltpu.SemaphoreType.DMA(())]),
        out_shape=jax.ShapeDtypeStruct((rows, d), F32),
        compiler_params=_cparams(("arbitrary",)),
        name="moe_combine",
    )(dest, y, x1, modg, tgates, fg.reshape(1, d))


def _moe(hp, te128, tg128, pos128, cnt, x1, modg, wgu_all, bgu, wd_all, bd, layer, fg, group, splits):
    n = x1.shape[0]
    rows_total = n * TOP_K + N_EXPERTS * MOE_TM
    n_blocks = rows_total // MOE_TM
    dest2d, meta = _moe_dest(te128, pos128, cnt, n_blocks)
    dest = dest2d.reshape(-1)
    block_e = meta[0, :n_blocks]
    nbu = meta[0, n_blocks:n_blocks + 1]
    row_tok = _row_token(dest, rows_total)
    xb = _dispatch(row_tok, nbu, hp, rows_total)
    act = _moe_gu(block_e, nbu, xb, wgu_all, layer, bgu)
    y = _moe_down(block_e, nbu, act, wd_all, layer, bd)
    return [_combine(dest, y, x1, modg, tg128, fg, group, fn, r0, rows) for r0, rows, fn in splits]


def _lru_in_kernel(x_ref, g1_ref, sh_ref, sc_ref, wy_ref, wx_ref, y_ref, xb_ref):
    h = _norm_mod(x_ref[...], g1_ref[...], sh_ref[...], sc_ref[...]).astype(BF16)
    y = jnp.dot(h, wy_ref[...], preferred_element_type=F32)
    y_ref[...] = jax.nn.gelu(y, approximate=True).astype(BF16)
    xb_ref[...] = jnp.dot(h, wx_ref[...], preferred_element_type=F32)


def _lru_in(x, g1, modg, wy, wx, group):
    n, d = x.shape
    dr = wy.shape[1]
    tm = ROW_TILE
    ng = tm // group
    row = lambda i: (i, 0)
    const = lambda i: (0, 0)
    return pl.pallas_call(
        _lru_in_kernel,
        grid=(n // tm,),
        in_specs=[pl.BlockSpec((tm, d), row),
                  pl.BlockSpec((1, d), const),
                  pl.BlockSpec((ng, d), lambda i: (i, 0)),
                  pl.BlockSpec((ng, d), lambda i: (i, 1)),
                  pl.BlockSpec((d, dr), const),
                  pl.BlockSpec((d, dr), const)],
        out_specs=[pl.BlockSpec((tm, dr), row), pl.BlockSpec((tm, dr), row)],
        out_shape=[jax.ShapeDtypeStruct((n, dr), BF16), jax.ShapeDtypeStruct((n, dr), F32)],
        compiler_params=_cparams(("arbitrary",)),
        name="lru_in",
    )(x, g1.reshape(1, d), modg, modg, wy, wx)


def _lru_scan_kernel(y_ref, xb_ref, cb_ref, h0_ref, cw_ref, cbias_ref, wa_ref, ba_ref, wx_ref, bx_ref, lam_ref,
                     hy_ref, cbo_ref, ho_ref, xe, a_s, u_s, hc, *, tc, starts_at_pos0):
    c = pl.program_id(1)
    dr = xb_ref.shape[1]
    nb = wa_ref.shape[0]
    bd = dr // nb
    pre = SUBLANES

    @pl.when(c == 0)
    def _():
        xe[0:pre, :] = jnp.zeros((pre, dr), F32)
        xe[pre - (CONV_W - 1):pre, :] = cb_ref[0]
        hc[...] = jnp.broadcast_to(h0_ref[0], (SUBLANES, dr))

    xe[pre:pre + tc, :] = xb_ref[...]
    cw = cw_ref[...]
    xc = cbias_ref[...] + xe[pre:pre + tc, :] * cw[CONV_W - 1:CONV_W, :]
    for k in range(1, CONV_W):
        xc = xc + xe[pre - k:pre - k + tc, :] * cw[CONV_W - 1 - k:CONV_W - k, :]
    cbo_ref[0] = xe[pre + tc - (CONV_W - 1):pre + tc, :]
    xe[0:pre, :] = xe[tc:tc + pre, :]

    xcb = xc.astype(BF16)
    ra = jnp.concatenate([jnp.dot(xcb[:, n * bd:(n + 1) * bd], wa_ref[n], preferred_element_type=F32)
                          for n in range(nb)], axis=1)
    rx = jnp.concatenate([jnp.dot(xcb[:, n * bd:(n + 1) * bd], wx_ref[n], preferred_element_type=F32)
                          for n in range(nb)], axis=1)
    r = jax.nn.sigmoid(ra + ba_ref[...])
    ig = jax.nn.sigmoid(rx + bx_ref[...])
    lam = lam_ref[...]
    log_sig = jnp.minimum(lam, 0.0) - jnp.log1p(jnp.exp(-jnp.abs(lam)))
    log_a = LRU_C * r * log_sig
    a = jnp.exp(log_a)
    th = jnp.tanh(log_a)
    mult = jnp.sqrt(-2.0 * th / (1.0 - th))
    if starts_at_pos0:
        first = (lax.broadcasted_iota(I32, (tc, 1), 0) == 0) & (c == 0)
        mult = jnp.where(first, 1.0, mult)
    a_s[...] = a
    u_s[...] = mult * ig * xc

    row8 = lax.broadcasted_iota(I32, (SUBLANES, dr), 0)

    def group_step(g, hprev):
        off = pl.multiple_of(g * SUBLANES, SUBLANES)
        aa = a_s[pl.ds(off, SUBLANES), :]
        uu = u_s[pl.ds(off, SUBLANES), :]
        s = 1
        while s < SUBLANES:
            m = row8 >= s
            uu = jnp.where(m, aa * pltpu.roll(uu, s, 0) + uu, uu)
            aa = jnp.where(m, aa * pltpu.roll(aa, s, 0), aa)
            s *= 2
        hh = aa * hprev + uu
        u_s[pl.ds(off, SUBLANES), :] = hh
        return jnp.broadcast_to(hh[SUBLANES - 1:SUBLANES, :], (SUBLANES, dr))

    hlast = lax.fori_loop(0, tc // SUBLANES, group_step, hc[...])
    hc[...] = hlast
    ho_ref[0] = hlast[0:1, :]
    hy_ref[...] = (u_s[...] * y_ref[...].astype(F32)).astype(BF16)


def _lru_scan(yb, xb, conv_buf, h0, cw, cbias, wa, ba, wx, bx, lam, row0, n_seq, t, tc, starts_at_pos0):
    dr = xb.shape[1]
    nc = t // tc
    rb0 = row0 // tc
    inmap = lambda s, c: (rb0 + s * nc + c, 0)
    outmap = lambda s, c: (s * nc + c, 0)
    const2 = lambda s, c: (0, 0)
    const3 = lambda s, c: (0, 0, 0)
    seq3 = lambda s, c: (s, 0, 0)
    return pl.pallas_call(
        functools.partial(_lru_scan_kernel, tc=tc, starts_at_pos0=starts_at_pos0),
        grid=(n_seq, nc),
        in_specs=[pl.BlockSpec((tc, dr), inmap),
                  pl.BlockSpec((tc, dr), inmap),
                  pl.BlockSpec((1, CONV_W - 1, dr), seq3),
                  pl.BlockSpec((1, 1, dr), seq3),
                  pl.BlockSpec((CONV_W, dr), const2),
                  pl.BlockSpec((1, dr), const2),
                  pl.BlockSpec(wa.shape, const3),
                  pl.BlockSpec((1, dr), const2),
                  pl.BlockSpec(wx.shape, const3),
                  pl.BlockSpec((1, dr), const2),
                  pl.BlockSpec((1, dr), const2)],
        out_specs=[pl.BlockSpec((tc, dr), outmap),
                   pl.BlockSpec((1, CONV_W - 1, dr), seq3),
                   pl.BlockSpec((1, 1, dr), seq3)],
        out_shape=[jax.ShapeDtypeStruct((n_seq * t, dr), BF16),
                   jax.ShapeDtypeStruct((n_seq, CONV_W - 1, dr), F32),
                   jax.ShapeDtypeStruct((n_seq, 1, dr), F32)],
        scratch_shapes=[pltpu.VMEM((SUBLANES + tc, dr), F32),
                        pltpu.VMEM((tc, dr), F32),
                        pltpu.VMEM((tc, dr), F32),
                        pltpu.VMEM((SUBLANES, dr), F32)],
        compiler_params=_cparams(("arbitrary", "arbitrary")),
        name="lru_scan",
    )(yb, xb, conv_buf, h0.reshape(n_seq, 1, dr), cw, cbias.reshape(1, dr), wa, ba.reshape(1, dr),
      wx, bx.reshape(1, dr), lam.reshape(1, dr))


def _rope_tables(pos):
    half = ROPE_DIM // 2
    inv = 1.0 / (ROPE_THETA ** (jnp.arange(0, ROPE_DIM, 2, dtype=F32) / ROPE_DIM))
    ang = pos.astype(F32)[:, None] * inv[None, :]
    cos, sin = jnp.cos(ang), jnp.sin(ang)
    z = jnp.zeros((pos.shape[0], LANES - ROPE_DIM), F32)
    return jnp.concatenate([cos, cos, z], axis=1), jnp.concatenate([-sin, sin, z], axis=1)


def _head_slab_weights(w_uq):
    ql = w_uq.shape[0]
    w = w_uq.reshape(ql, N_HEADS, NOPE_DIM + ROPE_DIM)
    z = jnp.zeros((ql, N_HEADS, HEAD_W - NOPE_DIM - ROPE_DIM), w.dtype)
    return jnp.concatenate([w, z], axis=2).reshape(ql, N_HEADS * HEAD_W).astype(BF16)


def kernel(x_prompt, x_sample, cache_ckv, cache_krope, state_conv, state_h, c_prompt, c_sample,
           mod_w, mod_b, norm1_g, norm2_g,
           mla_w_in, mla_q_norm_g, mla_kv_norm_g, mla_w_uq, mla_w_ukv, mla_w_o,
           lru_w_in, lru_conv_w, lru_conv_b, lru_w_a, lru_b_a, lru_w_x, lru_b_x, lru_lambda, lru_w_o,
           router_w, router_b, moe_w_gu, moe_b_gu, moe_w_down, moe_b_down, final_g):
    bp, tp, d = x_prompt.shape
    bs, ts, _ = x_sample.shape
    past = cache_ckv.shape[2]
    depth = mod_w.shape[0]
    n_p, n_s = bp * tp, bs * ts
    n = n_p + n_s
    group = math.gcd(tp, ts)
    assert group % SUBLANES == 0 and ROW_TILE % group == 0 and n_p % ROW_TILE == 0 and n_s % ROW_TILE == 0
    assert NOPE_DIM == LANES and V_DIM == LANES and ROPE_DIM <= LANES and n_p % n_s == 0

    x = jnp.concatenate([x_prompt.reshape(n_p, d), x_sample.reshape(n_s, d)], axis=0)
    nb = bp + bs
    nb_pad = -(-nb // SUBLANES) * SUBLANES
    c_all = jnp.concatenate([c_prompt, c_sample, jnp.zeros((nb_pad - nb, d), F32)], axis=0)
    grp_batch = np.concatenate([np.repeat(np.arange(bp), tp // group), bp + np.repeat(np.arange(bs), ts // group)])
    pos = jnp.concatenate([jnp.tile(jnp.arange(tp), bp), jnp.tile(past + jnp.arange(ts), bs)])
    cos, sin = _rope_tables(pos)

    outs = {k: [] for k in ("ckv_p", "kr_p", "conv_p", "h_p", "ckv_s", "kr_s", "conv_s", "h_s")}
    for i in range(depth):
        mod = _adaln(c_all, mod_w, i, mod_b[i])
        modg = jnp.take(mod, jnp.asarray(grp_batch), axis=0)
        j = i // 2
        if i % 2 == 0:
            w_in = mla_w_in[j]
            zpad = jnp.zeros((d, LANES - ROPE_DIM), F32)
            win = jnp.concatenate([w_in, zpad], axis=1).astype(BF16)
            wuq = _head_slab_weights(mla_w_uq[j])
            wukv = mla_w_ukv[j].reshape(KV_LORA, N_HEADS, NOPE_DIM + V_DIM)
            wk = wukv[:, :, :NOPE_DIM].reshape(KV_LORA, N_HEADS * NOPE_DIM).astype(BF16)
            wv = wukv[:, :, NOPE_DIM:].reshape(KV_LORA, N_HEADS * V_DIM).astype(BF16)
            q, ckv, kr, ckvb, krp = _mla_proj(x, norm1_g[i], modg, win, mla_q_norm_g[j], mla_kv_norm_g[j],
                                              wuq, cos, sin, group)
            kk, vv = _kv_expand(ckvb, krp, wk, wv, n_p)
            o_p = _attn_prompt(q, kk, vv, bp, tp)
            qa = _absorb(q, wk, n_p, n_s)
            ckr_pad = jnp.pad(cache_krope[j], ((0, 0), (0, 0), (0, LANES - ROPE_DIM)))
            o_lat = _attn_sample(qa, q, cache_ckv[j], ckr_pad, ckvb, krp, n_p, bs, ts, past)
            o_s = _unabsorb(o_lat, wv, n_s)
            wo = mla_w_o[j].astype(BF16)
            outs["ckv_p"].append(ckv[:n_p].reshape(bp, tp, KV_LORA))
            outs["kr_p"].append(kr[:n_p].reshape(bp, tp, ROPE_DIM))
            outs["ckv_s"].append(ckv[n_p:].reshape(bs, ts, KV_LORA))
            outs["kr_s"].append(kr[n_p:].reshape(bs, ts, ROPE_DIM))
        else:
            dr = lru_w_in.shape[2] // 2
            wy = lru_w_in[j][:, :dr].astype(BF16)
            wx = lru_w_in[j][:, dr:].astype(BF16)
            yb, xb = _lru_in(x, norm1_g[i], modg, wy, wx, group)
            wa = lru_w_a[j].astype(BF16)
            wxg = lru_w_x[j].astype(BF16)
            lru_args = (lru_conv_w[j], lru_conv_b[j], wa, lru_b_a[j], wxg, lru_b_x[j], lru_lambda[j])
            zbuf = jnp.zeros((bp, CONV_W - 1, dr), F32)
            zh = jnp.zeros((bp, dr), F32)
            o_p, cb_p, h_p = _lru_scan(yb, xb, zbuf, zh, *lru_args, 0, bp, tp, ROW_TILE, True)
            o_s, cb_s, h_s = _lru_scan(yb, xb, state_conv[j], state_h[j], *lru_args, n_p, bs, ts, ts, False)
            wo = lru_w_o[j].astype(BF16)
            outs["conv_p"].append(cb_p)
            outs["h_p"].append(h_p.reshape(bp, dr))
            outs["conv_s"].append(cb_s)
            outs["h_s"].append(h_s.reshape(bs, dr))
        x1, hp, te128, tg128, pos128, cnt = _post_mixer(o_p, o_s, wo, x, norm2_g[i], modg, router_w[i], router_b[i],
                                                        group)
        last = i == depth - 1
        splits = [(0, n_p, True), (n_p, n_s, True)] if last else [(0, n, False)]
        res = _moe(hp, te128, tg128, pos128, cnt, x1, modg, moe_w_gu, moe_b_gu[i], moe_w_down, moe_b_down[i], i,
                   final_g, group, splits)
        x = res[0]
    y_prompt = res[0].reshape(bp, tp, d)
    y_sample = res[1].reshape(bs, ts, d)
    return (y_prompt, y_sample,
            jnp.stack(outs["ckv_p"]), jnp.stack(outs["kr_p"]), jnp.stack(outs["conv_p"]), jnp.stack(outs["h_p"]),
            jnp.stack(outs["ckv_s"]), jnp.stack(outs["kr_s"]), jnp.stack(outs["conv_s"]), jnp.stack(outs["h_s"]))
```

```python
import functools
import math

import jax
import jax.numpy as jnp
import numpy as np
from jax import lax
from jax.experimental import pallas as pl
from jax.experimental.pallas import tpu as pltpu

F32 = jnp.float32
BF16 = jnp.bfloat16
I32 = jnp.int32

CHUNK = 64
N_HEADS = 16
Q_LORA = 512
KV_LORA = 512
NOPE_DIM = 128
ROPE_DIM = 64
V_DIM = 128
ROPE_THETA = 10000.0
LRU_BLOCKS = 8
CONV_W = 4
LRU_C = 8.0
N_EXPERTS = 32
TOP_K = 4
SWIGLU_LIMIT = 7.0
SWIGLU_ALPHA = 1.702
N_MOD = 6
EPS = 1e-6

LANES = 128
SUBLANES = 8
HEAD_W = 2 * LANES

ROW_TILE = 256
MOE_TM = 256
VMEM_LIMIT = 56 * 1024 * 1024
VMEM_LIMIT_DOWN = 60 * 1024 * 1024


def _cparams(sem, vmem_limit=VMEM_LIMIT):
    return pltpu.CompilerParams(dimension_semantics=sem, vmem_limit_bytes=vmem_limit)


def _rms(x, g):
    ms = jnp.mean(x * x, axis=-1, keepdims=True)
    return x * lax.rsqrt(ms + EPS) * g


def _norm_mod(x, g, shift, scale):
    tm, d = x.shape
    ng = shift.shape[0]
    y = _rms(x, g).reshape(ng, tm // ng, d)
    return (y * (1.0 + scale[:, None, :]) + shift[:, None, :]).reshape(tm, d)


def _gated_residual(x, gate, y):
    tm, d = x.shape
    ng = gate.shape[0]
    return (x.reshape(ng, tm // ng, d) + gate[:, None, :] * y.reshape(ng, tm // ng, d)).reshape(tm, d)


def _adaln_kernel(c_ref, w_ref, b_ref, o_ref):
    c = c_ref[...]
    a = (c * jax.nn.sigmoid(c)).astype(BF16)
    o_ref[...] = jnp.dot(a, w_ref[0].astype(BF16), preferred_element_type=F32) + b_ref[...]


def _adaln(c_all, w_all, layer, b):
    bp, d = c_all.shape
    n = w_all.shape[2]
    tn = 1024
    return pl.pallas_call(
        _adaln_kernel,
        grid=(n // tn,),
        in_specs=[pl.BlockSpec((bp, d), lambda j: (0, 0)),
                  pl.BlockSpec((1, d, tn), lambda j: (layer, 0, j)),
                  pl.BlockSpec((1, tn), lambda j: (0, j))],
        out_specs=pl.BlockSpec((bp, tn), lambda j: (0, j)),
        out_shape=jax.ShapeDtypeStruct((bp, n), F32),
        compiler_params=_cparams(("arbitrary",)),
        name="adaln",
    )(c_all, w_all, b.reshape(1, n))


def _rope128(v, cos, sin):
    half = ROPE_DIM // 2
    lane = lax.broadcasted_iota(I32, v.shape, 1)
    sw = jnp.where(lane < half, pltpu.roll(v, LANES - half, 1), pltpu.roll(v, half, 1))
    return v * cos + sw * sin


def _mla_proj_kernel(x_ref, g1_ref, sh_ref, sc_ref, win_ref, qg_ref, kvg_ref, wuq_ref, cos_ref, sin_ref,
                     q_ref, ckv_ref, kr_ref, ckvb_ref, krp_ref):
    h = _norm_mod(x_ref[...], g1_ref[...], sh_ref[...], sc_ref[...]).astype(BF16)
    lat = jnp.dot(h, win_ref[...], preferred_element_type=F32)
    q_lat = lat[:, :Q_LORA]
    c_kv = lat[:, Q_LORA:Q_LORA + KV_LORA]
    k_r = lat[:, Q_LORA + KV_LORA:]
    qn = _rms(q_lat, qg_ref[...]).astype(BF16)
    q = jnp.dot(qn, wuq_ref[...], preferred_element_type=F32)
    cos = cos_ref[...]
    sin = sin_ref[...]
    scale = (NOPE_DIM + ROPE_DIM) ** -0.5
    for hh in range(N_HEADS):
        lo = hh * HEAD_W
        q_ref[:, lo:lo + LANES] = (q[:, lo:lo + LANES] * scale).astype(BF16)
        q_ref[:, lo + LANES:lo + HEAD_W] = (_rope128(q[:, lo + LANES:lo + HEAD_W], cos, sin) * scale).astype(BF16)
    ckv = _rms(c_kv, kvg_ref[...])
    ckv_ref[...] = ckv
    ckvb_ref[...] = ckv.astype(BF16)
    kr = _rope128(k_r, cos, sin)
    kr_ref[...] = kr[:, :ROPE_DIM]
    krp_ref[...] = kr.astype(BF16)


def _mla_proj(x, g1, modg, win, qg, kvg, wuq, cos, sin, group, prompt_tiles, tiles_per_seq):
    n, d = x.shape
    tm = ROW_TILE
    ng = tm // group
    wl = win.shape[1]
    qw = wuq.shape[1]
    row = lambda i: (i, 0)
    const = lambda i: (0, 0)
    rope = lambda i: (jnp.where(i < prompt_tiles, i % tiles_per_seq, tiles_per_seq), 0)
    return pl.pallas_call(
        _mla_proj_kernel,
        grid=(n // tm,),
        in_specs=[pl.BlockSpec((tm, d), row),
                  pl.BlockSpec((1, d), const),
                  pl.BlockSpec((ng, d), lambda i: (i, 0)),
                  pl.BlockSpec((ng, d), lambda i: (i, 1)),
                  pl.BlockSpec((d, wl), const),
                  pl.BlockSpec((1, Q_LORA), const),
                  pl.BlockSpec((1, KV_LORA), const),
                  pl.BlockSpec((Q_LORA, qw), const),
                  pl.BlockSpec((tm, LANES), rope),
                  pl.BlockSpec((tm, LANES), rope)],
        out_specs=[pl.BlockSpec((tm, qw), row),
                   pl.BlockSpec((tm, KV_LORA), row),
                   pl.BlockSpec((tm, ROPE_DIM), row),
                   pl.BlockSpec((tm, KV_LORA), row),
                   pl.BlockSpec((tm, LANES), row)],
        out_shape=[jax.ShapeDtypeStruct((n, qw), BF16),
                   jax.ShapeDtypeStruct((n, KV_LORA), F32),
                   jax.ShapeDtypeStruct((n, ROPE_DIM), F32),
                   jax.ShapeDtypeStruct((n, KV_LORA), BF16),
                   jax.ShapeDtypeStruct((n, LANES), BF16)],
        compiler_params=_cparams(("arbitrary",)),
        name="mla_proj",
    )(x, g1.reshape(1, d), modg, modg, win, qg.reshape(1, -1), kvg.reshape(1, -1), wuq, cos, sin)


def _kv_expand_kernel(c_ref, krp_ref, wk_ref, wv_ref, k_ref, v_ref):
    c = c_ref[...]
    kn = jnp.dot(c, wk_ref[...], preferred_element_type=F32).astype(BF16)
    krp = krp_ref[...]
    for hh in range(N_HEADS):
        k_ref[:, hh * HEAD_W:hh * HEAD_W + LANES] = kn[:, hh * NOPE_DIM:(hh + 1) * NOPE_DIM]
        k_ref[:, hh * HEAD_W + LANES:(hh + 1) * HEAD_W] = krp
    v_ref[...] = jnp.dot(c, wv_ref[...], preferred_element_type=F32).astype(BF16)


def _kv_expand(ckvb, krp, wk, wv, rows):
    tm = 512
    row = lambda i: (i, 0)
    const = lambda i: (0, 0)
    return pl.pallas_call(
        _kv_expand_kernel,
        grid=(rows // tm,),
        in_specs=[pl.BlockSpec((tm, KV_LORA), row),
                  pl.BlockSpec((tm, LANES), row),
                  pl.BlockSpec(wk.shape, const),
                  pl.BlockSpec(wv.shape, const)],
        out_specs=[pl.BlockSpec((tm, N_HEADS * HEAD_W), row),
                   pl.BlockSpec((tm, N_HEADS * V_DIM), row)],
        out_shape=[jax.ShapeDtypeStruct((rows, N_HEADS * HEAD_W), BF16),
                   jax.ShapeDtypeStruct((rows, N_HEADS * V_DIM), BF16)],
        compiler_params=_cparams(("arbitrary",)),
        name="kv_expand",
    )(ckvb, krp, wk, wv)


ATTN_HEADS_PER_STEP = 8


def _attn_prompt_kernel(q_ref, k_ref, v_ref, o_ref, *, tq, nq):
    qi = pl.program_id(2)
    dn = (((1,), (1,)), ((), ()))
    hs = ATTN_HEADS_PER_STEP
    r = lax.broadcasted_iota(I32, (tq, tq), 0) // CHUNK
    c = lax.broadcasted_iota(I32, (tq, tq), 1) // CHUNK
    diag_visible = c <= r
    for qs in range(nq):
        @pl.when(qi == qs)
        def _(qs=qs):
            past = qs * tq
            for h in range(hs):
                q = q_ref[:, h * HEAD_W:(h + 1) * HEAD_W]
                kcols = slice(h * HEAD_W, (h + 1) * HEAD_W)
                vcols = slice(h * V_DIM, (h + 1) * V_DIM)
                s_d = lax.dot_general(q, k_ref[past:past + tq, kcols], dn, preferred_element_type=F32)
                s_d = jnp.where(diag_visible, s_d, -jnp.inf)
                m = jnp.max(s_d, axis=-1, keepdims=True)
                if past:
                    s_f = lax.dot_general(q, k_ref[0:past, kcols], dn, preferred_element_type=F32)
                    m = jnp.maximum(m, jnp.max(s_f, axis=-1, keepdims=True))
                p_d = jnp.exp(s_d - m)
                l = jnp.sum(p_d, axis=-1, keepdims=True)
                acc = jnp.dot(p_d.astype(BF16), v_ref[past:past + tq, vcols], preferred_element_type=F32)
                if past:
                    p_f = jnp.exp(s_f - m)
                    l = l + jnp.sum(p_f, axis=-1, keepdims=True)
                    acc = acc + jnp.dot(p_f.astype(BF16), v_ref[0:past, vcols], preferred_element_type=F32)
                o_ref[:, vcols] = (acc / l).astype(BF16)


def _attn_prompt(q, k, v, bp, tp):
    n_rows = bp * tp
    tq = 256
    nq = tp // tq
    hs = ATTN_HEADS_PER_STEP
    return pl.pallas_call(
        functools.partial(_attn_prompt_kernel, tq=tq, nq=nq),
        grid=(bp, N_HEADS // hs, nq),
        in_specs=[pl.BlockSpec((tq, hs * HEAD_W), lambda b, h, i: (b * nq + i, h)),
                  pl.BlockSpec((tp, hs * HEAD_W), lambda b, h, i: (b, h)),
                  pl.BlockSpec((tp, hs * V_DIM), lambda b, h, i: (b, h))],
        out_specs=pl.BlockSpec((tq, hs * V_DIM), lambda b, h, i: (b * nq + i, h)),
        out_shape=jax.ShapeDtypeStruct((n_rows, N_HEADS * V_DIM), BF16),
        compiler_params=_cparams(("arbitrary", "arbitrary", "arbitrary")),
        name="attn_prompt",
    )(q, k, v)


def _absorb_kernel(q_ref, wk_ref, o_ref):
    dn = (((1,), (1,)), ((), ()))
    o_ref[0] = lax.dot_general(q_ref[...], wk_ref[...], dn, preferred_element_type=F32).astype(BF16)


def _absorb(q, wk, row0, rows):
    rb = row0 // rows
    return pl.pallas_call(
        _absorb_kernel,
        grid=(N_HEADS,),
        in_specs=[pl.BlockSpec((rows, LANES), lambda h: (rb, 2 * h)),
                  pl.BlockSpec((KV_LORA, NOPE_DIM), lambda h: (0, h))],
        out_specs=pl.BlockSpec((1, rows, KV_LORA), lambda h: (h, 0, 0)),
        out_shape=jax.ShapeDtypeStruct((N_HEADS, rows, KV_LORA), BF16),
        compiler_params=_cparams(("arbitrary",)),
        name="absorb",
    )(q, wk)


def _attn_sample_kernel(qa_ref, q_ref, cc_ref, ckr_ref, cn_ref, krn_ref, o_ref, *, ts, past):
    hn = N_HEADS
    qa = qa_ref[...].reshape(hn * ts, KV_LORA)
    qfull = q_ref[...]
    qr = jnp.concatenate([qfull[:, h * HEAD_W + LANES:(h + 1) * HEAD_W] for h in range(hn)], axis=0)
    cc = cc_ref[0].astype(BF16)
    ckr = ckr_ref[0].astype(BF16)
    cn = cn_ref[...]
    krn = krn_ref[...]
    dn = (((1,), (1,)), ((), ()))
    s_c = (lax.dot_general(qa, cc, dn, preferred_element_type=F32)
           + lax.dot_general(qr, ckr, dn, preferred_element_type=F32))
    s_n = (lax.dot_general(qa, cn, dn, preferred_element_type=F32)
           + lax.dot_general(qr, krn, dn, preferred_element_type=F32))
    qchunk_c = (past + lax.broadcasted_iota(I32, s_c.shape, 0) % ts) // CHUNK
    s_c = jnp.where(lax.broadcasted_iota(I32, s_c.shape, 1) // CHUNK <= qchunk_c, s_c, -jnp.inf)
    qchunk_n = (past + lax.broadcasted_iota(I32, s_n.shape, 0) % ts) // CHUNK
    s_n = jnp.where((past + lax.broadcasted_iota(I32, s_n.shape, 1)) // CHUNK <= qchunk_n, s_n, -jnp.inf)
    m = jnp.maximum(jnp.max(s_c, axis=-1, keepdims=True), jnp.max(s_n, axis=-1, keepdims=True))
    p_c = jnp.exp(s_c - m)
    p_n = jnp.exp(s_n - m)
    l = jnp.sum(p_c, axis=-1, keepdims=True) + jnp.sum(p_n, axis=-1, keepdims=True)
    o = (jnp.dot(p_c.astype(BF16), cc, preferred_element_type=F32)
         + jnp.dot(p_n.astype(BF16), cn, preferred_element_type=F32)) / l
    o_ref[...] = o.astype(BF16).reshape(hn, ts, KV_LORA)


def _attn_sample(qa, q, cache_c, cache_kr, ckvb, krp, row0, bs, ts, past):
    rb0 = row0 // ts
    return pl.pallas_call(
        functools.partial(_attn_sample_kernel, ts=ts, past=past),
        grid=(bs,),
        in_specs=[pl.BlockSpec((N_HEADS, ts, KV_LORA), lambda b: (0, b, 0)),
                  pl.BlockSpec((ts, N_HEADS * HEAD_W), lambda b: (rb0 + b, 0)),
                  pl.BlockSpec((1, past, KV_LORA), lambda b: (b, 0, 0)),
                  pl.BlockSpec((1, past, LANES), lambda b: (b, 0, 0)),
                  pl.BlockSpec((ts, KV_LORA), lambda b: (rb0 + b, 0)),
                  pl.BlockSpec((ts, LANES), lambda b: (rb0 + b, 0))],
        out_specs=pl.BlockSpec((N_HEADS, ts, KV_LORA), lambda b: (0, b, 0)),
        out_shape=jax.ShapeDtypeStruct((N_HEADS, bs * ts, KV_LORA), BF16),
        compiler_params=_cparams(("arbitrary",)),
        name="attn_sample",
    )(qa, q, cache_c, cache_kr, ckvb, krp)


def _unabsorb_kernel(ol_ref, wv_ref, o_ref):
    o_ref[...] = jnp.dot(ol_ref[0], wv_ref[...], preferred_element_type=F32).astype(BF16)


def _unabsorb(o_lat, wv, rows):
    return pl.pallas_call(
        _unabsorb_kernel,
        grid=(N_HEADS,),
        in_specs=[pl.BlockSpec((1, rows, KV_LORA), lambda h: (h, 0, 0)),
                  pl.BlockSpec((KV_LORA, V_DIM), lambda h: (0, h))],
        out_specs=pl.BlockSpec((rows, V_DIM), lambda h: (0, h)),
        out_shape=jax.ShapeDtypeStruct((rows, N_HEADS * V_DIM), BF16),
        compiler_params=_cparams(("arbitrary",)),
        name="unabsorb",
    )(o_lat, wv)


def _post_mixer_kernel(op_ref, os_ref, wo_ref, x_ref, gate_ref, g2_ref, sh_ref, sc_ref, rw_ref, rb_ref,
                       x1_ref, hp_ref, te_ref, tg_ref, pos_ref, cnt_ref, carry, *, prompt_tiles):
    @pl.when(pl.program_id(0) == 0)
    def _():
        carry[...] = jnp.zeros(carry.shape, F32)

    o = jnp.where(pl.program_id(0) < prompt_tiles, op_ref[...], os_ref[...])
    y = jnp.dot(o, wo_ref[...], preferred_element_type=F32)
    x1 = _gated_residual(x_ref[...], gate_ref[...], y)
    x1_ref[...] = x1
    h2 = _norm_mod(x1, g2_ref[...], sh_ref[...], sc_ref[...])
    hp_ref[...] = h2
    rw = rw_ref[...]
    h_hi = h2.astype(BF16)
    h_lo = (h2 - h_hi.astype(F32)).astype(BF16)
    w_hi = rw.astype(BF16)
    w_lo = (rw - w_hi.astype(F32)).astype(BF16)
    logits = (jnp.dot(h_hi, w_hi, preferred_element_type=F32) + jnp.dot(h_lo, w_hi, preferred_element_type=F32)
              + jnp.dot(h_hi, w_lo, preferred_element_type=F32) + rb_ref[...])
    tm, ne = logits.shape
    eid = lax.broadcasted_iota(I32, (tm, ne), 1)
    lane = lax.broadcasted_iota(I32, (tm, LANES), 1)
    te = jnp.zeros((tm, LANES), I32)
    tv = jnp.full((tm, LANES), -jnp.inf, F32)
    work = logits
    picks = []
    for k in range(TOP_K):
        mx = jnp.max(work, axis=-1, keepdims=True)
        idx = jnp.min(jnp.where(work == mx, eid, ne), axis=-1, keepdims=True)
        picks.append(idx)
        te = jnp.where(lane == k, idx, te)
        tv = jnp.where(lane == k, mx, tv)
        work = jnp.where(eid == idx, -jnp.inf, work)
    ex = jnp.exp(tv - jnp.max(tv, axis=-1, keepdims=True))
    te_ref[...] = te
    tg_ref[...] = ex / jnp.sum(ex, axis=-1, keepdims=True)
    onehot = jnp.zeros((tm, LANES), F32)
    for idx in picks:
        onehot = onehot + (lane == idx).astype(F32)
    tri = (lax.broadcasted_iota(I32, (tm, tm), 1) < lax.broadcasted_iota(I32, (tm, tm), 0)).astype(BF16)
    rank = jnp.dot(tri, onehot.astype(BF16), preferred_element_type=F32) + carry[0:1, :]
    pos = jnp.zeros((tm, LANES), I32)
    for k, idx in enumerate(picks):
        pk = jnp.sum(jnp.where(lane == idx, rank, 0.0), axis=-1, keepdims=True)
        pos = jnp.where(lane == k, pk.astype(I32), pos)
    pos_ref[...] = pos
    total = carry[0:1, :] + jnp.sum(onehot, axis=0, keepdims=True)
    carry[...] = jnp.broadcast_to(total, carry.shape)
    cnt_ref[...] = carry[...]


def _post_mixer(o_p, o_s, wo, x, g2, modg, rw, rb, group):
    n, d = x.shape
    tm = ROW_TILE
    ng = tm // group
    npt = o_p.shape[0] // tm
    nst = o_s.shape[0] // tm
    row = lambda i: (i, 0)
    const = lambda i: (0, 0)
    return pl.pallas_call(
        functools.partial(_post_mixer_kernel, prompt_tiles=npt),
        grid=(n // tm,),
        in_specs=[pl.BlockSpec((tm, o_p.shape[1]), lambda i: (jnp.minimum(i, npt - 1), 0)),
                  pl.BlockSpec((tm, o_s.shape[1]), lambda i: (jnp.clip(i - npt, 0, nst - 1), 0)),
                  pl.BlockSpec(wo.shape, const),
                  pl.BlockSpec((tm, d), row),
                  pl.BlockSpec((ng, d), lambda i: (i, 2)),
                  pl.BlockSpec((1, d), const),
                  pl.BlockSpec((ng, d), lambda i: (i, 3)),
                  pl.BlockSpec((ng, d), lambda i: (i, 4)),
                  pl.BlockSpec(rw.shape, const),
                  pl.BlockSpec((1, rw.shape[1]), const)],
        out_specs=[pl.BlockSpec((tm, d), row),
                   pl.BlockSpec((tm, d), row),
                   pl.BlockSpec((tm, LANES), row),
                   pl.BlockSpec((tm, LANES), row),
                   pl.BlockSpec((tm, LANES), row),
                   pl.BlockSpec((SUBLANES, LANES), const)],
        out_shape=[jax.ShapeDtypeStruct((n, d), F32),
                   jax.ShapeDtypeStruct((n, d), F32),
                   jax.ShapeDtypeStruct((n, LANES), I32),
                   jax.ShapeDtypeStruct((n, LANES), F32),
                   jax.ShapeDtypeStruct((n, LANES), I32),
                   jax.ShapeDtypeStruct((SUBLANES, LANES), F32)],
        scratch_shapes=[pltpu.VMEM((SUBLANES, LANES), F32)],
        compiler_params=_cparams(("arbitrary",)),
        name="post_mixer",
    )(o_p, o_s, wo, x, modg, g2.reshape(1, d), modg, modg, rw, rb.reshape(1, -1))


DEST_GROUP = LANES // TOP_K


def _moe_dest_kernel(te_ref, pos_ref, cnt_ref, dest_ref, meta_ref, *, n_blocks):
    shift = MOE_TM.bit_length() - 1
    lane8 = lax.broadcasted_iota(I32, (SUBLANES, LANES), 1)
    cnt = cnt_ref[...].astype(I32)
    padded = ((cnt + (MOE_TM - 1)) >> shift) << shift
    ends = padded.astype(F32)
    s = 1
    while s < N_EXPERTS:
        ends = ends + jnp.where(lane8 >= s, pltpu.roll(ends, s, 1), 0.0)
        s *= 2
    ends_row = ends[0:1, :]
    starts_row = (ends - padded.astype(F32))[0:1, :]
    te = te_ref[...]
    pos = pos_ref[...]
    tm = te.shape[0]
    lane = lax.broadcasted_iota(I32, (tm, LANES), 1)
    dest = jnp.zeros((tm, LANES), F32)
    for k in range(TOP_K):
        sk = jnp.sum(jnp.where(lane == te[:, k:k + 1], starts_row, 0.0), axis=-1, keepdims=True)
        dest = jnp.where(lane == k, sk + pos[:, k:k + 1].astype(F32), dest)
    hi = jnp.floor(dest * (1.0 / 256.0))
    lo = dest - 256.0 * hi
    sel = (lax.broadcasted_iota(I32, (LANES, LANES), 0)
           == lax.broadcasted_iota(I32, (LANES, LANES), 1) % TOP_K).astype(BF16)
    spread = (256.0 * jnp.dot(hi.astype(BF16), sel, preferred_element_type=F32)
              + jnp.dot(lo.astype(BF16), sel, preferred_element_type=F32))
    row = lax.broadcasted_iota(I32, (tm, LANES), 0)
    keep = lane // TOP_K == row % DEST_GROUP
    dense = jnp.sum(jnp.where(keep, spread, 0.0).reshape(tm // DEST_GROUP, DEST_GROUP, LANES), axis=1)
    dest_ref[...] = dense.astype(I32)

    @pl.when(pl.program_id(0) == 0)
    def _():
        nl = meta_ref.shape[1]
        r_i = lax.broadcasted_iota(I32, (LANES, LANES), 0)
        l_i = lax.broadcasted_iota(I32, (LANES, LANES), 1)
        ends_col = jnp.sum(jnp.where(l_i == r_i, ends_row, 0.0), axis=-1, keepdims=True)
        e_i = lax.broadcasted_iota(I32, (LANES, nl), 0)
        b_i = lax.broadcasted_iota(I32, (LANES, nl), 1)
        closed = (e_i < N_EXPERTS) & (ends_col <= (b_i * MOE_TM).astype(F32))
        be = jnp.minimum(jnp.sum(jnp.where(closed, 1.0, 0.0), axis=0, keepdims=True), N_EXPERTS - 1.0)
        total = jnp.sum(jnp.where(lane8[0:1, :] == N_EXPERTS - 1, ends_row, 0.0), axis=-1, keepdims=True)
        n_used = (total.astype(I32) >> shift).astype(F32)
        meta = jnp.where(b_i[0:1, :] < n_blocks, be, n_used).astype(I32)
        meta_ref[...] = jnp.broadcast_to(meta, meta_ref.shape)


def _moe_dest(te128, pos128, cnt, n_blocks):
    n = te128.shape[0]
    tm = ROW_TILE
    nl = -(-(n_blocks + 1) // LANES) * LANES
    row = lambda i: (i, 0)
    const = lambda i: (0, 0)
    return pl.pallas_call(
        functools.partial(_moe_dest_kernel, n_blocks=n_blocks),
        grid=(n // tm,),
        in_specs=[pl.BlockSpec((tm, LANES), row),
                  pl.BlockSpec((tm, LANES), row),
                  pl.BlockSpec((SUBLANES, LANES), const)],
        out_specs=[pl.BlockSpec((tm // DEST_GROUP, LANES), row),
                   pl.BlockSpec((SUBLANES, nl), const)],
        out_shape=[jax.ShapeDtypeStruct((n // DEST_GROUP, LANES), I32),
                   jax.ShapeDtypeStruct((SUBLANES, nl), I32)],
        compiler_params=_cparams(("arbitrary",)),
        name="moe_dest",
    )(te128, pos128, cnt)


def _row_token_kernel(dest_ref, rt_ref, *, n_assign):
    def clear(r, c):
        rt_ref[r] = 0
        return c

    def place(t, c):
        for k in range(TOP_K):
            rt_ref[dest_ref[t * TOP_K + k]] = t
        return c

    lax.fori_loop(0, rt_ref.shape[0], clear, 0, unroll=8)
    lax.fori_loop(0, n_assign // TOP_K, place, 0, unroll=4)


def _row_token(dest, rows_total):
    return pl.pallas_call(
        functools.partial(_row_token_kernel, n_assign=dest.shape[0]),
        grid_spec=pltpu.PrefetchScalarGridSpec(
            num_scalar_prefetch=1,
            grid=(1,),
            in_specs=[],
            out_specs=pl.BlockSpec(memory_space=pltpu.SMEM)),
        out_shape=jax.ShapeDtypeStruct((rows_total,), I32),
        compiler_params=_cparams(("arbitrary",)),
        name="moe_row_token",
    )(dest)


def _dispatch_kernel(rt_ref, nb_ref, h_ref, o_ref, buf, sem, *, tg):
    i = pl.program_id(0)
    n_used = (nb_ref[0] * MOE_TM + tg - 1) // tg
    slot = i % 2
    unroll = 8

    def issue(blk, s):
        def body(g, c):
            for u in range(unroll):
                r = g * unroll + u
                tok = rt_ref[blk * tg + r]
                pltpu.make_async_copy(h_ref.at[pl.ds(tok, 1), :], buf.at[s, pl.ds(r, 1), :],
                                      sem.at[s]).start(priority=u % 2)
            return c
        lax.fori_loop(0, tg // unroll, body, 0)

    @pl.when(i == 0)
    def _():
        issue(0, 0)

    @pl.when(i + 1 < n_used)
    def _():
        issue(i + 1, 1 - slot)

    @pl.when(i < n_used)
    def _():
        pltpu.make_async_copy(h_ref.at[pl.ds(0, tg), :], buf.at[slot], sem.at[slot]).wait()
        o_ref[...] = buf[slot].astype(BF16)

    @pl.when(i >= n_used)
    def _():
        o_ref[...] = jnp.zeros(o_ref.shape, o_ref.dtype)


def _dispatch(row_tok, n_blocks_used, h, rows_total):
    tg = 2 * MOE_TM
    d = h.shape[1]
    return pl.pallas_call(
        functools.partial(_dispatch_kernel, tg=tg),
        grid_spec=pltpu.PrefetchScalarGridSpec(
            num_scalar_prefetch=2,
            grid=(rows_total // tg,),
            in_specs=[pl.BlockSpec(memory_space=pl.ANY)],
            out_specs=pl.BlockSpec((tg, d), lambda i, rt, nb: (i, 0)),
            scratch_shapes=[pltpu.VMEM((2, tg, d), F32), pltpu.SemaphoreType.DMA((2,))]),
        out_shape=jax.ShapeDtypeStruct((rows_total, d), BF16),
        compiler_params=_cparams(("arbitrary",)),
        name="moe_dispatch",
    )(row_tok, n_blocks_used, h)


def _new_expert(be_ref, b):
    return (b == 0) | (be_ref[b] != be_ref[jnp.maximum(b - 1, 0)])


def _swiglu_pairs(v):
    g = jnp.minimum(v, SWIGLU_LIMIT)
    glu = g * jax.nn.sigmoid(g * SWIGLU_ALPHA)
    up1 = jnp.clip(v, -SWIGLU_LIMIT, SWIGLU_LIMIT) + 1.0
    return glu, up1


WEIGHT_DMA_PRIORITY = 1


MOE_SUB_GU = 2
MOE_SUB_DOWN = 2


def _stream_expert_weights(be_ref, nb_ref, run_ctr, copies, consume, b):
    j = pl.program_id(0)
    nj = pl.num_programs(0)
    nb = nb_ref[0]
    last_blk = be_ref.shape[0] - 1
    e = be_ref[b]

    @pl.when((j == 0) & (b == 0))
    def _():
        run_ctr[0] = 0
        for c in copies(0, e, 0):
            c.start(priority=WEIGHT_DMA_PRIORITY)

    @pl.when((b < nb) & _new_expert(be_ref, b))
    def _():
        k = run_ctr[0]
        slot = k % 2
        for c in copies(j, e, slot):
            c.wait()
        consume(slot)
        run_end = lax.while_loop(lambda bb: (bb < nb) & (be_ref[jnp.minimum(bb, last_blk)] == e),
                                 lambda bb: bb + 1, b + 1)
        more_runs = run_end < nb
        j_next = jnp.where(more_runs, j, j + 1)
        e_next = jnp.where(more_runs, be_ref[jnp.minimum(run_end, last_blk)], be_ref[0])

        @pl.when(more_runs | (j + 1 < nj))
        def _():
            for c in copies(j_next, e_next, 1 - slot):
                c.start(priority=WEIGHT_DMA_PRIORITY)
        run_ctr[0] = k + 1


def _moe_gu_kernel(be_ref, nb_ref, xb_ref, w_ref, bias_ref, o_ref, wa_s, wb_s, wbuf, sem, run_ctr,
                   *, layer, tn, nj):
    j = pl.program_id(0)
    nb = nb_ref[0]
    tm = MOE_TM

    def copies(jj, e, s):
        return [pltpu.make_async_copy(
            w_ref.at[layer, e, :, pl.ds(pl.multiple_of((jj + h * nj) * tn, tn), tn)], wbuf.at[s, h], sem.at[s])
            for h in range(2)]

    def consume(s):
        wa_s[...] = wbuf[s, 0].astype(BF16)
        wb_s[...] = wbuf[s, 1].astype(BF16)

    def block(u):
        b = pl.program_id(1) * MOE_SUB_GU + u
        rows = slice(u * tm, (u + 1) * tm)
        _stream_expert_weights(be_ref, nb_ref, run_ctr, copies, consume, b)

        @pl.when(b < nb)
        def _():
            x = xb_ref[rows, :]
            brow = be_ref[b] * (2 * nj) + j
            ga = jnp.dot(x, wa_s[...], preferred_element_type=F32) + bias_ref[pl.ds(brow, 1), :]
            gb = jnp.dot(x, wb_s[...], preferred_element_type=F32) + bias_ref[pl.ds(brow + nj, 1), :]
            even = lax.broadcasted_iota(I32, (tm, LANES), 1) % 2 == 0
            for c in range(tn // LANES):
                glu_a, up_a = _swiglu_pairs(ga[:, c * LANES:(c + 1) * LANES])
                glu_b, up_b = _swiglu_pairs(gb[:, c * LANES:(c + 1) * LANES])
                ra = glu_a * pltpu.roll(up_a, LANES - 1, 1)
                rb = pltpu.roll(glu_b, 1, 1) * up_b
                o_ref[rows, c * LANES:(c + 1) * LANES] = jnp.where(even, ra, rb).astype(BF16)

        @pl.when(b >= nb)
        def _():
            o_ref[rows, :] = jnp.zeros((tm, tn), o_ref.dtype)

    for u in range(MOE_SUB_GU):
        block(u)


def _moe_gu(block_e, n_blocks_used, xb, wgu_all, layer, bgu):
    rows, d = xb.shape
    ne, f2 = bgu.shape
    tm = MOE_TM
    tn = 1024
    nj = f2 // 2 // tn
    bias = bgu.reshape(ne * 2 * nj, tn)
    return pl.pallas_call(
        functools.partial(_moe_gu_kernel, layer=layer, tn=tn, nj=nj),
        grid_spec=pltpu.PrefetchScalarGridSpec(
            num_scalar_prefetch=2,
            grid=(nj, rows // (tm * MOE_SUB_GU)),
            in_specs=[pl.BlockSpec((tm * MOE_SUB_GU, d), lambda j, b, be, nb: (b, 0)),
                      pl.BlockSpec(memory_space=pl.ANY),
                      pl.BlockSpec(bias.shape, lambda j, b, be, nb: (0, 0))],
            out_specs=pl.BlockSpec((tm * MOE_SUB_GU, tn), lambda j, b, be, nb: (b, j)),
            scratch_shapes=[pltpu.VMEM((d, tn), BF16), pltpu.VMEM((d, tn), BF16),
                            pltpu.VMEM((2, 2, d, tn), F32), pltpu.SemaphoreType.DMA((2,)),
                            pltpu.SMEM((1,), I32)]),
        out_shape=jax.ShapeDtypeStruct((rows, f2 // 2), BF16),
        compiler_params=_cparams(("arbitrary", "arbitrary")),
        name="moe_gate_up",
    )(block_e, n_blocks_used, xb, wgu_all, bias)


def _moe_down_kernel(be_ref, nb_ref, a_ref, w_ref, bd_ref, o_ref, wp_s, stage, wbuf, sem, run_ctr, *, layer, tn):
    nb = nb_ref[0]
    tm = MOE_TM

    def copies(j, e, slot):
        return [pltpu.make_async_copy(w_ref.at[layer, e, :, pl.ds(pl.multiple_of(j * tn, tn), tn)], wbuf.at[slot],
                                      sem.at[slot])]

    def consume(slot):
        f = wbuf.shape[1]
        ns = stage.shape[0]
        for c in range(tn // LANES):
            cols = slice(c * LANES, (c + 1) * LANES)
            stage[c % ns, pl.ds(0, f // 2, stride=2), :] = wbuf[slot, :f // 2, cols]
            stage[c % ns, pl.ds(1, f // 2, stride=2), :] = wbuf[slot, f // 2:, cols]
            wp_s[:, cols] = stage[c % ns].astype(BF16)

    def block(u):
        b = pl.program_id(1) * MOE_SUB_DOWN + u
        rows = slice(u * tm, (u + 1) * tm)
        _stream_expert_weights(be_ref, nb_ref, run_ctr, copies, consume, b)

        @pl.when(b < nb)
        def _():
            brow = be_ref[b] * pl.num_programs(0) + pl.program_id(0)
            o_ref[rows, :] = (jnp.dot(a_ref[rows, :], wp_s[...], preferred_element_type=F32)
                              + bd_ref[pl.ds(brow, 1), :])

        @pl.when(b >= nb)
        def _():
            o_ref[rows, :] = jnp.zeros((tm, tn), o_ref.dtype)

    for u in range(MOE_SUB_DOWN):
        block(u)


def _moe_down(block_e, n_blocks_used, act, wd_all, layer, bd):
    rows, f = act.shape
    ne, d = bd.shape
    tm = MOE_TM
    tn = d
    bias = bd.reshape(ne * (d // tn), tn)
    return pl.pallas_call(
        functools.partial(_moe_down_kernel, layer=layer, tn=tn),
        grid_spec=pltpu.PrefetchScalarGridSpec(
            num_scalar_prefetch=2,
            grid=(d // tn, rows // (tm * MOE_SUB_DOWN)),
            in_specs=[pl.BlockSpec((tm * MOE_SUB_DOWN, f), lambda j, b, be, nb: (b, 0)),
                      pl.BlockSpec(memory_space=pl.ANY),
                      pl.BlockSpec(bias.shape, lambda j, b, be, nb: (0, 0))],
            out_specs=pl.BlockSpec((tm * MOE_SUB_DOWN, tn), lambda j, b, be, nb: (b, j)),
            scratch_shapes=[pltpu.VMEM((f, tn), BF16), pltpu.VMEM((2, f, LANES), F32),
                            pltpu.VMEM((2, f, tn), F32), pltpu.SemaphoreType.DMA((2,)), pltpu.SMEM((1,), I32)]),
        out_shape=jax.ShapeDtypeStruct((rows, d), F32),
        compiler_params=_cparams(("arbitrary", "arbitrary"), VMEM_LIMIT_DOWN),
        name="moe_down",
    )(block_e, n_blocks_used, act, wd_all, bias)


def _combine_kernel(dest_ref, y_ref, x1_ref, gm_ref, tg_ref, fg_ref, o_ref, buf, sem, *, tn, tile0, final_norm):
    i = pl.program_id(0) + tile0

    def issue(t, c):
        for k in range(TOP_K):
            d = dest_ref[(i * tn + t) * TOP_K + k]
            pltpu.make_async_copy(y_ref.at[pl.ds(d, 1), :], buf.at[k, pl.ds(t, 1), :], sem).start(priority=k % 2)
        return c

    lax.fori_loop(0, tn, issue, 0, unroll=8)
    for k in range(TOP_K):
        pltpu.make_async_copy(y_ref.at[pl.ds(0, tn), :], buf.at[k], sem).wait()
    tg = tg_ref[...]
    moe = tg[:, 0:1] * buf[0]
    for k in range(1, TOP_K):
        moe = moe + tg[:, k:k + 1] * buf[k]
    x2 = _gated_residual(x1_ref[...], gm_ref[...], moe)
    if final_norm:
        x2 = _rms(x2, fg_ref[...])
    o_ref[...] = x2


def _combine(dest, y, x1, modg, tgates, fg, group, final_norm, row0, rows):
    d = x1.shape[1]
    tn = ROW_TILE
    ng = tn // group
    t0 = row0 // tn
    return pl.pallas_call(
        functools.partial(_combine_kernel, tn=tn, tile0=t0, final_norm=final_norm),
        grid_spec=pltpu.PrefetchScalarGridSpec(
            num_scalar_prefetch=1,
            grid=(rows // tn,),
            in_specs=[pl.BlockSpec(memory_space=pl.ANY),
                      pl.BlockSpec((tn, d), lambda i, ds: (i + t0, 0)),
                      pl.BlockSpec((ng, d), lambda i, ds: (i + t0, 5)),
                      pl.BlockSpec((tn, LANES), lambda i, ds: (i + t0, 0)),
                      pl.BlockSpec((1, d), lambda i, ds: (0, 0))],
            out_specs=pl.BlockSpec((tn, d), lambda i, ds: (i, 0)),
            scratch_shapes=[pltpu.VMEM((TOP_K, tn, d), F32), pltpu.SemaphoreType.DMA(())]),
        out_shape=jax.ShapeDtypeStruct((rows, d), F32),
        compiler_params=_cparams(("arbitrary",)),
        name="moe_combine",
    )(dest, y, x1, modg, tgates, fg.reshape(1, d))


def _moe(hp, te128, tg128, pos128, cnt, x1, modg, wgu_all, bgu, wd_all, bd, layer, fg, group, splits):
    n = x1.shape[0]
    rows_total = n * TOP_K + N_EXPERTS * MOE_TM
    n_blocks = rows_total // MOE_TM
    dest2d, meta = _moe_dest(te128, pos128, cnt, n_blocks)
    dest = dest2d.reshape(-1)
    block_e = meta[0, :n_blocks]
    nbu = meta[0, n_blocks:n_blocks + 1]
    row_tok = _row_token(dest, rows_total)
    xb = _dispatch(row_tok, nbu, hp, rows_total)
    act = _moe_gu(block_e, nbu, xb, wgu_all, layer, bgu)
    y = _moe_down(block_e, nbu, act, wd_all, layer, bd)
    return [_combine(dest, y, x1, modg, tg128, fg, group, fn, r0, rows) for r0, rows, fn in splits]


def _lru_in_kernel(x_ref, g1_ref, sh_ref, sc_ref, wy_ref, wx_ref, y_ref, xb_ref):
    h = _norm_mod(x_ref[...], g1_ref[...], sh_ref[...], sc_ref[...]).astype(BF16)
    y = jnp.dot(h, wy_ref[...], preferred_element_type=F32)
    y_ref[...] = jax.nn.gelu(y, approximate=True).astype(BF16)
    xb_ref[...] = jnp.dot(h, wx_ref[...], preferred_element_type=F32)


def _lru_in(x, g1, modg, wy, wx, group):
    n, d = x.shape
    dr = wy.shape[1]
    tm = ROW_TILE
    ng = tm // group
    row = lambda i: (i, 0)
    const = lambda i: (0, 0)
    return pl.pallas_call(
        _lru_in_kernel,
        grid=(n // tm,),
        in_specs=[pl.BlockSpec((tm, d), row),
                  pl.BlockSpec((1, d), const),
                  pl.BlockSpec((ng, d), lambda i: (i, 0)),
                  pl.BlockSpec((ng, d), lambda i: (i, 1)),
                  pl.BlockSpec((d, dr), const),
                  pl.BlockSpec((d, dr), const)],
        out_specs=[pl.BlockSpec((tm, dr), row), pl.BlockSpec((tm, dr), row)],
        out_shape=[jax.ShapeDtypeStruct((n, dr), BF16), jax.ShapeDtypeStruct((n, dr), F32)],
        compiler_params=_cparams(("arbitrary",)),
        name="lru_in",
    )(x, g1.reshape(1, d), modg, modg, wy, wx)


def _lru_scan_kernel(y_ref, xb_ref, cb_ref, h0_ref, cw_ref, cbias_ref, wa_ref, ba_ref, wx_ref, bx_ref, lam_ref,
                     hy_ref, cbo_ref, ho_ref, xe, a_s, u_s, hc, *, tc, starts_at_pos0):
    c = pl.program_id(1)
    dr = xb_ref.shape[1]
    nb = wa_ref.shape[0]
    bd = dr // nb
    pre = SUBLANES

    @pl.when(c == 0)
    def _():
        xe[0:pre, :] = jnp.zeros((pre, dr), F32)
        xe[pre - (CONV_W - 1):pre, :] = cb_ref[0]
        hc[...] = jnp.broadcast_to(h0_ref[0], (SUBLANES, dr))

    xe[pre:pre + tc, :] = xb_ref[...]
    cw = cw_ref[...]
    xc = cbias_ref[...] + xe[pre:pre + tc, :] * cw[CONV_W - 1:CONV_W, :]
    for k in range(1, CONV_W):
        xc = xc + xe[pre - k:pre - k + tc, :] * cw[CONV_W - 1 - k:CONV_W - k, :]
    cbo_ref[0] = xe[pre + tc - (CONV_W - 1):pre + tc, :]
    xe[0:pre, :] = xe[tc:tc + pre, :]

    xcb = xc.astype(BF16)
    ra = jnp.concatenate([jnp.dot(xcb[:, n * bd:(n + 1) * bd], wa_ref[n], preferred_element_type=F32)
                          for n in range(nb)], axis=1)
    rx = jnp.concatenate([jnp.dot(xcb[:, n * bd:(n + 1) * bd], wx_ref[n], preferred_element_type=F32)
                          for n in range(nb)], axis=1)
    r = jax.nn.sigmoid(ra + ba_ref[...])
    ig = jax.nn.sigmoid(rx + bx_ref[...])
    lam = lam_ref[...]
    log_sig = jnp.minimum(lam, 0.0) - jnp.log1p(jnp.exp(-jnp.abs(lam)))
    log_a = LRU_C * r * log_sig
    a = jnp.exp(log_a)
    th = jnp.tanh(log_a)
    mult = jnp.sqrt(-2.0 * th / (1.0 - th))
    if starts_at_pos0:
        first = (lax.broadcasted_iota(I32, (tc, 1), 0) == 0) & (c == 0)
        mult = jnp.where(first, 1.0, mult)
    a_s[...] = a
    u_s[...] = mult * ig * xc

    row8 = lax.broadcasted_iota(I32, (SUBLANES, dr), 0)

    def group_step(g, hprev):
        off = pl.multiple_of(g * SUBLANES, SUBLANES)
        aa = a_s[pl.ds(off, SUBLANES), :]
        uu = u_s[pl.ds(off, SUBLANES), :]
        s = 1
        while s < SUBLANES:
            m = row8 >= s
            uu = jnp.where(m, aa * pltpu.roll(uu, s, 0) + uu, uu)
            aa = jnp.where(m, aa * pltpu.roll(aa, s, 0), aa)
            s *= 2
        hh = aa * hprev + uu
        u_s[pl.ds(off, SUBLANES), :] = hh
        return jnp.broadcast_to(hh[SUBLANES - 1:SUBLANES, :], (SUBLANES, dr))

    hlast = lax.fori_loop(0, tc // SUBLANES, group_step, hc[...])
    hc[...] = hlast
    ho_ref[0] = hlast[0:1, :]
    hy_ref[...] = (u_s[...] * y_ref[...].astype(F32)).astype(BF16)


def _lru_scan(yb, xb, conv_buf, h0, cw, cbias, wa, ba, wx, bx, lam, row0, n_seq, t, tc, starts_at_pos0):
    dr = xb.shape[1]
    nc = t // tc
    rb0 = row0 // tc
    inmap = lambda s, c: (rb0 + s * nc + c, 0)
    outmap = lambda s, c: (s * nc + c, 0)
    const2 = lambda s, c: (0, 0)
    const3 = lambda s, c: (0, 0, 0)
    seq3 = lambda s, c: (s, 0, 0)
    return pl.pallas_call(
        functools.partial(_lru_scan_kernel, tc=tc, starts_at_pos0=starts_at_pos0),
        grid=(n_seq, nc),
        in_specs=[pl.BlockSpec((tc, dr), inmap),
                  pl.BlockSpec((tc, dr), inmap),
                  pl.BlockSpec((1, CONV_W - 1, dr), seq3),
                  pl.BlockSpec((1, 1, dr), seq3),
                  pl.BlockSpec((CONV_W, dr), const2),
                  pl.BlockSpec((1, dr), const2),
                  pl.BlockSpec(wa.shape, const3),
                  pl.BlockSpec((1, dr), const2),
                  pl.BlockSpec(wx.shape, const3),
                  pl.BlockSpec((1, dr), const2),
                  pl.BlockSpec((1, dr), const2)],
        out_specs=[pl.BlockSpec((tc, dr), outmap),
                   pl.BlockSpec((1, CONV_W - 1, dr), seq3),
                   pl.BlockSpec((1, 1, dr), seq3)],
        out_shape=[jax.ShapeDtypeStruct((n_seq * t, dr), BF16),
                   jax.ShapeDtypeStruct((n_seq, CONV_W - 1, dr), F32),
                   jax.ShapeDtypeStruct((n_seq, 1, dr), F32)],
        scratch_shapes=[pltpu.VMEM((SUBLANES + tc, dr), F32),
                        pltpu.VMEM((tc, dr), F32),
                        pltpu.VMEM((tc, dr), F32),
                        pltpu.VMEM((SUBLANES, dr), F32)],
        compiler_params=_cparams(("arbitrary", "arbitrary")),
        name="lru_scan",
    )(yb, xb, conv_buf, h0.reshape(n_seq, 1, dr), cw, cbias.reshape(1, dr), wa, ba.reshape(1, dr),
      wx, bx.reshape(1, dr), lam.reshape(1, dr))


def _rope_tables(pos):
    half = ROPE_DIM // 2
    inv = 1.0 / (ROPE_THETA ** (jnp.arange(0, ROPE_DIM, 2, dtype=F32) / ROPE_DIM))
    ang = pos.astype(F32)[:, None] * inv[None, :]
    cos, sin = jnp.cos(ang), jnp.sin(ang)
    z = jnp.zeros((pos.shape[0], LANES - ROPE_DIM), F32)
    return jnp.concatenate([cos, cos, z], axis=1), jnp.concatenate([-sin, sin, z], axis=1)


def _head_slab_weights(w_uq):
    ql = w_uq.shape[0]
    w = w_uq.reshape(ql, N_HEADS, NOPE_DIM + ROPE_DIM)
    z = jnp.zeros((ql, N_HEADS, HEAD_W - NOPE_DIM - ROPE_DIM), w.dtype)
    return jnp.concatenate([w, z], axis=2).reshape(ql, N_HEADS * HEAD_W).astype(BF16)


def kernel(x_prompt, x_sample, cache_ckv, cache_krope, state_conv, state_h, c_prompt, c_sample,
           mod_w, mod_b, norm1_g, norm2_g,
           mla_w_in, mla_q_norm_g, mla_kv_norm_g, mla_w_uq, mla_w_ukv, mla_w_o,
           lru_w_in, lru_conv_w, lru_conv_b, lru_w_a, lru_b_a, lru_w_x, lru_b_x, lru_lambda, lru_w_o,
           router_w, router_b, moe_w_gu, moe_b_gu, moe_w_down, moe_b_down, final_g):
    bp, tp, d = x_prompt.shape
    bs, ts, _ = x_sample.shape
    past = cache_ckv.shape[2]
    depth = mod_w.shape[0]
    n_p, n_s = bp * tp, bs * ts
    n = n_p + n_s
    group = math.gcd(tp, ts)
    assert group % SUBLANES == 0 and ROW_TILE % group == 0 and n_p % ROW_TILE == 0 and n_s % ROW_TILE == 0
    assert NOPE_DIM == LANES and V_DIM == LANES and ROPE_DIM <= LANES and n_p % n_s == 0

    x = jnp.concatenate([x_prompt.reshape(n_p, d), x_sample.reshape(n_s, d)], axis=0)
    nb = bp + bs
    nb_pad = -(-nb // SUBLANES) * SUBLANES
    c_all = jnp.concatenate([c_prompt, c_sample, jnp.zeros((nb_pad - nb, d), F32)], axis=0)
    grp_batch = np.concatenate([np.repeat(np.arange(bp), tp // group), bp + np.repeat(np.arange(bs), ts // group)])
    assert tp % ROW_TILE == 0 and ROW_TILE % ts == 0
    pos = jnp.concatenate([jnp.arange(tp), jnp.tile(past + jnp.arange(ts), ROW_TILE // ts)])
    cos, sin = _rope_tables(pos)

    outs = {k: [] for k in ("ckv_p", "kr_p", "conv_p", "h_p", "ckv_s", "kr_s", "conv_s", "h_s")}
    for i in range(depth):
        mod = _adaln(c_all, mod_w, i, mod_b[i])
        modg = jnp.take(mod, jnp.asarray(grp_batch), axis=0)
        j = i // 2
        if i % 2 == 0:
            w_in = mla_w_in[j]
            zpad = jnp.zeros((d, LANES - ROPE_DIM), F32)
            win = jnp.concatenate([w_in, zpad], axis=1).astype(BF16)
            wuq = _head_slab_weights(mla_w_uq[j])
            wukv = mla_w_ukv[j].reshape(KV_LORA, N_HEADS, NOPE_DIM + V_DIM)
            wk = wukv[:, :, :NOPE_DIM].reshape(KV_LORA, N_HEADS * NOPE_DIM).astype(BF16)
            wv = wukv[:, :, NOPE_DIM:].reshape(KV_LORA, N_HEADS * V_DIM).astype(BF16)
            q, ckv, kr, ckvb, krp = _mla_proj(x, norm1_g[i], modg, win, mla_q_norm_g[j], mla_kv_norm_g[j],
                                              wuq, cos, sin, group, n_p // ROW_TILE, tp // ROW_TILE)
            kk, vv = _kv_expand(ckvb, krp, wk, wv, n_p)
            o_p = _attn_prompt(q, kk, vv, bp, tp)
            qa = _absorb(q, wk, n_p, n_s)
            ckr_pad = jnp.pad(cache_krope[j], ((0, 0), (0, 0), (0, LANES - ROPE_DIM)))
            o_lat = _attn_sample(qa, q, cache_ckv[j], ckr_pad, ckvb, krp, n_p, bs, ts, past)
            o_s = _unabsorb(o_lat, wv, n_s)
            wo = mla_w_o[j].astype(BF16)
            outs["ckv_p"].append(ckv[:n_p].reshape(bp, tp, KV_LORA))
            outs["kr_p"].append(kr[:n_p].reshape(bp, tp, ROPE_DIM))
            outs["ckv_s"].append(ckv[n_p:].reshape(bs, ts, KV_LORA))
            outs["kr_s"].append(kr[n_p:].reshape(bs, ts, ROPE_DIM))
        else:
            dr = lru_w_in.shape[2] // 2
            wy = lru_w_in[j][:, :dr].astype(BF16)
            wx = lru_w_in[j][:, dr:].astype(BF16)
            yb, xb = _lru_in(x, norm1_g[i], modg, wy, wx, group)
            wa = lru_w_a[j].astype(BF16)
            wxg = lru_w_x[j].astype(BF16)
            lru_args = (lru_conv_w[j], lru_conv_b[j], wa, lru_b_a[j], wxg, lru_b_x[j], lru_lambda[j])
            zbuf = jnp.zeros((bp, CONV_W - 1, dr), F32)
            zh = jnp.zeros((bp, dr), F32)
            o_p, cb_p, h_p = _lru_scan(yb, xb, zbuf, zh, *lru_args, 0, bp, tp, ROW_TILE, True)
            o_s, cb_s, h_s = _lru_scan(yb, xb, state_conv[j], state_h[j], *lru_args, n_p, bs, ts, ts, False)
            wo = lru_w_o[j].astype(BF16)
            outs["conv_p"].append(cb_p)
            outs["h_p"].append(h_p.reshape(bp, dr))
            outs["conv_s"].append(cb_s)
            outs["h_s"].append(h_s.reshape(bs, dr))
        x1, hp, te128, tg128, pos128, cnt = _post_mixer(o_p, o_s, wo, x, norm2_g[i], modg, router_w[i], router_b[i],
                                                        group)
        last = i == depth - 1
        splits = [(0, n_p, True), (n_p, n_s, True)] if last else [(0, n, False)]
        res = _moe(hp, te128, tg128, pos128, cnt, x1, modg, moe_w_gu, moe_b_gu[i], moe_w_down, moe_b_down[i], i,
                   final_g, group, splits)
        x = res[0]
    y_prompt = res[0].reshape(bp, tp, d)
    y_sample = res[1].reshape(bs, ts, d)
    return (y_prompt, y_sample,
            jnp.stack(outs["ckv_p"]), jnp.stack(outs["kr_p"]), jnp.stack(outs["conv_p"]), jnp.stack(outs["h_p"]),
            jnp.stack(outs["ckv_s"]), jnp.stack(outs["kr_s"]), jnp.stack(outs["conv_s"]), jnp.stack(outs["h_s"]))
```

```python
import functools
import math

import jax
import jax.numpy as jnp
import numpy as np
from jax import lax
from jax.experimental import pallas as pl
from jax.experimental.pallas import tpu as pltpu

F32 = jnp.float32
BF16 = jnp.bfloat16
I32 = jnp.int32

CHUNK = 64
N_HEADS = 16
Q_LORA = 512
KV_LORA = 512
NOPE_DIM = 128
ROPE_DIM = 64
V_DIM = 128
ROPE_THETA = 10000.0
LRU_BLOCKS = 8
CONV_W = 4
LRU_C = 8.0
N_EXPERTS = 32
TOP_K = 4
SWIGLU_LIMIT = 7.0
SWIGLU_ALPHA = 1.702
N_MOD = 6
EPS = 1e-6

LANES = 128
SUBLANES = 8
HEAD_W = 2 * LANES

ROW_TILE = 256
MOE_TM = 256
VMEM_LIMIT = 56 * 1024 * 1024
VMEM_LIMIT_DOWN = 60 * 1024 * 1024


def _cparams(sem, vmem_limit=VMEM_LIMIT):
    return pltpu.CompilerParams(dimension_semantics=sem, vmem_limit_bytes=vmem_limit)


def _rms(x, g):
    ms = jnp.mean(x * x, axis=-1, keepdims=True)
    return x * lax.rsqrt(ms + EPS) * g


def _norm_mod(x, g, shift, scale):
    tm, d = x.shape
    ng = shift.shape[0]
    y = _rms(x, g).reshape(ng, tm // ng, d)
    return (y * (1.0 + scale[:, None, :]) + shift[:, None, :]).reshape(tm, d)


def _gated_residual(x, gate, y):
    tm, d = x.shape
    ng = gate.shape[0]
    return (x.reshape(ng, tm // ng, d) + gate[:, None, :] * y.reshape(ng, tm // ng, d)).reshape(tm, d)


def _adaln_kernel(c_ref, w_ref, b_ref, o_ref):
    c = c_ref[...]
    a = (c * jax.nn.sigmoid(c)).astype(BF16)
    o_ref[...] = jnp.dot(a, w_ref[0].astype(BF16), preferred_element_type=F32) + b_ref[...]


def _adaln(c_all, w_all, layer, b):
    bp, d = c_all.shape
    n = w_all.shape[2]
    tn = 1024
    return pl.pallas_call(
        _adaln_kernel,
        grid=(n // tn,),
        in_specs=[pl.BlockSpec((bp, d), lambda j: (0, 0)),
                  pl.BlockSpec((1, d, tn), lambda j: (layer, 0, j)),
                  pl.BlockSpec((1, tn), lambda j: (0, j))],
        out_specs=pl.BlockSpec((bp, tn), lambda j: (0, j)),
        out_shape=jax.ShapeDtypeStruct((bp, n), F32),
        compiler_params=_cparams(("arbitrary",)),
        name="adaln",
    )(c_all, w_all, b.reshape(1, n))


def _rope128(v, cos, sin):
    half = ROPE_DIM // 2
    lane = lax.broadcasted_iota(I32, v.shape, 1)
    sw = jnp.where(lane < half, pltpu.roll(v, LANES - half, 1), pltpu.roll(v, half, 1))
    return v * cos + sw * sin


def _mla_proj_kernel(x_ref, g1_ref, sh_ref, sc_ref, win_ref, qg_ref, kvg_ref, wuq_ref, cos_ref, sin_ref,
                     q_ref, ckv_ref, kr_ref, ckvb_ref, krp_ref):
    h = _norm_mod(x_ref[...], g1_ref[...], sh_ref[...], sc_ref[...]).astype(BF16)
    lat = jnp.dot(h, win_ref[...], preferred_element_type=F32)
    q_lat = lat[:, :Q_LORA]
    c_kv = lat[:, Q_LORA:Q_LORA + KV_LORA]
    k_r = lat[:, Q_LORA + KV_LORA:]
    qn = _rms(q_lat, qg_ref[...]).astype(BF16)
    q = jnp.dot(qn, wuq_ref[...], preferred_element_type=F32)
    cos = cos_ref[...]
    sin = sin_ref[...]
    scale = (NOPE_DIM + ROPE_DIM) ** -0.5
    for hh in range(N_HEADS):
        lo = hh * HEAD_W
        q_ref[:, lo:lo + LANES] = (q[:, lo:lo + LANES] * scale).astype(BF16)
        q_ref[:, lo + LANES:lo + HEAD_W] = (_rope128(q[:, lo + LANES:lo + HEAD_W], cos, sin) * scale).astype(BF16)
    ckv = _rms(c_kv, kvg_ref[...])
    ckv_ref[...] = ckv
    ckvb_ref[...] = ckv.astype(BF16)
    kr = _rope128(k_r, cos, sin)
    kr_ref[...] = kr[:, :ROPE_DIM]
    krp_ref[...] = kr.astype(BF16)


def _mla_proj(x, g1, modg, win, qg, kvg, wuq, cos, sin, group, prompt_tiles, tiles_per_seq):
    n, d = x.shape
    tm = ROW_TILE
    ng = tm // group
    wl = win.shape[1]
    qw = wuq.shape[1]
    row = lambda i: (i, 0)
    const = lambda i: (0, 0)
    rope = lambda i: (jnp.where(i < prompt_tiles, i % tiles_per_seq, tiles_per_seq), 0)
    return pl.pallas_call(
        _mla_proj_kernel,
        grid=(n // tm,),
        in_specs=[pl.BlockSpec((tm, d), row),
                  pl.BlockSpec((1, d), const),
                  pl.BlockSpec((ng, d), lambda i: (i, 0)),
                  pl.BlockSpec((ng, d), lambda i: (i, 1)),
                  pl.BlockSpec((d, wl), const),
                  pl.BlockSpec((1, Q_LORA), const),
                  pl.BlockSpec((1, KV_LORA), const),
                  pl.BlockSpec((Q_LORA, qw), const),
                  pl.BlockSpec((tm, LANES), rope),
                  pl.BlockSpec((tm, LANES), rope)],
        out_specs=[pl.BlockSpec((tm, qw), row),
                   pl.BlockSpec((tm, KV_LORA), row),
                   pl.BlockSpec((tm, ROPE_DIM), row),
                   pl.BlockSpec((tm, KV_LORA), row),
                   pl.BlockSpec((tm, LANES), row)],
        out_shape=[jax.ShapeDtypeStruct((n, qw), BF16),
                   jax.ShapeDtypeStruct((n, KV_LORA), F32),
                   jax.ShapeDtypeStruct((n, ROPE_DIM), F32),
                   jax.ShapeDtypeStruct((n, KV_LORA), BF16),
                   jax.ShapeDtypeStruct((n, LANES), BF16)],
        compiler_params=_cparams(("arbitrary",)),
        name="mla_proj",
    )(x, g1.reshape(1, d), modg, modg, win, qg.reshape(1, -1), kvg.reshape(1, -1), wuq, cos, sin)


def _kv_expand_kernel(c_ref, krp_ref, wk_ref, wv_ref, k_ref, v_ref):
    c = c_ref[...]
    kn = jnp.dot(c, wk_ref[...], preferred_element_type=F32).astype(BF16)
    krp = krp_ref[...]
    for hh in range(N_HEADS):
        k_ref[:, hh * HEAD_W:hh * HEAD_W + LANES] = kn[:, hh * NOPE_DIM:(hh + 1) * NOPE_DIM]
        k_ref[:, hh * HEAD_W + LANES:(hh + 1) * HEAD_W] = krp
    v_ref[...] = jnp.dot(c, wv_ref[...], preferred_element_type=F32).astype(BF16)


def _kv_expand(ckvb, krp, wk, wv, rows):
    tm = 512
    row = lambda i: (i, 0)
    const = lambda i: (0, 0)
    return pl.pallas_call(
        _kv_expand_kernel,
        grid=(rows // tm,),
        in_specs=[pl.BlockSpec((tm, KV_LORA), row),
                  pl.BlockSpec((tm, LANES), row),
                  pl.BlockSpec(wk.shape, const),
                  pl.BlockSpec(wv.shape, const)],
        out_specs=[pl.BlockSpec((tm, N_HEADS * HEAD_W), row),
                   pl.BlockSpec((tm, N_HEADS * V_DIM), row)],
        out_shape=[jax.ShapeDtypeStruct((rows, N_HEADS * HEAD_W), BF16),
                   jax.ShapeDtypeStruct((rows, N_HEADS * V_DIM), BF16)],
        compiler_params=_cparams(("arbitrary",)),
        name="kv_expand",
    )(ckvb, krp, wk, wv)


ATTN_HEADS_PER_STEP = 8


def _attn_prompt_kernel(q_ref, k_ref, v_ref, o_ref, *, tq, nq):
    qi = pl.program_id(2)
    dn = (((1,), (1,)), ((), ()))
    hs = ATTN_HEADS_PER_STEP
    r = lax.broadcasted_iota(I32, (tq, tq), 0) // CHUNK
    c = lax.broadcasted_iota(I32, (tq, tq), 1) // CHUNK
    diag_visible = c <= r
    for qs in range(nq):
        @pl.when(qi == qs)
        def _(qs=qs):
            past = qs * tq
            for h in range(hs):
                q = q_ref[:, h * HEAD_W:(h + 1) * HEAD_W]
                kcols = slice(h * HEAD_W, (h + 1) * HEAD_W)
                vcols = slice(h * V_DIM, (h + 1) * V_DIM)
                s_d = lax.dot_general(q, k_ref[past:past + tq, kcols], dn, preferred_element_type=F32)
                s_d = jnp.where(diag_visible, s_d, -jnp.inf)
                m = jnp.max(s_d, axis=-1, keepdims=True)
                if past:
                    s_f = lax.dot_general(q, k_ref[0:past, kcols], dn, preferred_element_type=F32)
                    m = jnp.maximum(m, jnp.max(s_f, axis=-1, keepdims=True))
                p_d = jnp.exp(s_d - m)
                l = jnp.sum(p_d, axis=-1, keepdims=True)
                acc = jnp.dot(p_d.astype(BF16), v_ref[past:past + tq, vcols], preferred_element_type=F32)
                if past:
                    p_f = jnp.exp(s_f - m)
                    l = l + jnp.sum(p_f, axis=-1, keepdims=True)
                    acc = acc + jnp.dot(p_f.astype(BF16), v_ref[0:past, vcols], preferred_element_type=F32)
                o_ref[:, vcols] = (acc / l).astype(BF16)


def _attn_prompt(q, k, v, bp, tp):
    n_rows = bp * tp
    tq = 256
    nq = tp // tq
    hs = ATTN_HEADS_PER_STEP
    return pl.pallas_call(
        functools.partial(_attn_prompt_kernel, tq=tq, nq=nq),
        grid=(bp, N_HEADS // hs, nq),
        in_specs=[pl.BlockSpec((tq, hs * HEAD_W), lambda b, h, i: (b * nq + i, h)),
                  pl.BlockSpec((tp, hs * HEAD_W), lambda b, h, i: (b, h)),
                  pl.BlockSpec((tp, hs * V_DIM), lambda b, h, i: (b, h))],
        out_specs=pl.BlockSpec((tq, hs * V_DIM), lambda b, h, i: (b * nq + i, h)),
        out_shape=jax.ShapeDtypeStruct((n_rows, N_HEADS * V_DIM), BF16),
        compiler_params=_cparams(("arbitrary", "arbitrary", "arbitrary")),
        name="attn_prompt",
    )(q, k, v)


def _absorb_kernel(q_ref, wk_ref, o_ref):
    dn = (((1,), (1,)), ((), ()))
    o_ref[0] = lax.dot_general(q_ref[...], wk_ref[...], dn, preferred_element_type=F32).astype(BF16)


def _absorb(q, wk, row0, rows):
    rb = row0 // rows
    return pl.pallas_call(
        _absorb_kernel,
        grid=(N_HEADS,),
        in_specs=[pl.BlockSpec((rows, LANES), lambda h: (rb, 2 * h)),
                  pl.BlockSpec((KV_LORA, NOPE_DIM), lambda h: (0, h))],
        out_specs=pl.BlockSpec((1, rows, KV_LORA), lambda h: (h, 0, 0)),
        out_shape=jax.ShapeDtypeStruct((N_HEADS, rows, KV_LORA), BF16),
        compiler_params=_cparams(("arbitrary",)),
        name="absorb",
    )(q, wk)


def _attn_sample_kernel(qa_ref, q_ref, cc_ref, ckr_ref, cn_ref, krn_ref, o_ref, *, ts, past):
    hn = N_HEADS
    qa = qa_ref[...].reshape(hn * ts, KV_LORA)
    qfull = q_ref[...]
    qr = jnp.concatenate([qfull[:, h * HEAD_W + LANES:(h + 1) * HEAD_W] for h in range(hn)], axis=0)
    cc = cc_ref[0].astype(BF16)
    ckr = ckr_ref[0].astype(BF16)
    cn = cn_ref[...]
    krn = krn_ref[...]
    dn = (((1,), (1,)), ((), ()))
    s_c = (lax.dot_general(qa, cc, dn, preferred_element_type=F32)
           + lax.dot_general(qr, ckr, dn, preferred_element_type=F32))
    s_n = (lax.dot_general(qa, cn, dn, preferred_element_type=F32)
           + lax.dot_general(qr, krn, dn, preferred_element_type=F32))
    qchunk_c = (past + lax.broadcasted_iota(I32, s_c.shape, 0) % ts) // CHUNK
    s_c = jnp.where(lax.broadcasted_iota(I32, s_c.shape, 1) // CHUNK <= qchunk_c, s_c, -jnp.inf)
    qchunk_n = (past + lax.broadcasted_iota(I32, s_n.shape, 0) % ts) // CHUNK
    s_n = jnp.where((past + lax.broadcasted_iota(I32, s_n.shape, 1)) // CHUNK <= qchunk_n, s_n, -jnp.inf)
    m = jnp.maximum(jnp.max(s_c, axis=-1, keepdims=True), jnp.max(s_n, axis=-1, keepdims=True))
    p_c = jnp.exp(s_c - m)
    p_n = jnp.exp(s_n - m)
    l = jnp.sum(p_c, axis=-1, keepdims=True) + jnp.sum(p_n, axis=-1, keepdims=True)
    o = (jnp.dot(p_c.astype(BF16), cc, preferred_element_type=F32)
         + jnp.dot(p_n.astype(BF16), cn, preferred_element_type=F32)) / l
    o_ref[...] = o.astype(BF16).reshape(hn, ts, KV_LORA)


def _attn_sample(qa, q, cache_c, cache_kr, ckvb, krp, row0, bs, ts, past):
    rb0 = row0 // ts
    return pl.pallas_call(
        functools.partial(_attn_sample_kernel, ts=ts, past=past),
        grid=(bs,),
        in_specs=[pl.BlockSpec((N_HEADS, ts, KV_LORA), lambda b: (0, b, 0)),
                  pl.BlockSpec((ts, N_HEADS * HEAD_W), lambda b: (rb0 + b, 0)),
                  pl.BlockSpec((1, past, KV_LORA), lambda b: (b, 0, 0)),
                  pl.BlockSpec((1, past, LANES), lambda b: (b, 0, 0)),
                  pl.BlockSpec((ts, KV_LORA), lambda b: (rb0 + b, 0)),
                  pl.BlockSpec((ts, LANES), lambda b: (rb0 + b, 0))],
        out_specs=pl.BlockSpec((N_HEADS, ts, KV_LORA), lambda b: (0, b, 0)),
        out_shape=jax.ShapeDtypeStruct((N_HEADS, bs * ts, KV_LORA), BF16),
        compiler_params=_cparams(("arbitrary",)),
        name="attn_sample",
    )(qa, q, cache_c, cache_kr, ckvb, krp)


def _unabsorb_kernel(ol_ref, wv_ref, o_ref):
    o_ref[...] = jnp.dot(ol_ref[0], wv_ref[...], preferred_element_type=F32).astype(BF16)


def _unabsorb(o_lat, wv, rows):
    return pl.pallas_call(
        _unabsorb_kernel,
        grid=(N_HEADS,),
        in_specs=[pl.BlockSpec((1, rows, KV_LORA), lambda h: (h, 0, 0)),
                  pl.BlockSpec((KV_LORA, V_DIM), lambda h: (0, h))],
        out_specs=pl.BlockSpec((rows, V_DIM), lambda h: (0, h)),
        out_shape=jax.ShapeDtypeStruct((rows, N_HEADS * V_DIM), BF16),
        compiler_params=_cparams(("arbitrary",)),
        name="unabsorb",
    )(o_lat, wv)


def _post_mixer_kernel(op_ref, os_ref, wo_ref, x_ref, gate_ref, g2_ref, sh_ref, sc_ref, rw_ref, rb_ref,
                       x1_ref, hp_ref, te_ref, tg_ref, pos_ref, cnt_ref, carry, *, prompt_tiles):
    @pl.when(pl.program_id(0) == 0)
    def _():
        carry[...] = jnp.zeros(carry.shape, F32)

    o = jnp.where(pl.program_id(0) < prompt_tiles, op_ref[...], os_ref[...])
    y = jnp.dot(o, wo_ref[...], preferred_element_type=F32)
    x1 = _gated_residual(x_ref[...], gate_ref[...], y)
    x1_ref[...] = x1
    h2 = _norm_mod(x1, g2_ref[...], sh_ref[...], sc_ref[...])
    hp_ref[...] = h2
    rw = rw_ref[...]
    h_hi = h2.astype(BF16)
    h_lo = (h2 - h_hi.astype(F32)).astype(BF16)
    w_hi = rw.astype(BF16)
    w_lo = (rw - w_hi.astype(F32)).astype(BF16)
    logits = (jnp.dot(h_hi, w_hi, preferred_element_type=F32) + jnp.dot(h_lo, w_hi, preferred_element_type=F32)
              + jnp.dot(h_hi, w_lo, preferred_element_type=F32) + rb_ref[...])
    tm, ne = logits.shape
    eid = lax.broadcasted_iota(I32, (tm, ne), 1)
    lane = lax.broadcasted_iota(I32, (tm, LANES), 1)
    te = jnp.zeros((tm, LANES), I32)
    tv = jnp.full((tm, LANES), -jnp.inf, F32)
    work = logits
    picks = []
    for k in range(TOP_K):
        mx = jnp.max(work, axis=-1, keepdims=True)
        idx = jnp.min(jnp.where(work == mx, eid, ne), axis=-1, keepdims=True)
        picks.append(idx)
        te = jnp.where(lane == k, idx, te)
        tv = jnp.where(lane == k, mx, tv)
        work = jnp.where(eid == idx, -jnp.inf, work)
    ex = jnp.exp(tv - jnp.max(tv, axis=-1, keepdims=True))
    te_ref[...] = te
    tg_ref[...] = ex / jnp.sum(ex, axis=-1, keepdims=True)
    onehot = jnp.zeros((tm, LANES), F32)
    for idx in picks:
        onehot = onehot + (lane == idx).astype(F32)
    tri = (lax.broadcasted_iota(I32, (tm, tm), 1) < lax.broadcasted_iota(I32, (tm, tm), 0)).astype(BF16)
    rank = jnp.dot(tri, onehot.astype(BF16), preferred_element_type=F32) + carry[0:1, :]
    pos = jnp.zeros((tm, LANES), I32)
    for k, idx in enumerate(picks):
        pk = jnp.sum(jnp.where(lane == idx, rank, 0.0), axis=-1, keepdims=True)
        pos = jnp.where(lane == k, pk.astype(I32), pos)
    pos_ref[...] = pos
    total = carry[0:1, :] + jnp.sum(onehot, axis=0, keepdims=True)
    carry[...] = jnp.broadcast_to(total, carry.shape)
    cnt_ref[...] = carry[...]


def _post_mixer(o_p, o_s, wo, x, g2, modg, rw, rb, group):
    n, d = x.shape
    tm = ROW_TILE
    ng = tm // group
    npt = o_p.shape[0] // tm
    nst = o_s.shape[0] // tm
    row = lambda i: (i, 0)
    const = lambda i: (0, 0)
    return pl.pallas_call(
        functools.partial(_post_mixer_kernel, prompt_tiles=npt),
        grid=(n // tm,),
        in_specs=[pl.BlockSpec((tm, o_p.shape[1]), lambda i: (jnp.minimum(i, npt - 1), 0)),
                  pl.BlockSpec((tm, o_s.shape[1]), lambda i: (jnp.clip(i - npt, 0, nst - 1), 0)),
                  pl.BlockSpec(wo.shape, const),
                  pl.BlockSpec((tm, d), row),
                  pl.BlockSpec((ng, d), lambda i: (i, 2)),
                  pl.BlockSpec((1, d), const),
                  pl.BlockSpec((ng, d), lambda i: (i, 3)),
                  pl.BlockSpec((ng, d), lambda i: (i, 4)),
                  pl.BlockSpec(rw.shape, const),
                  pl.BlockSpec((1, rw.shape[1]), const)],
        out_specs=[pl.BlockSpec((tm, d), row),
                   pl.BlockSpec((tm, d), row),
                   pl.BlockSpec((tm, LANES), row),
                   pl.BlockSpec((tm, LANES), row),
                   pl.BlockSpec((tm, LANES), row),
                   pl.BlockSpec((SUBLANES, LANES), const)],
        out_shape=[jax.ShapeDtypeStruct((n, d), F32),
                   jax.ShapeDtypeStruct((n, d), F32),
                   jax.ShapeDtypeStruct((n, LANES), I32),
                   jax.ShapeDtypeStruct((n, LANES), F32),
                   jax.ShapeDtypeStruct((n, LANES), I32),
                   jax.ShapeDtypeStruct((SUBLANES, LANES), F32)],
        scratch_shapes=[pltpu.VMEM((SUBLANES, LANES), F32)],
        compiler_params=_cparams(("arbitrary",)),
        name="post_mixer",
    )(o_p, o_s, wo, x, modg, g2.reshape(1, d), modg, modg, rw, rb.reshape(1, -1))


DEST_GROUP = LANES // TOP_K


def _moe_dest_kernel(te_ref, pos_ref, cnt_ref, dest_ref, meta_ref, *, n_blocks):
    shift = MOE_TM.bit_length() - 1
    lane8 = lax.broadcasted_iota(I32, (SUBLANES, LANES), 1)
    cnt = cnt_ref[...].astype(I32)
    padded = ((cnt + (MOE_TM - 1)) >> shift) << shift
    ends = padded.astype(F32)
    s = 1
    while s < N_EXPERTS:
        ends = ends + jnp.where(lane8 >= s, pltpu.roll(ends, s, 1), 0.0)
        s *= 2
    ends_row = ends[0:1, :]
    starts_row = (ends - padded.astype(F32))[0:1, :]
    te = te_ref[...]
    pos = pos_ref[...]
    tm = te.shape[0]
    lane = lax.broadcasted_iota(I32, (tm, LANES), 1)
    dest = jnp.zeros((tm, LANES), F32)
    for k in range(TOP_K):
        sk = jnp.sum(jnp.where(lane == te[:, k:k + 1], starts_row, 0.0), axis=-1, keepdims=True)
        dest = jnp.where(lane == k, sk + pos[:, k:k + 1].astype(F32), dest)
    hi = jnp.floor(dest * (1.0 / 256.0))
    lo = dest - 256.0 * hi
    sel = (lax.broadcasted_iota(I32, (LANES, LANES), 0)
           == lax.broadcasted_iota(I32, (LANES, LANES), 1) % TOP_K).astype(BF16)
    spread = (256.0 * jnp.dot(hi.astype(BF16), sel, preferred_element_type=F32)
              + jnp.dot(lo.astype(BF16), sel, preferred_element_type=F32))
    row = lax.broadcasted_iota(I32, (tm, LANES), 0)
    keep = lane // TOP_K == row % DEST_GROUP
    dense = jnp.sum(jnp.where(keep, spread, 0.0).reshape(tm // DEST_GROUP, DEST_GROUP, LANES), axis=1)
    dest_ref[...] = dense.astype(I32)

    @pl.when(pl.program_id(0) == 0)
    def _():
        nl = meta_ref.shape[1]
        r_i = lax.broadcasted_iota(I32, (LANES, LANES), 0)
        l_i = lax.broadcasted_iota(I32, (LANES, LANES), 1)
        ends_col = jnp.sum(jnp.where(l_i == r_i, ends_row, 0.0), axis=-1, keepdims=True)
        e_i = lax.broadcasted_iota(I32, (LANES, nl), 0)
        b_i = lax.broadcasted_iota(I32, (LANES, nl), 1)
        closed = (e_i < N_EXPERTS) & (ends_col <= (b_i * MOE_TM).astype(F32))
        be = jnp.minimum(jnp.sum(jnp.where(closed, 1.0, 0.0), axis=0, keepdims=True), N_EXPERTS - 1.0)
        total = jnp.sum(jnp.where(lane8[0:1, :] == N_EXPERTS - 1, ends_row, 0.0), axis=-1, keepdims=True)
        n_used = (total.astype(I32) >> shift).astype(F32)
        meta = jnp.where(b_i[0:1, :] < n_blocks, be, n_used).astype(I32)
        meta_ref[...] = jnp.broadcast_to(meta, meta_ref.shape)


def _moe_dest(te128, pos128, cnt, n_blocks):
    n = te128.shape[0]
    tm = ROW_TILE
    nl = -(-(n_blocks + 1) // LANES) * LANES
    row = lambda i: (i, 0)
    const = lambda i: (0, 0)
    return pl.pallas_call(
        functools.partial(_moe_dest_kernel, n_blocks=n_blocks),
        grid=(n // tm,),
        in_specs=[pl.BlockSpec((tm, LANES), row),
                  pl.BlockSpec((tm, LANES), row),
                  pl.BlockSpec((SUBLANES, LANES), const)],
        out_specs=[pl.BlockSpec((tm // DEST_GROUP, LANES), row),
                   pl.BlockSpec((SUBLANES, nl), const)],
        out_shape=[jax.ShapeDtypeStruct((n // DEST_GROUP, LANES), I32),
                   jax.ShapeDtypeStruct((SUBLANES, nl), I32)],
        compiler_params=_cparams(("arbitrary",)),
        name="moe_dest",
    )(te128, pos128, cnt)


def _row_token_kernel(dest_ref, rt_ref, *, n_assign):
    def clear(r, c):
        rt_ref[r] = 0
        return c

    def place(t, c):
        for k in range(TOP_K):
            rt_ref[dest_ref[t * TOP_K + k]] = t
        return c

    lax.fori_loop(0, rt_ref.shape[0], clear, 0, unroll=8)
    lax.fori_loop(0, n_assign // TOP_K, place, 0, unroll=4)


def _row_token(dest, rows_total):
    return pl.pallas_call(
        functools.partial(_row_token_kernel, n_assign=dest.shape[0]),
        grid_spec=pltpu.PrefetchScalarGridSpec(
            num_scalar_prefetch=1,
            grid=(1,),
            in_specs=[],
            out_specs=pl.BlockSpec(memory_space=pltpu.SMEM)),
        out_shape=jax.ShapeDtypeStruct((rows_total,), I32),
        compiler_params=_cparams(("arbitrary",)),
        name="moe_row_token",
    )(dest)


def _dispatch_kernel(rt_ref, nb_ref, h_ref, o_ref, buf, sem, *, tg):
    i = pl.program_id(0)
    n_used = (nb_ref[0] * MOE_TM + tg - 1) // tg
    slot = i % 2
    unroll = 8

    def issue(blk, s):
        def body(g, c):
            for u in range(unroll):
                r = g * unroll + u
                tok = rt_ref[blk * tg + r]
                pltpu.make_async_copy(h_ref.at[pl.ds(tok, 1), :], buf.at[s, pl.ds(r, 1), :],
                                      sem.at[s]).start(priority=u % 2)
            return c
        lax.fori_loop(0, tg // unroll, body, 0)

    @pl.when(i == 0)
    def _():
        issue(0, 0)

    @pl.when(i + 1 < n_used)
    def _():
        issue(i + 1, 1 - slot)

    @pl.when(i < n_used)
    def _():
        pltpu.make_async_copy(h_ref.at[pl.ds(0, tg), :], buf.at[slot], sem.at[slot]).wait()
        o_ref[...] = buf[slot].astype(BF16)

    @pl.when(i >= n_used)
    def _():
        o_ref[...] = jnp.zeros(o_ref.shape, o_ref.dtype)


def _dispatch(row_tok, n_blocks_used, h, rows_total):
    tg = 2 * MOE_TM
    d = h.shape[1]
    return pl.pallas_call(
        functools.partial(_dispatch_kernel, tg=tg),
        grid_spec=pltpu.PrefetchScalarGridSpec(
            num_scalar_prefetch=2,
            grid=(rows_total // tg,),
            in_specs=[pl.BlockSpec(memory_space=pl.ANY)],
            out_specs=pl.BlockSpec((tg, d), lambda i, rt, nb: (i, 0)),
            scratch_shapes=[pltpu.VMEM((2, tg, d), F32), pltpu.SemaphoreType.DMA((2,))]),
        out_shape=jax.ShapeDtypeStruct((rows_total, d), BF16),
        compiler_params=_cparams(("arbitrary",)),
        name="moe_dispatch",
    )(row_tok, n_blocks_used, h)


def _new_expert(be_ref, b):
    return (b == 0) | (be_ref[b] != be_ref[jnp.maximum(b - 1, 0)])


def _swiglu_pairs(v):
    g = jnp.minimum(v, SWIGLU_LIMIT)
    glu = g * jax.nn.sigmoid(g * SWIGLU_ALPHA)
    up1 = jnp.clip(v, -SWIGLU_LIMIT, SWIGLU_LIMIT) + 1.0
    return glu, up1


WEIGHT_DMA_PRIORITY = 1


GU_COL_CHUNK = 256
MOE_SUB_GU = 2
MOE_SUB_DOWN = 2


def _stream_expert_weights(be_ref, nb_ref, run_ctr, copies, consume, b):
    j = pl.program_id(0)
    nj = pl.num_programs(0)
    nb = nb_ref[0]
    last_blk = be_ref.shape[0] - 1
    e = be_ref[b]

    @pl.when((j == 0) & (b == 0))
    def _():
        run_ctr[0] = 0
        for c in copies(0, e, 0):
            c.start(priority=WEIGHT_DMA_PRIORITY)

    @pl.when((b < nb) & _new_expert(be_ref, b))
    def _():
        k = run_ctr[0]
        slot = k % 2
        for c in copies(j, e, slot):
            c.wait()
        consume(slot)
        run_end = lax.while_loop(lambda bb: (bb < nb) & (be_ref[jnp.minimum(bb, last_blk)] == e),
                                 lambda bb: bb + 1, b + 1)
        more_runs = run_end < nb
        j_next = jnp.where(more_runs, j, j + 1)
        e_next = jnp.where(more_runs, be_ref[jnp.minimum(run_end, last_blk)], be_ref[0])

        @pl.when(more_runs | (j + 1 < nj))
        def _():
            for c in copies(j_next, e_next, 1 - slot):
                c.start(priority=WEIGHT_DMA_PRIORITY)
        run_ctr[0] = k + 1


def _moe_gu_kernel(be_ref, nb_ref, xb_ref, w_ref, bias_ref, o_ref, wa_s, wb_s, wbuf, sem, run_ctr,
                   *, layer, tn, nj):
    j = pl.program_id(0)
    nb = nb_ref[0]
    tm = MOE_TM

    def copies(jj, e, s):
        return [pltpu.make_async_copy(
            w_ref.at[layer, e, :, pl.ds(pl.multiple_of((jj + h * nj) * tn, tn), tn)], wbuf.at[s, h], sem.at[s])
            for h in range(2)]

    def consume(s):
        wa_s[...] = wbuf[s, 0].astype(BF16)
        wb_s[...] = wbuf[s, 1].astype(BF16)

    def block(u):
        b = pl.program_id(1) * MOE_SUB_GU + u
        rows = slice(u * tm, (u + 1) * tm)
        _stream_expert_weights(be_ref, nb_ref, run_ctr, copies, consume, b)

        @pl.when(b < nb)
        def _():
            x = xb_ref[rows, :]
            brow = be_ref[b] * (2 * nj) + j
            even = lax.broadcasted_iota(I32, (tm, LANES), 1) % 2 == 0
            cw = GU_COL_CHUNK
            for cc in range(tn // cw):
                cs = slice(cc * cw, (cc + 1) * cw)
                ga = jnp.dot(x, wa_s[:, cs], preferred_element_type=F32) + bias_ref[pl.ds(brow, 1), cs]
                gb = jnp.dot(x, wb_s[:, cs], preferred_element_type=F32) + bias_ref[pl.ds(brow + nj, 1), cs]
                for c in range(cw // LANES):
                    glu_a, up_a = _swiglu_pairs(ga[:, c * LANES:(c + 1) * LANES])
                    glu_b, up_b = _swiglu_pairs(gb[:, c * LANES:(c + 1) * LANES])
                    ra = glu_a * pltpu.roll(up_a, LANES - 1, 1)
                    rb = pltpu.roll(glu_b, 1, 1) * up_b
                    lo = cc * cw + c * LANES
                    o_ref[rows, lo:lo + LANES] = jnp.where(even, ra, rb).astype(BF16)

        @pl.when(b >= nb)
        def _():
            o_ref[rows, :] = jnp.zeros((tm, tn), o_ref.dtype)

    for u in range(MOE_SUB_GU):
        block(u)


def _moe_gu(block_e, n_blocks_used, xb, wgu_all, layer, bgu):
    rows, d = xb.shape
    ne, f2 = bgu.shape
    tm = MOE_TM
    tn = 1024
    nj = f2 // 2 // tn
    bias = bgu.reshape(ne * 2 * nj, tn)
    return pl.pallas_call(
        functools.partial(_moe_gu_kernel, layer=layer, tn=tn, nj=nj),
        grid_spec=pltpu.PrefetchScalarGridSpec(
            num_scalar_prefetch=2,
            grid=(nj, rows // (tm * MOE_SUB_GU)),
            in_specs=[pl.BlockSpec((tm * MOE_SUB_GU, d), lambda j, b, be, nb: (b, 0)),
                      pl.BlockSpec(memory_space=pl.ANY),
                      pl.BlockSpec(bias.shape, lambda j, b, be, nb: (0, 0))],
            out_specs=pl.BlockSpec((tm * MOE_SUB_GU, tn), lambda j, b, be, nb: (b, j)),
            scratch_shapes=[pltpu.VMEM((d, tn), BF16), pltpu.VMEM((d, tn), BF16),
                            pltpu.VMEM((2, 2, d, tn), F32), pltpu.SemaphoreType.DMA((2,)),
                            pltpu.SMEM((1,), I32)]),
        out_shape=jax.ShapeDtypeStruct((rows, f2 // 2), BF16),
        compiler_params=_cparams(("arbitrary", "arbitrary")),
        name="moe_gate_up",
    )(block_e, n_blocks_used, xb, wgu_all, bias)


def _moe_down_kernel(be_ref, nb_ref, a_ref, w_ref, bd_ref, o_ref, wp_s, stage, wbuf, sem, run_ctr, *, layer, tn):
    nb = nb_ref[0]
    tm = MOE_TM

    def copies(j, e, slot):
        return [pltpu.make_async_copy(w_ref.at[layer, e, :, pl.ds(pl.multiple_of(j * tn, tn), tn)], wbuf.at[slot],
                                      sem.at[slot])]

    def consume(slot):
        f = wbuf.shape[1]
        ns = stage.shape[0]
        for c in range(tn // LANES):
            cols = slice(c * LANES, (c + 1) * LANES)
            stage[c % ns, pl.ds(0, f // 2, stride=2), :] = wbuf[slot, :f // 2, cols]
            stage[c % ns, pl.ds(1, f // 2, stride=2), :] = wbuf[slot, f // 2:, cols]
            wp_s[:, cols] = stage[c % ns].astype(BF16)

    def block(u):
        b = pl.program_id(1) * MOE_SUB_DOWN + u
        rows = slice(u * tm, (u + 1) * tm)
        _stream_expert_weights(be_ref, nb_ref, run_ctr, copies, consume, b)

        @pl.when(b < nb)
        def _():
            brow = be_ref[b] * pl.num_programs(0) + pl.program_id(0)
            o_ref[rows, :] = (jnp.dot(a_ref[rows, :], wp_s[...], preferred_element_type=F32)
                              + bd_ref[pl.ds(brow, 1), :])

        @pl.when(b >= nb)
        def _():
            o_ref[rows, :] = jnp.zeros((tm, tn), o_ref.dtype)

    for u in range(MOE_SUB_DOWN):
        block(u)


def _moe_down(block_e, n_blocks_used, act, wd_all, layer, bd):
    rows, f = act.shape
    ne, d = bd.shape
    tm = MOE_TM
    tn = d
    bias = bd.reshape(ne * (d // tn), tn)
    return pl.pallas_call(
        functools.partial(_moe_down_kernel, layer=layer, tn=tn),
        grid_spec=pltpu.PrefetchScalarGridSpec(
            num_scalar_prefetch=2,
            grid=(d // tn, rows // (tm * MOE_SUB_DOWN)),
            in_specs=[pl.BlockSpec((tm * MOE_SUB_DOWN, f), lambda j, b, be, nb: (b, 0)),
                      pl.BlockSpec(memory_space=pl.ANY),
                      pl.BlockSpec(bias.shape, lambda j, b, be, nb: (0, 0))],
            out_specs=pl.BlockSpec((tm * MOE_SUB_DOWN, tn), lambda j, b, be, nb: (b, j)),
            scratch_shapes=[pltpu.VMEM((f, tn), BF16), pltpu.VMEM((2, f, LANES), F32),
                            pltpu.VMEM((2, f, tn), F32), pltpu.SemaphoreType.DMA((2,)), pltpu.SMEM((1,), I32)]),
        out_shape=jax.ShapeDtypeStruct((rows, d), F32),
        compiler_params=_cparams(("arbitrary", "arbitrary"), VMEM_LIMIT_DOWN),
        name="moe_down",
    )(block_e, n_blocks_used, act, wd_all, bias)


def _combine_kernel(dest_ref, y_ref, x1_ref, gm_ref, tg_ref, fg_ref, o_ref, buf, sem, *, tn, tile0, final_norm):
    i = pl.program_id(0) + tile0

    def issue(t, c):
        for k in range(TOP_K):
            d = dest_ref[(i * tn + t) * TOP_K + k]
            pltpu.make_async_copy(y_ref.at[pl.ds(d, 1), :], buf.at[k, pl.ds(t, 1), :], sem).start(priority=k % 2)
        return c

    lax.fori_loop(0, tn, issue, 0, unroll=8)
    for k in range(TOP_K):
        pltpu.make_async_copy(y_ref.at[pl.ds(0, tn), :], buf.at[k], sem).wait()
    tg = tg_ref[...]
    moe = tg[:, 0:1] * buf[0]
    for k in range(1, TOP_K):
        moe = moe + tg[:, k:k + 1] * buf[k]
    x2 = _gated_residual(x1_ref[...], gm_ref[...], moe)
    if final_norm:
        x2 = _rms(x2, fg_ref[...])
    o_ref[...] = x2


def _combine(dest, y, x1, modg, tgates, fg, group, final_norm, row0, rows):
    d = x1.shape[1]
    tn = ROW_TILE
    ng = tn // group
    t0 = row0 // tn
    return pl.pallas_call(
        functools.partial(_combine_kernel, tn=tn, tile0=t0, final_norm=final_norm),
        grid_spec=pltpu.PrefetchScalarGridSpec(
            num_scalar_prefetch=1,
            grid=(rows // tn,),
            in_specs=[pl.BlockSpec(memory_space=pl.ANY),
                      pl.BlockSpec((tn, d), lambda i, ds: (i + t0, 0)),
                      pl.BlockSpec((ng, d), lambda i, ds: (i + t0, 5)),
                      pl.BlockSpec((tn, LANES), lambda i, ds: (i + t0, 0)),
                      pl.BlockSpec((1, d), lambda i, ds: (0, 0))],
            out_specs=pl.BlockSpec((tn, d), lambda i, ds: (i, 0)),
            scratch_shapes=[pltpu.VMEM((TOP_K, tn, d), F32), pltpu.SemaphoreType.DMA(())]),
        out_shape=jax.ShapeDtypeStruct((rows, d), F32),
        compiler_params=_cparams(("arbitrary",)),
        name="moe_combine",
    )(dest, y, x1, modg, tgates, fg.reshape(1, d))


def _moe(hp, te128, tg128, pos128, cnt, x1, modg, wgu_all, bgu, wd_all, bd, layer, fg, group, splits):
    n = x1.shape[0]
    rows_total = n * TOP_K + N_EXPERTS * MOE_TM
    n_blocks = rows_total // MOE_TM
    dest2d, meta = _moe_dest(te128, pos128, cnt, n_blocks)
    dest = dest2d.reshape(-1)
    block_e = meta[0, :n_blocks]
    nbu = meta[0, n_blocks:n_blocks + 1]
    row_tok = _row_token(dest, rows_total)
    xb = _dispatch(row_tok, nbu, hp, rows_total)
    act = _moe_gu(block_e, nbu, xb, wgu_all, layer, bgu)
    y = _moe_down(block_e, nbu, act, wd_all, layer, bd)
    return [_combine(dest, y, x1, modg, tg128, fg, group, fn, r0, rows) for r0, rows, fn in splits]


def _lru_in_kernel(x_ref, g1_ref, sh_ref, sc_ref, wy_ref, wx_ref, y_ref, xb_ref):
    h = _norm_mod(x_ref[...], g1_ref[...], sh_ref[...], sc_ref[...]).astype(BF16)
    y = jnp.dot(h, wy_ref[...], preferred_element_type=F32)
    y_ref[...] = jax.nn.gelu(y, approximate=True).astype(BF16)
    xb_ref[...] = jnp.dot(h, wx_ref[...], preferred_element_type=F32)


def _lru_in(x, g1, modg, wy, wx, group):
    n, d = x.shape
    dr = wy.shape[1]
    tm = ROW_TILE
    ng = tm // group
    row = lambda i: (i, 0)
    const = lambda i: (0, 0)
    return pl.pallas_call(
        _lru_in_kernel,
        grid=(n // tm,),
        in_specs=[pl.BlockSpec((tm, d), row),
                  pl.BlockSpec((1, d), const),
                  pl.BlockSpec((ng, d), lambda i: (i, 0)),
                  pl.BlockSpec((ng, d), lambda i: (i, 1)),
                  pl.BlockSpec((d, dr), const),
                  pl.BlockSpec((d, dr), const)],
        out_specs=[pl.BlockSpec((tm, dr), row), pl.BlockSpec((tm, dr), row)],
        out_shape=[jax.ShapeDtypeStruct((n, dr), BF16), jax.ShapeDtypeStruct((n, dr), F32)],
        compiler_params=_cparams(("arbitrary",)),
        name="lru_in",
    )(x, g1.reshape(1, d), modg, modg, wy, wx)


def _lru_scan_kernel(y_ref, xb_ref, cb_ref, h0_ref, cw_ref, cbias_ref, wa_ref, ba_ref, wx_ref, bx_ref, lam_ref,
                     hy_ref, cbo_ref, ho_ref, xe, a_s, u_s, hc, *, tc, starts_at_pos0):
    c = pl.program_id(1)
    dr = xb_ref.shape[1]
    nb = wa_ref.shape[0]
    bd = dr // nb
    pre = SUBLANES

    @pl.when(c == 0)
    def _():
        xe[0:pre, :] = jnp.zeros((pre, dr), F32)
        xe[pre - (CONV_W - 1):pre, :] = cb_ref[0]
        hc[...] = jnp.broadcast_to(h0_ref[0], (SUBLANES, dr))

    xe[pre:pre + tc, :] = xb_ref[...]
    cw = cw_ref[...]
    xc = cbias_ref[...] + xe[pre:pre + tc, :] * cw[CONV_W - 1:CONV_W, :]
    for k in range(1, CONV_W):
        xc = xc + xe[pre - k:pre - k + tc, :] * cw[CONV_W - 1 - k:CONV_W - k, :]
    cbo_ref[0] = xe[pre + tc - (CONV_W - 1):pre + tc, :]
    xe[0:pre, :] = xe[tc:tc + pre, :]

    xcb = xc.astype(BF16)
    ra = jnp.concatenate([jnp.dot(xcb[:, n * bd:(n + 1) * bd], wa_ref[n], preferred_element_type=F32)
                          for n in range(nb)], axis=1)
    rx = jnp.concatenate([jnp.dot(xcb[:, n * bd:(n + 1) * bd], wx_ref[n], preferred_element_type=F32)
                          for n in range(nb)], axis=1)
    r = jax.nn.sigmoid(ra + ba_ref[...])
    ig = jax.nn.sigmoid(rx + bx_ref[...])
    lam = lam_ref[...]
    log_sig = jnp.minimum(lam, 0.0) - jnp.log1p(jnp.exp(-jnp.abs(lam)))
    log_a = LRU_C * r * log_sig
    a = jnp.exp(log_a)
    th = jnp.tanh(log_a)
    mult = jnp.sqrt(-2.0 * th / (1.0 - th))
    if starts_at_pos0:
        first = (lax.broadcasted_iota(I32, (tc, 1), 0) == 0) & (c == 0)
        mult = jnp.where(first, 1.0, mult)
    a_s[...] = a
    u_s[...] = mult * ig * xc

    row8 = lax.broadcasted_iota(I32, (SUBLANES, dr), 0)

    def group_step(g, hprev):
        off = pl.multiple_of(g * SUBLANES, SUBLANES)
        aa = a_s[pl.ds(off, SUBLANES), :]
        uu = u_s[pl.ds(off, SUBLANES), :]
        s = 1
        while s < SUBLANES:
            m = row8 >= s
            uu = jnp.where(m, aa * pltpu.roll(uu, s, 0) + uu, uu)
            aa = jnp.where(m, aa * pltpu.roll(aa, s, 0), aa)
            s *= 2
        hh = aa * hprev + uu
        u_s[pl.ds(off, SUBLANES), :] = hh
        return jnp.broadcast_to(hh[SUBLANES - 1:SUBLANES, :], (SUBLANES, dr))

    hlast = lax.fori_loop(0, tc // SUBLANES, group_step, hc[...])
    hc[...] = hlast
    ho_ref[0] = hlast[0:1, :]
    hy_ref[...] = (u_s[...] * y_ref[...].astype(F32)).astype(BF16)


def _lru_scan(yb, xb, conv_buf, h0, cw, cbias, wa, ba, wx, bx, lam, row0, n_seq, t, tc, starts_at_pos0):
    dr = xb.shape[1]
    nc = t // tc
    rb0 = row0 // tc
    inmap = lambda s, c: (rb0 + s * nc + c, 0)
    outmap = lambda s, c: (s * nc + c, 0)
    const2 = lambda s, c: (0, 0)
    const3 = lambda s, c: (0, 0, 0)
    seq3 = lambda s, c: (s, 0, 0)
    return pl.pallas_call(
        functools.partial(_lru_scan_kernel, tc=tc, starts_at_pos0=starts_at_pos0),
        grid=(n_seq, nc),
        in_specs=[pl.BlockSpec((tc, dr), inmap),
                  pl.BlockSpec((tc, dr), inmap),
                  pl.BlockSpec((1, CONV_W - 1, dr), seq3),
                  pl.BlockSpec((1, 1, dr), seq3),
                  pl.BlockSpec((CONV_W, dr), const2),
                  pl.BlockSpec((1, dr), const2),
                  pl.BlockSpec(wa.shape, const3),
                  pl.BlockSpec((1, dr), const2),
                  pl.BlockSpec(wx.shape, const3),
                  pl.BlockSpec((1, dr), const2),
                  pl.BlockSpec((1, dr), const2)],
        out_specs=[pl.BlockSpec((tc, dr), outmap),
                   pl.BlockSpec((1, CONV_W - 1, dr), seq3),
                   pl.BlockSpec((1, 1, dr), seq3)],
        out_shape=[jax.ShapeDtypeStruct((n_seq * t, dr), BF16),
                   jax.ShapeDtypeStruct((n_seq, CONV_W - 1, dr), F32),
                   jax.ShapeDtypeStruct((n_seq, 1, dr), F32)],
        scratch_shapes=[pltpu.VMEM((SUBLANES + tc, dr), F32),
                        pltpu.VMEM((tc, dr), F32),
                        pltpu.VMEM((tc, dr), F32),
                        pltpu.VMEM((SUBLANES, dr), F32)],
        compiler_params=_cparams(("arbitrary", "arbitrary")),
        name="lru_scan",
    )(yb, xb, conv_buf, h0.reshape(n_seq, 1, dr), cw, cbias.reshape(1, dr), wa, ba.reshape(1, dr),
      wx, bx.reshape(1, dr), lam.reshape(1, dr))


def _rope_tables(pos):
    half = ROPE_DIM // 2
    inv = 1.0 / (ROPE_THETA ** (jnp.arange(0, ROPE_DIM, 2, dtype=F32) / ROPE_DIM))
    ang = pos.astype(F32)[:, None] * inv[None, :]
    cos, sin = jnp.cos(ang), jnp.sin(ang)
    z = jnp.zeros((pos.shape[0], LANES - ROPE_DIM), F32)
    return jnp.concatenate([cos, cos, z], axis=1), jnp.concatenate([-sin, sin, z], axis=1)


def _head_slab_weights(w_uq):
    ql = w_uq.shape[0]
    w = w_uq.reshape(ql, N_HEADS, NOPE_DIM + ROPE_DIM)
    z = jnp.zeros((ql, N_HEADS, HEAD_W - NOPE_DIM - ROPE_DIM), w.dtype)
    return jnp.concatenate([w, z], axis=2).reshape(ql, N_HEADS * HEAD_W).astype(BF16)


def kernel(x_prompt, x_sample, cache_ckv, cache_krope, state_conv, state_h, c_prompt, c_sample,
           mod_w, mod_b, norm1_g, norm2_g,
           mla_w_in, mla_q_norm_g, mla_kv_norm_g, mla_w_uq, mla_w_ukv, mla_w_o,
           lru_w_in, lru_conv_w, lru_conv_b, lru_w_a, lru_b_a, lru_w_x, lru_b_x, lru_lambda, lru_w_o,
           router_w, router_b, moe_w_gu, moe_b_gu, moe_w_down, moe_b_down, final_g):
    bp, tp, d = x_prompt.shape
    bs, ts, _ = x_sample.shape
    past = cache_ckv.shape[2]
    depth = mod_w.shape[0]
    n_p, n_s = bp * tp, bs * ts
    n = n_p + n_s
    group = math.gcd(tp, ts)
    assert group % SUBLANES == 0 and ROW_TILE % group == 0 and n_p % ROW_TILE == 0 and n_s % ROW_TILE == 0
    assert NOPE_DIM == LANES and V_DIM == LANES and ROPE_DIM <= LANES and n_p % n_s == 0

    x = jnp.concatenate([x_prompt.reshape(n_p, d), x_sample.reshape(n_s, d)], axis=0)
    nb = bp + bs
    nb_pad = -(-nb // SUBLANES) * SUBLANES
    c_all = jnp.concatenate([c_prompt, c_sample, jnp.zeros((nb_pad - nb, d), F32)], axis=0)
    grp_batch = np.concatenate([np.repeat(np.arange(bp), tp // group), bp + np.repeat(np.arange(bs), ts // group)])
    assert tp % ROW_TILE == 0 and ROW_TILE % ts == 0
    pos = jnp.concatenate([jnp.arange(tp), jnp.tile(past + jnp.arange(ts), ROW_TILE // ts)])
    cos, sin = _rope_tables(pos)

    outs = {k: [] for k in ("ckv_p", "kr_p", "conv_p", "h_p", "ckv_s", "kr_s", "conv_s", "h_s")}
    for i in range(depth):
        mod = _adaln(c_all, mod_w, i, mod_b[i])
        modg = jnp.take(mod, jnp.asarray(grp_batch), axis=0)
        j = i // 2
        if i % 2 == 0:
            w_in = mla_w_in[j]
            zpad = jnp.zeros((d, LANES - ROPE_DIM), F32)
            win = jnp.concatenate([w_in, zpad], axis=1).astype(BF16)
            wuq = _head_slab_weights(mla_w_uq[j])
            wukv = mla_w_ukv[j].reshape(KV_LORA, N_HEADS, NOPE_DIM + V_DIM)
            wk = wukv[:, :, :NOPE_DIM].reshape(KV_LORA, N_HEADS * NOPE_DIM).astype(BF16)
            wv = wukv[:, :, NOPE_DIM:].reshape(KV_LORA, N_HEADS * V_DIM).astype(BF16)
            q, ckv, kr, ckvb, krp = _mla_proj(x, norm1_g[i], modg, win, mla_q_norm_g[j], mla_kv_norm_g[j],
                                              wuq, cos, sin, group, n_p // ROW_TILE, tp // ROW_TILE)
            kk, vv = _kv_expand(ckvb, krp, wk, wv, n_p)
            o_p = _attn_prompt(q, kk, vv, bp, tp)
            qa = _absorb(q, wk, n_p, n_s)
            ckr_pad = jnp.pad(cache_krope[j], ((0, 0), (0, 0), (0, LANES - ROPE_DIM)))
            o_lat = _attn_sample(qa, q, cache_ckv[j], ckr_pad, ckvb, krp, n_p, bs, ts, past)
            o_s = _unabsorb(o_lat, wv, n_s)
            wo = mla_w_o[j].astype(BF16)
            outs["ckv_p"].append(ckv[:n_p].reshape(bp, tp, KV_LORA))
            outs["kr_p"].append(kr[:n_p].reshape(bp, tp, ROPE_DIM))
            outs["ckv_s"].append(ckv[n_p:].reshape(bs, ts, KV_LORA))
            outs["kr_s"].append(kr[n_p:].reshape(bs, ts, ROPE_DIM))
        else:
            dr = lru_w_in.shape[2] // 2
            wy = lru_w_in[j][:, :dr].astype(BF16)
            wx = lru_w_in[j][:, dr:].astype(BF16)
            yb, xb = _lru_in(x, norm1_g[i], modg, wy, wx, group)
            wa = lru_w_a[j].astype(BF16)
            wxg = lru_w_x[j].astype(BF16)
            lru_args = (lru_conv_w[j], lru_conv_b[j], wa, lru_b_a[j], wxg, lru_b_x[j], lru_lambda[j])
            zbuf = jnp.zeros((bp, CONV_W - 1, dr), F32)
            zh = jnp.zeros((bp, dr), F32)
            o_p, cb_p, h_p = _lru_scan(yb, xb, zbuf, zh, *lru_args, 0, bp, tp, ROW_TILE, True)
            o_s, cb_s, h_s = _lru_scan(yb, xb, state_conv[j], state_h[j], *lru_args, n_p, bs, ts, ts, False)
            wo = lru_w_o[j].astype(BF16)
            outs["conv_p"].append(cb_p)
            outs["h_p"].append(h_p.reshape(bp, dr))
            outs["conv_s"].append(cb_s)
            outs["h_s"].append(h_s.reshape(bs, dr))
        x1, hp, te128, tg128, pos128, cnt = _post_mixer(o_p, o_s, wo, x, norm2_g[i], modg, router_w[i], router_b[i],
                                                        group)
        last = i == depth - 1
        splits = [(0, n_p, True), (n_p, n_s, True)] if last else [(0, n, False)]
        res = _moe(hp, te128, tg128, pos128, cnt, x1, modg, moe_w_gu, moe_b_gu[i], moe_w_down, moe_b_down[i], i,
                   final_g, group, splits)
        x = res[0]
    y_prompt = res[0].reshape(bp, tp, d)
    y_sample = res[1].reshape(bs, ts, d)
    return (y_prompt, y_sample,
            jnp.stack(outs["ckv_p"]), jnp.stack(outs["kr_p"]), jnp.stack(outs["conv_p"]), jnp.stack(outs["h_p"]),
            jnp.stack(outs["ckv_s"]), jnp.stack(outs["kr_s"]), jnp.stack(outs["conv_s"]), jnp.stack(outs["h_s"]))
```
